```python
import jax, jax.numpy as jnp
from jax import lax
import numpy as np

D_MODEL = 1024
BATCH = 2
SEQ = 16384
DEPTH = 1
DEC_BATCH = 32
DEC_SEQ = 2048
PAST_LEN = 128

N_META = 16
GRID_W = 64
HA_Q = 8
HA_KV = 2
HEAD_DIM = 64
WINDOW = 128
BLOCK = 128
HB = 8
NB_ROWS_MAX = 8
NB_COLS = 16
DA_Q = HA_Q * HEAD_DIM
DA_KV = HA_KV * HEAD_DIM
DB = HB * HEAD_DIM
Q_A_END = DA_Q
Q_B_END = Q_A_END + DB
G_A_END = Q_B_END + D_MODEL
G_B_END = G_A_END + D_MODEL
D_IN = G_B_END + 2 * DA_KV + 2 * DB
N_EXPERTS = 32
TOP_K = 4
D_FF = D_MODEL
SWIGLU_ALPHA = 1.702
SWIGLU_LIMIT = 7.0
MOE_BLOCK = 256
NORM_EPS = 1e-5
NEG_INF = -1e30

kernel_name = 'hybrid_meta_window_natten_moe_encoder'


def rms_norm(x, g):
    xf = x.astype(jnp.float32)
    y = xf * lax.rsqrt(jnp.mean(xf * xf, axis=-1, keepdims=True) + NORM_EPS)
    return y.astype(x.dtype) * g


def alibi_slopes():
    return jnp.exp2(-8.0 * jnp.arange(1, HA_Q + 1, dtype=jnp.float32) / HA_Q)


def softmax_with_sink(pieces, sink_b):
    s = jnp.concatenate(pieces, axis=-1)
    sink = jnp.broadcast_to(sink_b.astype(jnp.float32), s.shape[:-1] + (1,))
    return jax.nn.softmax(jnp.concatenate([s, sink], axis=-1), axis=-1)[..., :-1]


def split_kv(z, B, L):
    kA = z[..., :DA_KV].reshape(B, L, HA_KV, HEAD_DIM)
    vA = z[..., DA_KV:2 * DA_KV].reshape(B, L, HA_KV, HEAD_DIM)
    kB = z[..., 2 * DA_KV:2 * DA_KV + DB].reshape(B, L, HB, HEAD_DIM)
    vB = z[..., 2 * DA_KV + DB:].reshape(B, L, HB, HEAD_DIM)
    return kA, vA, kB, vB


def window_attention(q, k, v, k_meta, v_meta, sink):
    B, N = q.shape[:2]
    M = k_meta.shape[1]
    nb = N // BLOCK
    G = HA_Q // HA_KV
    scale = HEAD_DIM ** -0.5
    qb = q.reshape(B, nb, BLOCK, HA_KV, G, HEAD_DIM)

    def band(a):
        a = jnp.pad(a, ((0, 0), (BLOCK, BLOCK), (0, 0), (0, 0)))
        a = a.reshape(B, nb + 2, BLOCK, HA_KV, HEAD_DIM)
        return jnp.concatenate([a[:, :-2], a[:, 1:-1], a[:, 2:]], axis=2)

    kb, vb = band(k), band(v)
    s_win = jnp.einsum('bnqhgd,bnkhd->bnhgqk', qb, kb,
                       preferred_element_type=jnp.float32) * scale
    rel = jnp.arange(3 * BLOCK)[None, :] - BLOCK - jnp.arange(BLOCK)[:, None]
    dist = jnp.abs(rel)
    kpos = (jnp.arange(nb)[:, None] - 1) * BLOCK + jnp.arange(3 * BLOCK)[None, :]
    in_seq = (kpos >= 0) & (kpos < N)
    valid = (dist <= WINDOW)[None] & in_seq[:, None, :]
    slopes = alibi_slopes().reshape(HA_KV, G)
    alibi = -slopes[:, :, None, None] * dist.astype(jnp.float32)
    s_win = jnp.where(valid[None, :, None, None], s_win + alibi, NEG_INF)
    s_meta = jnp.einsum('bnqhgd,bmhd->bnhgqm', qb, k_meta,
                        preferred_element_type=jnp.float32) * scale
    p = softmax_with_sink([s_win, s_meta], sink.reshape(HA_KV, G)[None, None, :, :, None, None])
    p_win = p[..., :3 * BLOCK].astype(v.dtype)
    p_meta = p[..., 3 * BLOCK:].astype(v.dtype)
    o = (jnp.einsum('bnhgqk,bnkhd->bnqhgd', p_win, vb)
         + jnp.einsum('bnhgqm,bmhd->bnqhgd', p_meta, v_meta))
    return o.reshape(B, N, DA_Q)


def window_attention_meta(qm, k, v, k_meta, v_meta, sink):
    B, M = qm.shape[:2]
    G = HA_Q // HA_KV
    scale = HEAD_DIM ** -0.5
    kc = jnp.concatenate([k_meta, k[:, :BLOCK]], axis=1)
    vc = jnp.concatenate([v_meta, v[:, :BLOCK]], axis=1)
    dist = jnp.abs(jnp.arange(M + BLOCK)[None, :] - jnp.arange(M)[:, None])
    slopes = alibi_slopes().reshape(HA_KV, G)
    s = jnp.einsum('bqhgd,bkhd->bhgqk', qm, kc, preferred_element_type=jnp.float32) * scale
    s = jnp.where(dist <= WINDOW, s - slopes[:, :, None, None] * dist.astype(jnp.float32), NEG_INF)
    p = softmax_with_sink([s], sink.reshape(HA_KV, G)[None, :, :, None, None]).astype(vc.dtype)
    return jnp.einsum('bhgqk,bkhd->bqhgd', p, vc).reshape(B, M, DA_Q)


def neighbourhood_attention(q, k, v, k_meta, v_meta, rpb):
    B, N = q.shape[:2]
    rows = N // GRID_W
    kh = min(NB_ROWS_MAX, rows)
    scale = HEAD_DIM ** -0.5

    def to_grid(a):
        return a.reshape(B, rows, GRID_W, HB, HEAD_DIM).transpose(0, 3, 1, 2, 4)

    qg, kg, vg = to_grid(q), to_grid(k), to_grid(v)
    cols = np.arange(GRID_W)
    col_start = np.clip(cols - NB_COLS // 2, 0, GRID_W - NB_COLS)
    col_idx = col_start[:, None] + np.arange(NB_COLS)[None, :]
    col_rel = col_idx - cols[:, None] + NB_COLS - 1
    rpb_cols = rpb[:, :, col_rel]

    def row_block(r):
        rs = jnp.clip(r - kh // 2, 0, rows - kh)
        q_r = lax.dynamic_index_in_dim(qg, r, axis=2, keepdims=False)
        k_nb = lax.dynamic_slice_in_dim(kg, rs, kh, axis=2)[:, :, :, col_idx]
        v_nb = lax.dynamic_slice_in_dim(vg, rs, kh, axis=2)[:, :, :, col_idx]
        s_nb = jnp.einsum('bhcd,bhrckd->bhcrk', q_r, k_nb,
                          preferred_element_type=jnp.float32) * scale
        row_rel = rs + jnp.arange(kh) - r + NB_ROWS_MAX - 1
        bias = jnp.take(rpb_cols, row_rel, axis=1).transpose(0, 2, 1, 3)
        s_nb = s_nb + bias[None].astype(jnp.float32)
        s_m = jnp.einsum('bhcd,bmhd->bhcm', q_r, k_meta,
                         preferred_element_type=jnp.float32) * scale
        p = jax.nn.softmax(jnp.concatenate(
            [s_nb.reshape(B, HB, GRID_W, kh * NB_COLS), s_m], axis=-1), axis=-1)
        p_nb = p[..., :kh * NB_COLS].reshape(B, HB, GRID_W, kh, NB_COLS).astype(v.dtype)
        p_m = p[..., kh * NB_COLS:].astype(v.dtype)
        return (jnp.einsum('bhcrk,bhrckd->bhcd', p_nb, v_nb)
                + jnp.einsum('bhcm,bmhd->bhcd', p_m, v_meta))

    o = lax.map(row_block, jnp.arange(rows))
    return o.transpose(1, 0, 3, 2, 4).reshape(B, N, DB)


def meta_attention(qm, k_meta, v_meta):
    B, M = qm.shape[:2]
    s = jnp.einsum('bqhd,bkhd->bhqk', qm, k_meta, preferred_element_type=jnp.float32) * HEAD_DIM ** -0.5
    p = jax.nn.softmax(s, axis=-1).astype(v_meta.dtype)
    return jnp.einsum('bhqk,bkhd->bqhd', p, v_meta).reshape(B, M, DB)


def merge_branches(z, oA, oB, w_branch_a, w_branch_b, w_out):
    gate_a = jax.nn.sigmoid(z[..., Q_B_END:G_A_END])
    gate_b = jax.nn.sigmoid(z[..., G_A_END:G_B_END])
    return (gate_a * (oA @ w_branch_a) + gate_b * (oB @ w_branch_b)) @ w_out


def clamped_swiglu(gu):
    glu = jnp.minimum(gu[..., 0::2], SWIGLU_LIMIT)
    lin = jnp.clip(gu[..., 1::2], -SWIGLU_LIMIT, SWIGLU_LIMIT)
    return glu * jax.nn.sigmoid(SWIGLU_ALPHA * glu) * (lin + 1.0)


def moe_ffn(h, router_w, router_b, w_gate_up, b_gate_up, w_down, b_down):
    T = h.shape[0]
    A = T * TOP_K
    logits = (h @ router_w + router_b).astype(jnp.float32)
    top_val, top_idx = lax.top_k(logits, TOP_K)
    gate = jax.nn.softmax(top_val, axis=-1)
    flat_e = top_idx.reshape(-1)
    order = jnp.argsort(flat_e, stable=True)
    sorted_e = flat_e[order]
    counts = jnp.bincount(flat_e, length=N_EXPERTS)
    padded = (counts + MOE_BLOCK - 1) // MOE_BLOCK * MOE_BLOCK
    pad_end = jnp.cumsum(padded)
    pad_start = pad_end - padded
    grp_start = jnp.cumsum(counts) - counts
    slot_sorted = pad_start[sorted_e] + jnp.arange(A) - grp_start[sorted_e]
    slot = jnp.zeros((A,), slot_sorted.dtype).at[order].set(slot_sorted)
    n_blocks = -(-A // MOE_BLOCK) + N_EXPERTS
    n_slots = n_blocks * MOE_BLOCK
    slot_tok = jnp.zeros((n_slots,), jnp.int32).at[slot].set(jnp.arange(A, dtype=jnp.int32) // TOP_K)
    x_slots = h[slot_tok].reshape(n_blocks, MOE_BLOCK, h.shape[1])
    block_e = jnp.minimum(jnp.searchsorted(pad_end, jnp.arange(n_blocks) * MOE_BLOCK, side='right'),
                          N_EXPERTS - 1)

    def expert_block(args):
        xb, e = args
        act = clamped_swiglu(xb @ w_gate_up[e] + b_gate_up[e])
        return act @ w_down[e] + b_down[e]

    y_slots = lax.map(expert_block, (x_slots, block_e)).reshape(n_slots, h.shape[1])
    y = y_slots[slot].reshape(T, TOP_K, h.shape[1])
    return jnp.einsum('tk,tkd->td', gate.astype(y.dtype), y)


def encoder_layer(m, x, norm1_g, w_in, attn_sink, rel_pos_bias, w_branch_a, w_branch_b, w_out,
                  norm2_g, router_w, router_b, w_gate_up, b_gate_up, w_down, b_down, last):
    B, N, _ = x.shape
    M = m.shape[1]
    G = HA_Q // HA_KV
    moe_params = (router_w, router_b, w_gate_up, b_gate_up, w_down, b_down)
    merge_params = (w_branch_a, w_branch_b, w_out)
    hx = rms_norm(x, norm1_g)
    hm = rms_norm(m, norm1_g)
    zx = hx @ w_in
    kA, vA, kB, vB = split_kv(zx[..., G_B_END:], B, N)
    kAm, vAm, kBm, vBm = split_kv(hm @ w_in[:, G_B_END:], B, M)
    oA = window_attention(zx[..., :Q_A_END].reshape(B, N, HA_KV, G, HEAD_DIM),
                          kA, vA, kAm, vAm, attn_sink)
    oB = neighbourhood_attention(zx[..., Q_A_END:Q_B_END].reshape(B, N, HB, HEAD_DIM),
                                 kB, vB, kBm, vBm, rel_pos_bias)
    x = x + merge_branches(zx, oA, oB, *merge_params)
    if last:
        h2 = rms_norm(x, norm2_g).reshape(B * N, D_MODEL)
        x = x + moe_ffn(h2, *moe_params).reshape(B, N, D_MODEL)
    else:
        zm = hm @ w_in[:, :G_B_END]
        oAm = window_attention_meta(zm[..., :Q_A_END].reshape(B, M, HA_KV, G, HEAD_DIM),
                                    kA, vA, kAm, vAm, attn_sink)
        oBm = meta_attention(zm[..., Q_A_END:Q_B_END].reshape(B, M, HB, HEAD_DIM), kBm, vBm)
        m = m + merge_branches(zm, oAm, oBm, *merge_params)
        h2 = rms_norm(jnp.concatenate([m, x], axis=1), norm2_g).reshape(B * (M + N), D_MODEL)
        y2 = moe_ffn(h2, *moe_params).reshape(B, M + N, D_MODEL)
        m = m + y2[:, :M]
        x = x + y2[:, M:]
    return m, x


def setup_inputs(seed: int = 0) -> dict:
    key = jax.random.key(seed)
    ks = jax.random.split(key, 18)

    def nrm(k, shape, scale):
        return jax.random.normal(k, shape, jnp.float32) * scale

    return {
        'x_prompt': nrm(ks[0], (BATCH, SEQ, D_MODEL), 1.0),
        'x_sample': nrm(ks[1], (DEC_BATCH, DEC_SEQ, D_MODEL), 1.0),
        'meta_tokens': nrm(ks[2], (N_META, D_MODEL), 1.0),
        'norm1_g': 1.0 + nrm(ks[3], (DEPTH, D_MODEL), 0.05),
        'w_in': nrm(ks[4], (DEPTH, D_MODEL, D_IN), D_MODEL ** -0.5),
        'attn_sink': nrm(ks[5], (DEPTH, HA_Q), 0.5),
        'rel_pos_bias': nrm(ks[6], (DEPTH, HB, 2 * NB_ROWS_MAX - 1, 2 * NB_COLS - 1), 0.5),
        'w_branch_a': nrm(ks[7], (DEPTH, DA_Q, D_MODEL), DA_Q ** -0.5),
        'w_branch_b': nrm(ks[8], (DEPTH, DB, D_MODEL), DB ** -0.5),
        'w_out': nrm(ks[9], (DEPTH, D_MODEL, D_MODEL), D_MODEL ** -0.5),
        'norm2_g': 1.0 + nrm(ks[10], (DEPTH, D_MODEL), 0.05),
        'router_w': nrm(ks[11], (DEPTH, D_MODEL, N_EXPERTS), D_MODEL ** -0.5),
        'router_b': nrm(ks[12], (DEPTH, N_EXPERTS), 0.01),
        'w_gate_up': nrm(ks[13], (DEPTH, N_EXPERTS, D_MODEL, 2 * D_FF), D_MODEL ** -0.5),
        'b_gate_up': nrm(ks[14], (DEPTH, N_EXPERTS, 2 * D_FF), 0.01),
        'w_down': nrm(ks[15], (DEPTH, N_EXPERTS, D_FF, D_MODEL), D_FF ** -0.5),
        'b_down': nrm(ks[16], (DEPTH, N_EXPERTS, D_MODEL), 0.01),
        'final_norm_g': 1.0 + nrm(ks[17], (D_MODEL,), 0.05),
    }


def reference(x_prompt, x_sample, meta_tokens, norm1_g, w_in, attn_sink, rel_pos_bias,
              w_branch_a, w_branch_b, w_out, norm2_g, router_w, router_b, w_gate_up,
              b_gate_up, w_down, b_down, final_norm_g):
    stacked = (norm1_g, w_in, attn_sink, rel_pos_bias, w_branch_a, w_branch_b, w_out,
               norm2_g, router_w, router_b, w_gate_up, b_gate_up, w_down, b_down)

    def trunk(x):
        B = x.shape[0]
        m = jnp.broadcast_to(meta_tokens.astype(x.dtype)[None], (B, N_META, D_MODEL))
        for layer in range(DEPTH):
            m, x = encoder_layer(m, x, *[p[layer] for p in stacked], last=(layer == DEPTH - 1))
        return rms_norm(x, final_norm_g)

    y_prompt = trunk(x_prompt)
    y_sample = trunk(x_sample)
    return (y_prompt, y_sample)
```

```python
import functools

import numpy as np
import jax
import jax.numpy as jnp
from jax import lax
from jax.experimental import pallas as pl
from jax.experimental.pallas import tpu as pltpu

F32 = jnp.float32
BF16 = jnp.bfloat16

N_META = 16
GRID_W = 64
HA_Q = 8
HA_KV = 2
HEAD_DIM = 64
WINDOW = 128
HB = 8
NB_ROWS = 8
NB_COLS = 16
N_EXPERTS = 32
TOP_K = 4
SWIGLU_ALPHA = 1.702
SWIGLU_LIMIT = 7.0
NORM_EPS = 1e-5
NEG_INF = -1e30
SCALE = HEAD_DIM ** -0.5

LANE = 128
ROW_TILE = 512
WIN_TQ = 256
NBR_ROWS = 4
FFN_BM = 256
CMB_TM = 128
VMEM_LIMIT = 56 * 1024 * 1024


def _rms(x, g):
    return x * lax.rsqrt(jnp.mean(x * x, axis=-1, keepdims=True) + NORM_EPS) * g


def _sigmoid(z):
    return 1.0 / (1.0 + jnp.exp(-z))


def _params(*sem):
    return pltpu.CompilerParams(dimension_semantics=sem, vmem_limit_bytes=VMEM_LIMIT)


def _inproj_kernel(npb, segs, xp_ref, xs_ref, g_ref, w_ref, *o_refs):
    i = pl.program_id(0)
    x = jnp.where(i < npb, xp_ref[...], xs_ref[...])
    h = _rms(x, g_ref[...]).astype(BF16)
    for (c0, width), o_ref in zip(segs, o_refs):
        for off in range(0, width, 512):
            n = min(512, width - off)
            o_ref[:, off:off + n] = jnp.dot(
                h, w_ref[:, c0 + off:c0 + off + n], preferred_element_type=F32).astype(BF16)


def _inproj(xp, xs, g, w, segs):
    npt, d = xp.shape
    t = npt + xs.shape[0]
    tm = ROW_TILE
    npb = npt // tm
    return pl.pallas_call(
        functools.partial(_inproj_kernel, npb, segs),
        grid=(t // tm,),
        in_specs=[
            pl.BlockSpec((tm, d), lambda i: (jnp.minimum(i, npb - 1), 0)),
            pl.BlockSpec((tm, d), lambda i: (jnp.maximum(i - npb, 0), 0)),
            pl.BlockSpec((1, d), lambda i: (0, 0)),
            pl.BlockSpec(w.shape, lambda i: (0, 0)),
        ],
        out_specs=[pl.BlockSpec((tm, width), lambda i: (i, 0)) for _, width in segs],
        out_shape=[jax.ShapeDtypeStruct((t, width), BF16) for _, width in segs],
        compiler_params=_params("arbitrary"),
        name="inproj",
    )(xp, xs, g, w)


def _meta_kernel(m_ref, g_ref, w_ref, o_ref):
    h = _rms(m_ref[...], g_ref[...]).astype(BF16)
    o_ref[...] = jnp.dot(h, w_ref[...], preferred_element_type=F32).astype(BF16)


def _meta_proj(meta, g, w):
    return pl.pallas_call(
        _meta_kernel,
        out_shape=jax.ShapeDtypeStruct((meta.shape[0], w.shape[1]), BF16),
        name="meta_proj",
    )(meta, g, w)


def _window_bias(tq):
    slopes = 2.0 ** (-8.0 * np.arange(1, HA_Q + 1) / HA_Q)
    q = np.arange(tq)[:, None]
    col = np.arange(tq + 2 * WINDOW)[None, :]
    dist = np.abs(col - WINDOW - q)
    band = np.where((dist <= WINDOW)[None], -slopes[:, None, None] * dist[None], NEG_INF)
    full = np.concatenate([band, np.broadcast_to(_meta_cols(), (HA_Q, tq, LANE))], axis=-1)
    return jnp.asarray(full.reshape(HA_Q // 2, 2 * tq, -1), F32)


def _meta_cols():
    return np.where(np.arange(LANE) < N_META, 0.0, NEG_INF)


def _pad_meta(a):
    return jnp.pad(a, ((0, LANE - a.shape[0]), (0, 0)))


def _seq_pos(start, npt, lp, ls):
    is_p = start < npt
    pos = jnp.where(is_p, lax.rem(start, lp), lax.rem(jnp.maximum(start - npt, 0), ls))
    return pos, jnp.where(is_p, lp, ls)


def _window_kernel(npt, lp, ls, sink_ref, qa_ref, kp_ref, kc_ref, kn_ref, vp_ref, vc_ref, vn_ref,
                   km_ref, vm_ref, bias_ref, o_ref):
    tq = qa_ref.shape[0]
    nk = tq + 2 * WINDOW
    pos, seq_len = _seq_pos(pl.program_id(0) * tq, npt, lp, ls)
    pen_prev = jnp.where(pos == 0, NEG_INF, 0.0).astype(F32)
    pen_next = jnp.where(pos + tq == seq_len, NEG_INF, 0.0).astype(F32)
    col = lax.broadcasted_iota(jnp.int32, (1, nk + LANE), 1)
    pen = (jnp.where(col < WINDOW, pen_prev, 0.0)
           + jnp.where((col >= WINDOW + tq) & (col < nk), pen_next, 0.0))
    lo = lax.broadcasted_iota(jnp.int32, (1, LANE), 1) < HEAD_DIM
    top = lax.broadcasted_iota(jnp.int32, (2 * tq, 1), 0) < tq
    for i in range(HA_Q // 2):
        hk = (2 * i) // (HA_Q // HA_KV)
        sl = slice(i * LANE, (i + 1) * LANE)
        ksl = slice(hk * LANE, (hk + 1) * LANE)
        q = qa_ref[:, sl] * jnp.asarray(SCALE, BF16)
        zero = jnp.zeros_like(q)
        qq = jnp.concatenate([jnp.where(lo, q, zero), jnp.where(lo, zero, q)], axis=0)
        k_all = jnp.concatenate([kp_ref[:, ksl], kc_ref[:, ksl], kn_ref[:, ksl], km_ref[:, ksl]], axis=0)
        v_all = jnp.concatenate([vp_ref[:, ksl], vc_ref[:, ksl], vn_ref[:, ksl], vm_ref[:, ksl]], axis=0)
        s = lax.dot_general(qq, k_all, (((1,), (1,)), ((), ())), preferred_element_type=F32)
        s = s + bias_ref[i] + pen
        sink = jnp.where(top, sink_ref[2 * i], sink_ref[2 * i + 1])
        m = jnp.maximum(jnp.max(s, axis=-1, keepdims=True), sink)
        p = jnp.exp(s - m)
        denom = jnp.sum(p, axis=-1, keepdims=True) + jnp.exp(sink - m)
        o = jnp.dot(p.astype(BF16), v_all, preferred_element_type=F32) / denom
        o_ref[:, sl] = jnp.where(lo, o[:tq], o[tq:]).astype(BF16)


def _window_attention(qa, ka2, va2, km2, vm2, sink, npt, lp, ls):
    t = qa.shape[0]
    tq = WIN_TQ
    assert lp % tq == 0 and ls % tq == 0 and tq == 2 * WINDOW
    nhalf = t // WINDOW
    bias = _window_bias(tq)
    kw = ka2.shape[1]
    prev = pl.BlockSpec((WINDOW, kw), lambda j: (jnp.maximum(2 * j - 1, 0), 0))
    cur = pl.BlockSpec((tq, kw), lambda j: (j, 0))
    nxt = pl.BlockSpec((WINDOW, kw), lambda j: (jnp.minimum(2 * j + 2, nhalf - 1), 0))
    whole = lambda a: pl.BlockSpec(a.shape, lambda j: (0,) * a.ndim)
    return pl.pallas_call(
        functools.partial(_window_kernel, npt, lp, ls),
        grid=(t // tq,),
        in_specs=[
            pl.BlockSpec(memory_space=pltpu.SMEM),
            pl.BlockSpec((tq, qa.shape[1]), lambda j: (j, 0)),
            prev, cur, nxt, prev, cur, nxt,
            whole(km2), whole(vm2), whole(bias),
        ],
        out_specs=pl.BlockSpec((tq, qa.shape[1]), lambda j: (j, 0)),
        out_shape=jax.ShapeDtypeStruct(qa.shape, BF16),
        compiler_params=_params("arbitrary"),
        name="window_attn",
    )(sink, qa, ka2, ka2, ka2, va2, va2, va2, km2, vm2, bias)


def _nbr_tables(rpb):
    r, w = NBR_ROWS, GRID_W
    a = np.repeat(np.arange(r), w)[:, None]
    c = np.tile(np.arange(w), r)[:, None]
    b = np.repeat(np.arange(3 * r), w)[None, :]
    kc = np.tile(np.arange(w), 3 * r)[None, :]
    row_rel = b - a + (NB_ROWS - 1 - NBR_ROWS)
    col_start = np.clip(c - NB_COLS // 2, 0, w - NB_COLS)
    col_ok = (kc >= col_start) & (kc < col_start + NB_COLS)
    col_rel = np.clip(kc - c + NB_COLS - 1, 0, 2 * NB_COLS - 2)
    row_idx = np.clip(row_rel, 0, 2 * NB_ROWS - 2)
    vals = rpb[:, row_idx, col_rel].astype(F32)
    vals = jnp.where(jnp.asarray(col_ok)[None], vals, NEG_INF)
    meta = jnp.broadcast_to(jnp.asarray(_meta_cols(), F32), (HB, r * w, LANE))
    bias = jnp.concatenate([vals, meta], axis=-1)
    first = NBR_ROWS
    ok = np.stack([
        (b >= first) & (b < first + NB_ROWS) & (a >= 0),
        (b >= a) & (b < a + NB_ROWS),
        (b >= 0) & (b < NB_ROWS) & (a >= 0),
    ])
    ok = np.concatenate([ok, np.ones((3, r * w, LANE), bool)], axis=-1)
    return bias, jnp.asarray(np.where(ok, 0.0, NEG_INF), F32)


def _nbr_kernel(npg, gp, gs, qb_ref, kp_ref, kc_ref, kn_ref, vp_ref, vc_ref, vn_ref, km_ref, vm_ref,
                bias_ref, mask_ref, o_ref):
    tq = qb_ref.shape[0]
    g = pl.program_id(0)
    is_p = g < npg
    per_seq = jnp.where(is_p, gp, gs)
    r = jnp.where(is_p, lax.rem(g, gp), lax.rem(jnp.maximum(g - npg, 0), gs))
    pat = jnp.where(r == 0, 0, jnp.where(r == per_seq - 1, 2, 1))
    row_mask = mask_ref[pat]
    lo = lax.broadcasted_iota(jnp.int32, (1, LANE), 1) < HEAD_DIM
    for i in range(HB // 2):
        sl = slice(i * LANE, (i + 1) * LANE)
        q = qb_ref[:, sl] * jnp.asarray(SCALE, BF16)
        zero = jnp.zeros_like(q)
        qq = jnp.concatenate([jnp.where(lo, q, zero), jnp.where(lo, zero, q)], axis=0)
        k_all = jnp.concatenate([kp_ref[:, sl], kc_ref[:, sl], kn_ref[:, sl], km_ref[:, sl]], axis=0)
        v_all = jnp.concatenate([vp_ref[:, sl], vc_ref[:, sl], vn_ref[:, sl], vm_ref[:, sl]], axis=0)
        s = lax.dot_general(qq, k_all, (((1,), (1,)), ((), ())), preferred_element_type=F32)
        halves = []
        for hf in range(2):
            sh = s[hf * tq:(hf + 1) * tq] + bias_ref[2 * i + hf] + row_mask
            m = jnp.max(sh, axis=-1, keepdims=True)
            p = jnp.exp(sh - m)
            denom = jnp.sum(p, axis=-1, keepdims=True)
            halves.append(jnp.dot(p.astype(BF16), v_all, preferred_element_type=F32) / denom)
        o_ref[:, sl] = jnp.where(lo, halves[0], halves[1]).astype(BF16)


def _nbr_attention(qb, kb, vb, kbm, vbm, rpb, npt, lp, ls):
    t, width = qb.shape
    tq = NBR_ROWS * GRID_W
    rows_p, rows_s = lp // GRID_W, ls // GRID_W
    assert rows_p % NBR_ROWS == 0 and rows_s % NBR_ROWS == 0 and min(rows_p, rows_s) >= NB_ROWS
    assert NB_ROWS == 2 * NBR_ROWS
    ng = t // tq
    bias, mask = _nbr_tables(rpb)
    prev = pl.BlockSpec((tq, width), lambda g: (jnp.maximum(g - 1, 0), 0))
    cur = pl.BlockSpec((tq, width), lambda g: (g, 0))
    nxt = pl.BlockSpec((tq, width), lambda g: (jnp.minimum(g + 1, ng - 1), 0))
    whole = lambda a: pl.BlockSpec(a.shape, lambda g: (0,) * a.ndim)
    return pl.pallas_call(
        functools.partial(_nbr_kernel, npt // tq, rows_p // NBR_ROWS, rows_s // NBR_ROWS),
        grid=(ng,),
        in_specs=[cur, prev, cur, nxt, prev, cur, nxt, whole(kbm), whole(vbm), whole(bias), whole(mask)],
        out_specs=cur,
        out_shape=jax.ShapeDtypeStruct(qb.shape, BF16),
        compiler_params=_params("arbitrary"),
        name="nbr_attn",
    )(qb, kb, kb, kb, vb, vb, vb, kbm, vbm, bias, mask)


def _merge_kernel(npb, oa_ref, ob_ref, g_ref, xp_ref, xs_ref, wa_ref, wb_ref, wo_ref, g2_ref,
                  rw_ref, rb_ref, x1_ref, h2_ref, lg_ref):
    i = pl.program_id(0)
    d = x1_ref.shape[1]
    x = jnp.where(i < npb, xp_ref[...], xs_ref[...])
    a = jnp.dot(oa_ref[...], wa_ref[...], preferred_element_type=F32)
    b = jnp.dot(ob_ref[...], wb_ref[...], preferred_element_type=F32)
    mix = _sigmoid(g_ref[:, :d].astype(F32)) * a + _sigmoid(g_ref[:, d:].astype(F32)) * b
    x1 = x + jnp.dot(mix.astype(BF16), wo_ref[...], preferred_element_type=F32)
    x1_ref[...] = x1
    h2 = _rms(x1, g2_ref[...])
    h2_ref[...] = h2
    lg_ref[...] = jnp.dot(h2, rw_ref[...], preferred_element_type=F32,
                          precision=lax.Precision.HIGHEST) + rb_ref[...]


def _merge(oa, ob, gates, xp, xs, wa, wb, wo, g2, rw, rb):
    npt, d = xp.shape
    t = npt + xs.shape[0]
    tm = ROW_TILE
    npb = npt // tm
    row = lambda width: pl.BlockSpec((tm, width), lambda i: (i, 0))
    whole = lambda a: pl.BlockSpec(a.shape, lambda i: (0,) * a.ndim)
    return pl.pallas_call(
        functools.partial(_merge_kernel, npb),
        grid=(t // tm,),
        in_specs=[
            row(oa.shape[1]), row(ob.shape[1]), row(gates.shape[1]),
            pl.BlockSpec((tm, d), lambda i: (jnp.minimum(i, npb - 1), 0)),
            pl.BlockSpec((tm, d), lambda i: (jnp.maximum(i - npb, 0), 0)),
            whole(wa), whole(wb), whole(wo), whole(g2), whole(rw), whole(rb),
        ],
        out_specs=[row(d), row(d), row(N_EXPERTS)],
        out_shape=[jax.ShapeDtypeStruct((t, d), F32), jax.ShapeDtypeStruct((t, d), F32),
                   jax.ShapeDtypeStruct((t, N_EXPERTS), F32)],
        compiler_params=_params("arbitrary"),
        name="merge_router",
    )(oa, ob, gates, xp, xs, wa, wb, wo, g2, rw, rb)


def _route(logits, bm):
    t = logits.shape[0]
    a = t * TOP_K
    top_val, top_idx = lax.top_k(logits, TOP_K)
    gate = jax.nn.softmax(top_val, axis=-1)
    flat_e = top_idx.reshape(-1).astype(jnp.int32)
    order = jnp.argsort(flat_e, stable=True).astype(jnp.int32)
    sorted_e = flat_e[order]
    counts = jnp.bincount(flat_e, length=N_EXPERTS).astype(jnp.int32)
    padded = (counts + bm - 1) // bm * bm
    pad_end = jnp.cumsum(padded)
    pad_start = pad_end - padded
    grp_start = jnp.cumsum(counts) - counts
    slot_sorted = pad_start[sorted_e] + jnp.arange(a, dtype=jnp.int32) - grp_start[sorted_e]
    n_blocks = -(-a // bm) + N_EXPERTS
    slot = jnp.zeros((a,), jnp.int32).at[order].set(slot_sorted)
    slot_tok = jnp.zeros((n_blocks * bm,), jnp.int32).at[slot_sorted].set(order // TOP_K)
    block_e = jnp.minimum(
        jnp.searchsorted(pad_end, jnp.arange(n_blocks, dtype=jnp.int32) * bm, side="right"),
        N_EXPERTS - 1).astype(jnp.int32)
    n_used = (pad_end[-1] // bm).astype(jnp.int32).reshape(1)
    return gate, slot, slot_tok, block_e, n_used


def _ffn_kernel(be_ref, nu_ref, tok_ref, tokn_ref, h2_hbm, wg_ref, wu_ref, wd_ref, bg_ref, bu_ref,
                bd_ref, y_ref, xbuf, sem):
    bm = y_ref.shape[0]
    b = pl.program_id(0)
    n_used = nu_ref[0]
    slot = lax.rem(b, 2)

    def row_copy(tok, j, s):
        return pltpu.make_async_copy(h2_hbm.at[pl.ds(tok, 1)], xbuf.at[s, pl.ds(j, 1)], sem.at[s])

    def issue(t_ref, s):
        def body(j, carry):
            row_copy(t_ref[0, 0, j], j, s).start()
            return carry
        lax.fori_loop(0, bm, body, 0, unroll=8)

    @pl.when(b == 0)
    def _():
        issue(tok_ref, 0)

    @pl.when(b + 1 < n_used)
    def _():
        issue(tokn_ref, 1 - slot)

    @pl.when(b < n_used)
    def _():
        pltpu.make_async_copy(h2_hbm.at[pl.ds(0, bm)], xbuf.at[slot], sem.at[slot]).wait()
        x = xbuf[slot].astype(BF16)
        hg = jnp.dot(x, wg_ref[0], preferred_element_type=F32) + bg_ref[0]
        hu = jnp.dot(x, wu_ref[0], preferred_element_type=F32) + bu_ref[0]
        glu = jnp.minimum(hg, SWIGLU_LIMIT)
        lin = jnp.clip(hu, -SWIGLU_LIMIT, SWIGLU_LIMIT)
        act = glu * _sigmoid(SWIGLU_ALPHA * glu) * (lin + 1.0)
        y_ref[...] = jnp.dot(act.astype(BF16), wd_ref[0], preferred_element_type=F32) + bd_ref[0]

    @pl.when(b >= n_used)
    def _():
        y_ref[...] = jnp.zeros_like(y_ref)


def _expert_ffn(h2, slot_tok, block_e, n_used, wg, wu, wd, bg, bu, bd):
    bm = FFN_BM
    d = h2.shape[1]
    nb = block_e.shape[0]
    tok3 = slot_tok.reshape(nb, 1, bm)
    wspec = lambda a: pl.BlockSpec((1,) + a.shape[1:], lambda b, be, nu: (be[b], 0, 0))
    smem_blk = lambda f: pl.BlockSpec((1, 1, bm), f, memory_space=pltpu.SMEM)
    return pl.pallas_call(
        _ffn_kernel,
        grid_spec=pltpu.PrefetchScalarGridSpec(
            num_scalar_prefetch=2,
            grid=(nb,),
            in_specs=[
                smem_blk(lambda b, be, nu: (b, 0, 0)),
                smem_blk(lambda b, be, nu: (jnp.minimum(b + 1, nb - 1), 0, 0)),
                pl.BlockSpec(memory_space=pl.ANY),
                wspec(wg), wspec(wu), wspec(wd), wspec(bg), wspec(bu), wspec(bd),
            ],
            out_specs=pl.BlockSpec((bm, d), lambda b, be, nu: (b, 0)),
            scratch_shapes=[pltpu.VMEM((2, bm, d), F32), pltpu.SemaphoreType.DMA((2,))],
        ),
        out_shape=jax.ShapeDtypeStruct((nb * bm, d), F32),
        compiler_params=_params("arbitrary"),
        name="expert_ffn",
    )(block_e, n_used, tok3, tok3, h2, wg, wu, wd, bg, bu, bd)


def _combine_kernel(npb, sl_ref, sln_ref, gate_ref, x1_ref, gf_ref, y_hbm, yp_ref, ys_ref, ybuf, sem):
    tm = x1_ref.shape[0]
    i = pl.program_id(0)
    n = pl.num_programs(0)
    slot = lax.rem(i, 2)

    def issue(s_ref, s):
        def body(j, carry):
            src = y_hbm.at[pl.ds(s_ref[0, 0, j], 1)]
            dst = ybuf.at[s, lax.rem(j, TOP_K), pl.ds(lax.div(j, TOP_K), 1)]
            pltpu.make_async_copy(src, dst, sem.at[s]).start()
            return carry
        lax.fori_loop(0, tm * TOP_K, body, 0, unroll=8)

    @pl.when(i == 0)
    def _():
        issue(sl_ref, 0)

    @pl.when(i + 1 < n)
    def _():
        issue(sln_ref, 1 - slot)

    for k in range(TOP_K):
        pltpu.make_async_copy(y_hbm.at[pl.ds(0, tm)], ybuf.at[slot, k], sem.at[slot]).wait()
    y = jnp.zeros(x1_ref.shape, F32)
    gate = gate_ref[...]
    for k in range(TOP_K):
        y = y + gate[:, k:k + 1] * ybuf[slot, k]
    out = _rms(x1_ref[...] + y, gf_ref[...])

    @pl.when(i < npb)
    def _():
        yp_ref[...] = out

    @pl.when(i >= npb)
    def _():
        ys_ref[...] = out


def _combine(y_slots, slot, gate, x1, gf, npt):
    t, d = x1.shape
    tm = CMB_TM
    nt = t // tm
    npb = npt // tm
    slot3 = slot.reshape(nt, 1, tm * TOP_K)
    smem_blk = lambda f: pl.BlockSpec((1, 1, tm * TOP_K), f, memory_space=pltpu.SMEM)
    return pl.pallas_call(
        functools.partial(_combine_kernel, npb),
        grid=(nt,),
        in_specs=[
            smem_blk(lambda i: (i, 0, 0)),
            smem_blk(lambda i: (jnp.minimum(i + 1, nt - 1), 0, 0)),
            pl.BlockSpec((tm, TOP_K), lambda i: (i, 0)),
            pl.BlockSpec((tm, d), lambda i: (i, 0)),
            pl.BlockSpec((1, d), lambda i: (0, 0)),
            pl.BlockSpec(memory_space=pl.ANY),
        ],
        out_specs=[
            pl.BlockSpec((tm, d), lambda i: (jnp.minimum(i, npb - 1), 0)),
            pl.BlockSpec((tm, d), lambda i: (jnp.maximum(i - npb, 0), 0)),
        ],
        out_shape=[jax.ShapeDtypeStruct((npt, d), F32), jax.ShapeDtypeStruct((t - npt, d), F32)],
        scratch_shapes=[pltpu.VMEM((2, TOP_K, tm, d), F32), pltpu.SemaphoreType.DMA((2,))],
        compiler_params=_params("arbitrary"),
        name="combine_norm",
    )(slot3, slot3, gate, x1, gf, y_slots)


def _inproj_columns(d):
    da_q, db = HA_Q * HEAD_DIM, HB * HEAD_DIM
    da_kv = HA_KV * HEAD_DIM
    q_b_end = da_q + db
    g_b_end = q_b_end + 2 * d
    ka0 = g_b_end
    va0 = ka0 + da_kv
    kb0 = va0 + da_kv
    vb0 = kb0 + db
    dup = lambda c0: np.concatenate(
        [np.tile(np.arange(c0 + h * HEAD_DIM, c0 + (h + 1) * HEAD_DIM), 2) for h in range(HA_KV)])
    cols = np.concatenate([np.arange(0, g_b_end), dup(ka0), dup(va0), np.arange(kb0, vb0 + db)])
    segs = ((0, da_q), (da_q, db), (q_b_end, 2 * d), (g_b_end, 2 * da_kv),
            (g_b_end + 2 * da_kv, 2 * da_kv), (g_b_end + 4 * da_kv, db), (g_b_end + 4 * da_kv + db, db))
    return cols, segs


def kernel(x_prompt, x_sample, meta_tokens, norm1_g, w_in, attn_sink, rel_pos_bias, w_branch_a,
           w_branch_b, w_out, norm2_g, router_w, router_b, w_gate_up, b_gate_up, w_down, b_down,
           final_norm_g):
    assert norm1_g.shape[0] == 1, "single-layer trunk"
    bp, lp, d = x_prompt.shape
    bs, ls, _ = x_sample.shape
    npt, nst = bp * lp, bs * ls
    assert npt % ROW_TILE == 0 and nst % ROW_TILE == 0
    xp = x_prompt.reshape(npt, d)
    xs = x_sample.reshape(nst, d)

    cols, segs = _inproj_columns(d)
    w_p = w_in[0][:, cols].astype(BF16)
    g1 = norm1_g[0].reshape(1, d)
    qa, qb, gates, ka2, va2, kb, vb = _inproj(xp, xs, g1, w_p, segs)
    kv0 = segs[3][0]
    meta_kv = _meta_proj(meta_tokens, g1, w_p[:, kv0:])
    w_kv = 2 * HA_KV * HEAD_DIM
    meta_kv = _pad_meta(meta_kv)
    km2, vm2 = meta_kv[:, :w_kv], meta_kv[:, w_kv:2 * w_kv]
    kbm, vbm = meta_kv[:, 2 * w_kv:2 * w_kv + HB * HEAD_DIM], meta_kv[:, 2 * w_kv + HB * HEAD_DIM:]

    oa = _window_attention(qa, ka2, va2, km2, vm2, attn_sink[0].astype(F32), npt, lp, ls)
    ob = _nbr_attention(qb, kb, vb, kbm, vbm, rel_pos_bias[0], npt, lp, ls)

    x1, h2, logits = _merge(
        oa, ob, gates, xp, xs, w_branch_a[0].astype(BF16), w_branch_b[0].astype(BF16),
        w_out[0].astype(BF16), norm2_g[0].reshape(1, d), router_w[0], router_b[0].reshape(1, -1))

    gate, slot, slot_tok, block_e, n_used = _route(logits, FFN_BM)
    wgu = w_gate_up[0]
    bgu = b_gate_up[0]
    y_slots = _expert_ffn(
        h2, slot_tok, block_e, n_used,
        wgu[:, :, 0::2].astype(BF16), wgu[:, :, 1::2].astype(BF16), w_down[0].astype(BF16),
        bgu[:, None, 0::2], bgu[:, None, 1::2], b_down[0][:, None, :])

    yp, ys = _combine(y_slots, slot, gate, x1, final_norm_g.reshape(1, d), npt)
    return yp.reshape(bp, lp, d), ys.reshape(bs, ls, d)
```

```python
import functools

import numpy as np
import jax
import jax.numpy as jnp
from jax import lax
from jax.experimental import pallas as pl
from jax.experimental.pallas import tpu as pltpu

F32 = jnp.float32
BF16 = jnp.bfloat16

N_META = 16
GRID_W = 64
HA_Q = 8
HA_KV = 2
HEAD_DIM = 64
WINDOW = 128
HB = 8
NB_ROWS = 8
NB_COLS = 16
N_EXPERTS = 32
TOP_K = 4
SWIGLU_ALPHA = 1.702
SWIGLU_LIMIT = 7.0
NORM_EPS = 1e-5
NEG_INF = -1e30
SCALE = HEAD_DIM ** -0.5

LANE = 128
SUB = 8
ROW_TILE = 512
WIN_TQ = 256
NBR_ROWS = 4
FFN_BM = 256
CMB_TM = 128
VMEM_LIMIT = 56 * 1024 * 1024


def _rms(x, g):
    return x * lax.rsqrt(jnp.mean(x * x, axis=-1, keepdims=True) + NORM_EPS) * g


def _sigmoid(z):
    return 1.0 / (1.0 + jnp.exp(-z))


def _params(*sem):
    return pltpu.CompilerParams(dimension_semantics=sem, vmem_limit_bytes=VMEM_LIMIT)


def _inproj_kernel(npb, segs, xp_ref, xs_ref, g_ref, w_ref, *o_refs):
    i = pl.program_id(0)
    x = jnp.where(i < npb, xp_ref[...], xs_ref[...])
    h = _rms(x, g_ref[...]).astype(BF16)
    for (c0, width), o_ref in zip(segs, o_refs):
        for off in range(0, width, 512):
            n = min(512, width - off)
            o_ref[:, off:off + n] = jnp.dot(
                h, w_ref[:, c0 + off:c0 + off + n], preferred_element_type=F32).astype(BF16)


def _inproj(xp, xs, g, w, segs):
    npt, d = xp.shape
    t = npt + xs.shape[0]
    tm = ROW_TILE
    npb = npt // tm
    return pl.pallas_call(
        functools.partial(_inproj_kernel, npb, segs),
        grid=(t // tm,),
        in_specs=[
            pl.BlockSpec((tm, d), lambda i: (jnp.minimum(i, npb - 1), 0)),
            pl.BlockSpec((tm, d), lambda i: (jnp.maximum(i - npb, 0), 0)),
            pl.BlockSpec((1, d), lambda i: (0, 0)),
            pl.BlockSpec(w.shape, lambda i: (0, 0)),
        ],
        out_specs=[pl.BlockSpec((tm, width), lambda i: (i, 0)) for _, width in segs],
        out_shape=[jax.ShapeDtypeStruct((t, width), BF16) for _, width in segs],
        compiler_params=_params("arbitrary"),
        name="inproj",
    )(xp, xs, g, w)


def _meta_kernel(m_ref, g_ref, w_ref, o_ref):
    h = _rms(m_ref[...], g_ref[...]).astype(BF16)
    o_ref[...] = jnp.dot(h, w_ref[...], preferred_element_type=F32).astype(BF16)


def _meta_proj(meta, g, w):
    return pl.pallas_call(
        _meta_kernel,
        out_shape=jax.ShapeDtypeStruct((meta.shape[0], w.shape[1]), BF16),
        name="meta_proj",
    )(meta, g, w)


def _window_bias(tq):
    slopes = 2.0 ** (-8.0 * np.arange(1, HA_Q + 1) / HA_Q)
    q = np.arange(tq)[:, None]
    col = np.arange(tq + 2 * WINDOW)[None, :]
    dist = np.abs(col - WINDOW - q)
    band = np.where((dist <= WINDOW)[None], -slopes[:, None, None] * dist[None], NEG_INF)
    full = np.concatenate([band, np.broadcast_to(_meta_cols(), (HA_Q, tq, LANE))], axis=-1)
    return jnp.asarray(full.reshape(HA_Q // 2, 2 * tq, -1), F32)


def _meta_cols():
    return np.where(np.arange(LANE) < N_META, 0.0, NEG_INF)


def _pad_meta(a):
    return jnp.pad(a, ((0, LANE - a.shape[0]), (0, 0)))


def _seq_pos(start, npt, lp, ls):
    is_p = start < npt
    pos = jnp.where(is_p, lax.rem(start, lp), lax.rem(jnp.maximum(start - npt, 0), ls))
    return pos, jnp.where(is_p, lp, ls)


def _window_kernel(npt, lp, ls, sink_ref, qa_ref, kp_ref, kc_ref, kn_ref, vp_ref, vc_ref, vn_ref,
                   km_ref, vm_ref, bias_ref, o_ref):
    tq = qa_ref.shape[0]
    nk = tq + 2 * WINDOW
    pos, seq_len = _seq_pos(pl.program_id(0) * tq, npt, lp, ls)
    pen_prev = jnp.where(pos == 0, NEG_INF, 0.0).astype(F32)
    pen_next = jnp.where(pos + tq == seq_len, NEG_INF, 0.0).astype(F32)
    col = lax.broadcasted_iota(jnp.int32, (1, nk + LANE), 1)
    pen = (jnp.where(col < WINDOW, pen_prev, 0.0)
           + jnp.where((col >= WINDOW + tq) & (col < nk), pen_next, 0.0))
    lo = lax.broadcasted_iota(jnp.int32, (1, LANE), 1) < HEAD_DIM
    top = lax.broadcasted_iota(jnp.int32, (2 * tq, 1), 0) < tq
    for i in range(HA_Q // 2):
        hk = (2 * i) // (HA_Q // HA_KV)
        sl = slice(i * LANE, (i + 1) * LANE)
        ksl = slice(hk * LANE, (hk + 1) * LANE)
        q = qa_ref[:, sl] * jnp.asarray(SCALE, BF16)
        zero = jnp.zeros_like(q)
        qq = jnp.concatenate([jnp.where(lo, q, zero), jnp.where(lo, zero, q)], axis=0)
        k_all = jnp.concatenate([kp_ref[:, ksl], kc_ref[:, ksl], kn_ref[:, ksl], km_ref[:, ksl]], axis=0)
        v_all = jnp.concatenate([vp_ref[:, ksl], vc_ref[:, ksl], vn_ref[:, ksl], vm_ref[:, ksl]], axis=0)
        s = lax.dot_general(qq, k_all, (((1,), (1,)), ((), ())), preferred_element_type=F32)
        s = s + bias_ref[i] + pen
        sink = jnp.where(top, sink_ref[2 * i], sink_ref[2 * i + 1])
        m = jnp.maximum(jnp.max(s, axis=-1, keepdims=True), sink)
        p = jnp.exp(s - m)
        denom = jnp.sum(p, axis=-1, keepdims=True) + jnp.exp(sink - m)
        o = jnp.dot(p.astype(BF16), v_all, preferred_element_type=F32) / denom
        o_ref[:, sl] = jnp.where(lo, o[:tq], o[tq:]).astype(BF16)


def _window_attention(qa, ka2, va2, km2, vm2, sink, npt, lp, ls):
    t = qa.shape[0]
    tq = WIN_TQ
    assert lp % tq == 0 and ls % tq == 0 and tq == 2 * WINDOW
    nhalf = t // WINDOW
    bias = _window_bias(tq)
    kw = ka2.shape[1]
    prev = pl.BlockSpec((WINDOW, kw), lambda j: (jnp.maximum(2 * j - 1, 0), 0))
    cur = pl.BlockSpec((tq, kw), lambda j: (j, 0))
    nxt = pl.BlockSpec((WINDOW, kw), lambda j: (jnp.minimum(2 * j + 2, nhalf - 1), 0))
    whole = lambda a: pl.BlockSpec(a.shape, lambda j: (0,) * a.ndim)
    return pl.pallas_call(
        functools.partial(_window_kernel, npt, lp, ls),
        grid=(t // tq,),
        in_specs=[
            pl.BlockSpec(memory_space=pltpu.SMEM),
            pl.BlockSpec((tq, qa.shape[1]), lambda j: (j, 0)),
            prev, cur, nxt, prev, cur, nxt,
            whole(km2), whole(vm2), whole(bias),
        ],
        out_specs=pl.BlockSpec((tq, qa.shape[1]), lambda j: (j, 0)),
        out_shape=jax.ShapeDtypeStruct(qa.shape, BF16),
        compiler_params=_params("arbitrary"),
        name="window_attn",
    )(sink, qa, ka2, ka2, ka2, va2, va2, va2, km2, vm2, bias)


def _nbr_tables(rpb):
    r, w = NBR_ROWS, GRID_W
    a = np.repeat(np.arange(r), w)[:, None]
    c = np.tile(np.arange(w), r)[:, None]
    b = np.repeat(np.arange(3 * r), w)[None, :]
    kc = np.tile(np.arange(w), 3 * r)[None, :]
    row_rel = b - a + (NB_ROWS - 1 - NBR_ROWS)
    col_start = np.clip(c - NB_COLS // 2, 0, w - NB_COLS)
    col_ok = (kc >= col_start) & (kc < col_start + NB_COLS)
    col_rel = np.clip(kc - c + NB_COLS - 1, 0, 2 * NB_COLS - 2)
    row_idx = np.clip(row_rel, 0, 2 * NB_ROWS - 2)
    vals = rpb[:, row_idx, col_rel].astype(F32)
    vals = jnp.where(jnp.asarray(col_ok)[None], vals, NEG_INF)
    meta = jnp.broadcast_to(jnp.asarray(_meta_cols(), F32), (HB, r * w, LANE))
    bias = jnp.concatenate([vals, meta], axis=-1)
    first = NBR_ROWS
    ok = np.stack([
        (b >= first) & (b < first + NB_ROWS) & (a >= 0),
        (b >= a) & (b < a + NB_ROWS),
        (b >= 0) & (b < NB_ROWS) & (a >= 0),
    ])
    ok = np.concatenate([ok, np.ones((3, r * w, LANE), bool)], axis=-1)
    return bias, jnp.asarray(np.where(ok, 0.0, NEG_INF), F32)


def _nbr_kernel(npg, gp, gs, qb_ref, kp_ref, kc_ref, kn_ref, vp_ref, vc_ref, vn_ref, km_ref, vm_ref,
                bias_ref, mask_ref, o_ref):
    tq = qb_ref.shape[0]
    g = pl.program_id(0)
    is_p = g < npg
    per_seq = jnp.where(is_p, gp, gs)
    r = jnp.where(is_p, lax.rem(g, gp), lax.rem(jnp.maximum(g - npg, 0), gs))
    pat = jnp.where(r == 0, 0, jnp.where(r == per_seq - 1, 2, 1))
    row_mask = mask_ref[pat]
    lo = lax.broadcasted_iota(jnp.int32, (1, LANE), 1) < HEAD_DIM
    for i in range(HB // 2):
        sl = slice(i * LANE, (i + 1) * LANE)
        q = qb_ref[:, sl] * jnp.asarray(SCALE, BF16)
        zero = jnp.zeros_like(q)
        qq = jnp.concatenate([jnp.where(lo, q, zero), jnp.where(lo, zero, q)], axis=0)
        k_all = jnp.concatenate([kp_ref[:, sl], kc_ref[:, sl], kn_ref[:, sl], km_ref[:, sl]], axis=0)
        v_all = jnp.concatenate([vp_ref[:, sl], vc_ref[:, sl], vn_ref[:, sl], vm_ref[:, sl]], axis=0)
        s = lax.dot_general(qq, k_all, (((1,), (1,)), ((), ())), preferred_element_type=F32)
        halves = []
        for hf in range(2):
            sh = s[hf * tq:(hf + 1) * tq] + bias_ref[2 * i + hf] + row_mask
            m = jnp.max(sh, axis=-1, keepdims=True)
            p = jnp.exp(sh - m)
            denom = jnp.sum(p, axis=-1, keepdims=True)
            halves.append(jnp.dot(p.astype(BF16), v_all, preferred_element_type=F32) / denom)
        o_ref[:, sl] = jnp.where(lo, halves[0], halves[1]).astype(BF16)


def _nbr_attention(qb, kb, vb, kbm, vbm, rpb, npt, lp, ls):
    t, width = qb.shape
    tq = NBR_ROWS * GRID_W
    rows_p, rows_s = lp // GRID_W, ls // GRID_W
    assert rows_p % NBR_ROWS == 0 and rows_s % NBR_ROWS == 0 and min(rows_p, rows_s) >= NB_ROWS
    assert NB_ROWS == 2 * NBR_ROWS
    ng = t // tq
    bias, mask = _nbr_tables(rpb)
    prev = pl.BlockSpec((tq, width), lambda g: (jnp.maximum(g - 1, 0), 0))
    cur = pl.BlockSpec((tq, width), lambda g: (g, 0))
    nxt = pl.BlockSpec((tq, width), lambda g: (jnp.minimum(g + 1, ng - 1), 0))
    whole = lambda a: pl.BlockSpec(a.shape, lambda g: (0,) * a.ndim)
    return pl.pallas_call(
        functools.partial(_nbr_kernel, npt // tq, rows_p // NBR_ROWS, rows_s // NBR_ROWS),
        grid=(ng,),
        in_specs=[cur, prev, cur, nxt, prev, cur, nxt, whole(kbm), whole(vbm), whole(bias), whole(mask)],
        out_specs=cur,
        out_shape=jax.ShapeDtypeStruct(qb.shape, BF16),
        compiler_params=_params("arbitrary"),
        name="nbr_attn",
    )(qb, kb, kb, kb, vb, vb, vb, kbm, vbm, bias, mask)


def _merge_kernel(npb, oa_ref, ob_ref, g_ref, xp_ref, xs_ref, wa_ref, wb_ref, wo_ref, g2_ref,
                  rw_ref, rb_ref, tri_ref, x1_ref, h2_ref, idx_ref, gate_ref, rank_ref, cnt_ref, carry):
    i = pl.program_id(0)
    d = x1_ref.shape[1]
    x = jnp.where(i < npb, xp_ref[...], xs_ref[...])
    a = jnp.dot(oa_ref[...], wa_ref[...], preferred_element_type=F32)
    b = jnp.dot(ob_ref[...], wb_ref[...], preferred_element_type=F32)
    mix = _sigmoid(g_ref[:, :d].astype(F32)) * a + _sigmoid(g_ref[:, d:].astype(F32)) * b
    x1 = x + jnp.dot(mix.astype(BF16), wo_ref[...], preferred_element_type=F32)
    x1_ref[...] = x1
    h2 = _rms(x1, g2_ref[...])
    _store_token_tiled(h2_ref, h2)
    logits = jnp.dot(h2, rw_ref[...], preferred_element_type=F32,
                     precision=lax.Precision.HIGHEST) + rb_ref[...]

    lane = lax.broadcasted_iota(jnp.int32, logits.shape, 1).astype(F32)
    work = logits
    vals, idxs = [], []
    for _ in range(TOP_K):
        m = jnp.max(work, axis=-1, keepdims=True)
        idx = jnp.min(jnp.where(work == m, lane, float(N_EXPERTS)), axis=-1, keepdims=True)
        work = jnp.where(lane == idx, -jnp.inf, work)
        vals.append(m)
        idxs.append(idx)
    e = [jnp.exp(v - vals[0]) for v in vals]
    denom = functools.reduce(lambda p, q: p + q, e)
    gate_ref[...] = jnp.concatenate(e, axis=1) / denom
    idx_ref[...] = jnp.concatenate(idxs, axis=1).astype(jnp.int32)

    @pl.when(i == 0)
    def _():
        carry[...] = jnp.zeros_like(carry)

    member = jnp.where(work == -jnp.inf, 1.0, 0.0)
    before = jnp.dot(tri_ref[...], member.astype(BF16), preferred_element_type=F32) + carry[...]
    ranks = [jnp.sum(jnp.where(lane == idx, before, 0.0), axis=-1, keepdims=True) for idx in idxs]
    rank_ref[...] = jnp.concatenate(ranks, axis=1).astype(jnp.int32)
    carry[...] += jnp.sum(member, axis=0, keepdims=True)
    cnt_ref[...] = carry[...].astype(jnp.int32)


def _merge(oa, ob, gates, xp, xs, wa, wb, wo, g2, rw, rb):
    npt, d = xp.shape
    t = npt + xs.shape[0]
    tm = ROW_TILE
    npb = npt // tm
    tri = jnp.asarray(np.tril(np.ones((tm, tm)), -1), BF16)
    row = lambda width: pl.BlockSpec((tm, width), lambda i: (i, 0))
    whole = lambda a: pl.BlockSpec(a.shape, lambda i: (0,) * a.ndim)
    return pl.pallas_call(
        functools.partial(_merge_kernel, npb),
        grid=(t // tm,),
        in_specs=[
            row(oa.shape[1]), row(ob.shape[1]), row(gates.shape[1]),
            pl.BlockSpec((tm, d), lambda i: (jnp.minimum(i, npb - 1), 0)),
            pl.BlockSpec((tm, d), lambda i: (jnp.maximum(i - npb, 0), 0)),
            whole(wa), whole(wb), whole(wo), whole(g2), whole(rw), whole(rb), whole(tri),
        ],
        out_specs=[row(d), pl.BlockSpec((tm * SUB, LANE), lambda i: (i, 0)),
                   row(TOP_K), row(TOP_K), row(TOP_K),
                   pl.BlockSpec((1, N_EXPERTS), lambda i: (0, 0))],
        out_shape=[jax.ShapeDtypeStruct((t, d), F32), jax.ShapeDtypeStruct((t * SUB, LANE), F32),
                   jax.ShapeDtypeStruct((t, TOP_K), jnp.int32), jax.ShapeDtypeStruct((t, TOP_K), F32),
                   jax.ShapeDtypeStruct((t, TOP_K), jnp.int32),
                   jax.ShapeDtypeStruct((1, N_EXPERTS), jnp.int32)],
        scratch_shapes=[pltpu.VMEM((1, N_EXPERTS), F32)],
        compiler_params=_params("arbitrary"),
        name="merge_router",
    )(oa, ob, gates, xp, xs, wa, wb, wo, g2, rw, rb, tri)


def _route(top_idx, rank, counts, bm):
    t = top_idx.shape[0]
    a = t * TOP_K
    counts = counts.reshape(N_EXPERTS)
    padded = (counts + bm - 1) // bm * bm
    pad_end = jnp.cumsum(padded)
    pad_start = pad_end - padded
    n_blocks = -(-a // bm) + N_EXPERTS
    onehot = top_idx[..., None] == jnp.arange(N_EXPERTS, dtype=jnp.int32)
    slot = rank + jnp.sum(jnp.where(onehot, pad_start, 0), axis=-1)
    tok = jnp.broadcast_to(jnp.arange(t, dtype=jnp.int32)[:, None], (t, TOP_K))
    slot_tok = jnp.zeros((n_blocks * bm,), jnp.int32).at[slot.reshape(-1)].set(tok.reshape(-1))
    starts = jnp.arange(n_blocks, dtype=jnp.int32) * bm
    block_e = jnp.minimum(jnp.sum(starts[:, None] >= pad_end[None, :], axis=-1), N_EXPERTS - 1)
    n_used = (pad_end[-1] // bm).astype(jnp.int32).reshape(1)
    return slot.reshape(-1), slot_tok, block_e.astype(jnp.int32), n_used


def _wprep_kernel(w_ref, eo_ref, o_ref):
    half = o_ref.shape[2] // 2
    two = 2 * LANE
    for c in range(w_ref.shape[2] // two):
        blk = w_ref[0, :, c * two:(c + 1) * two].astype(BF16)
        r = jnp.dot(blk, eo_ref[...], preferred_element_type=F32)
        o_ref[0, :, c * LANE:(c + 1) * LANE] = r[:, :LANE].astype(BF16)
        o_ref[0, :, half + c * LANE:half + (c + 1) * LANE] = r[:, LANE:].astype(BF16)


def _wprep(w):
    ne, d, f2 = w.shape
    eo = np.zeros((2 * LANE, 2 * LANE))
    eo[2 * np.arange(LANE), np.arange(LANE)] = 1.0
    eo[2 * np.arange(LANE) + 1, LANE + np.arange(LANE)] = 1.0
    return pl.pallas_call(
        _wprep_kernel,
        grid=(ne,),
        in_specs=[pl.BlockSpec((1, d, f2), lambda e: (e, 0, 0)),
                  pl.BlockSpec((2 * LANE, 2 * LANE), lambda e: (0, 0))],
        out_specs=pl.BlockSpec((1, d, f2), lambda e: (e, 0, 0)),
        out_shape=jax.ShapeDtypeStruct(w.shape, BF16),
        compiler_params=_params("arbitrary"),
        name="expert_weight_layout",
    )(w, jnp.asarray(eo, BF16))


def _tile_rows(r):
    return pl.ds(pl.multiple_of(r * SUB, SUB), SUB)


def _store_token_tiled(ref, val):
    n = val.shape[0]
    for c in range(SUB):
        ref[pl.ds(c, n, stride=SUB), :] = val[:, c * LANE:(c + 1) * LANE]


def _load_token_tiled(ref, n):
    return jnp.concatenate([ref[pl.ds(c, n, stride=SUB), :] for c in range(SUB)], axis=1)


def _ffn_kernel(be_ref, nu_ref, tok_ref, tokn_ref, h2_hbm, wg_ref, wu_ref, wd_ref, bg_ref, bu_ref,
                bd_ref, y_ref, xbuf, sem):
    bm = y_ref.shape[0] // SUB
    b = pl.program_id(0)
    n_used = nu_ref[0]
    slot = lax.rem(b, 2)

    def issue(t_ref, s):
        def body(j, carry):
            pltpu.make_async_copy(h2_hbm.at[_tile_rows(t_ref[0, 0, j])], xbuf.at[s, _tile_rows(j)],
                                  sem.at[s]).start()
            return carry
        lax.fori_loop(0, bm, body, 0, unroll=8)

    @pl.when(b == 0)
    def _():
        issue(tok_ref, 0)

    @pl.when(b + 1 < n_used)
    def _():
        issue(tokn_ref, 1 - slot)

    @pl.when(b < n_used)
    def _():
        pltpu.make_async_copy(h2_hbm.at[pl.ds(0, bm * SUB)], xbuf.at[slot], sem.at[slot]).wait()
        x = _load_token_tiled(xbuf.at[slot], bm).astype(BF16)
        hg = jnp.dot(x, wg_ref[0], preferred_element_type=F32) + bg_ref[0]
        hu = jnp.dot(x, wu_ref[0], preferred_element_type=F32) + bu_ref[0]
        glu = jnp.minimum(hg, SWIGLU_LIMIT)
        lin = jnp.clip(hu, -SWIGLU_LIMIT, SWIGLU_LIMIT)
        act = glu * _sigmoid(SWIGLU_ALPHA * glu) * (lin + 1.0)
        y = jnp.dot(act.astype(BF16), wd_ref[0], preferred_element_type=F32) + bd_ref[0]
        _store_token_tiled(y_ref, y)

    @pl.when(b >= n_used)
    def _():
        y_ref[...] = jnp.zeros_like(y_ref)


def _expert_ffn(h2, slot_tok, block_e, n_used, wgu, wd, bg, bu, bd):
    bm = FFN_BM
    d, f = wd.shape[2], wd.shape[1]
    assert d == SUB * LANE
    nb = block_e.shape[0]
    tok3 = slot_tok.reshape(nb, 1, bm)
    wspec = lambda a: pl.BlockSpec((1,) + a.shape[1:], lambda b, be, nu: (be[b], 0, 0))
    wg_spec = pl.BlockSpec((1, d, f), lambda b, be, nu: (be[b], 0, 0))
    wu_spec = pl.BlockSpec((1, d, f), lambda b, be, nu: (be[b], 0, 1))
    smem_blk = lambda f: pl.BlockSpec((1, 1, bm), f, memory_space=pltpu.SMEM)
    return pl.pallas_call(
        _ffn_kernel,
        grid_spec=pltpu.PrefetchScalarGridSpec(
            num_scalar_prefetch=2,
            grid=(nb,),
            in_specs=[
                smem_blk(lambda b, be, nu: (b, 0, 0)),
                smem_blk(lambda b, be, nu: (jnp.minimum(b + 1, nb - 1), 0, 0)),
                pl.BlockSpec(memory_space=pl.ANY),
                wg_spec, wu_spec, wspec(wd), wspec(bg), wspec(bu), wspec(bd),
            ],
            out_specs=pl.BlockSpec((bm * SUB, LANE), lambda b, be, nu: (b, 0)),
            scratch_shapes=[pltpu.VMEM((2, bm * SUB, LANE), F32), pltpu.SemaphoreType.DMA((2,))],
        ),
        out_shape=jax.ShapeDtypeStruct((nb * bm * SUB, LANE), F32),
        compiler_params=_params("arbitrary"),
        name="expert_ffn",
    )(block_e, n_used, tok3, tok3, h2, wgu, wgu, wd, bg, bu, bd)


def _combine_kernel(npb, sl_ref, sln_ref, gate_ref, x1_ref, gf_ref, y_hbm, yp_ref, ys_ref, ybuf, sem):
    tm = x1_ref.shape[0]
    i = pl.program_id(0)
    n = pl.num_programs(0)
    slot = lax.rem(i, 2)

    def issue(s_ref, s):
        def body(tok, carry):
            for k in range(TOP_K):
                src = y_hbm.at[_tile_rows(s_ref[0, 0, tok * TOP_K + k])]
                pltpu.make_async_copy(src, ybuf.at[s, k, _tile_rows(tok)], sem.at[s]).start()
            return carry
        lax.fori_loop(0, tm, body, 0, unroll=2)

    @pl.when(i == 0)
    def _():
        issue(sl_ref, 0)

    @pl.when(i + 1 < n)
    def _():
        issue(sln_ref, 1 - slot)

    for k in range(TOP_K):
        pltpu.make_async_copy(y_hbm.at[pl.ds(0, tm * SUB)], ybuf.at[slot, k], sem.at[slot]).wait()
    y = jnp.zeros(x1_ref.shape, F32)
    gate = gate_ref[...]
    for k in range(TOP_K):
        y = y + gate[:, k:k + 1] * _load_token_tiled(ybuf.at[slot, k], tm)
    out = _rms(x1_ref[...] + y, gf_ref[...])

    @pl.when(i < npb)
    def _():
        yp_ref[...] = out

    @pl.when(i >= npb)
    def _():
        ys_ref[...] = out


def _combine(y_slots, slot, gate, x1, gf, npt):
    t, d = x1.shape
    tm = CMB_TM
    nt = t // tm
    npb = npt // tm
    slot3 = slot.reshape(nt, 1, tm * TOP_K)
    smem_blk = lambda f: pl.BlockSpec((1, 1, tm * TOP_K), f, memory_space=pltpu.SMEM)
    return pl.pallas_call(
        functools.partial(_combine_kernel, npb),
        grid=(nt,),
        in_specs=[
            smem_blk(lambda i: (i, 0, 0)),
            smem_blk(lambda i: (jnp.minimum(i + 1, nt - 1), 0, 0)),
            pl.BlockSpec((tm, TOP_K), lambda i: (i, 0)),
            pl.BlockSpec((tm, d), lambda i: (i, 0)),
            pl.BlockSpec((1, d), lambda i: (0, 0)),
            pl.BlockSpec(memory_space=pl.ANY),
        ],
        out_specs=[
            pl.BlockSpec((tm, d), lambda i: (jnp.minimum(i, npb - 1), 0)),
            pl.BlockSpec((tm, d), lambda i: (jnp.maximum(i - npb, 0), 0)),
        ],
        out_shape=[jax.ShapeDtypeStruct((npt, d), F32), jax.ShapeDtypeStruct((t - npt, d), F32)],
        scratch_shapes=[pltpu.VMEM((2, TOP_K, tm * SUB, LANE), F32), pltpu.SemaphoreType.DMA((2,))],
        compiler_params=_params("arbitrary"),
        name="combine_norm",
    )(slot3, slot3, gate, x1, gf, y_slots)


def _inproj_columns(d):
    da_q, db = HA_Q * HEAD_DIM, HB * HEAD_DIM
    da_kv = HA_KV * HEAD_DIM
    q_b_end = da_q + db
    g_b_end = q_b_end + 2 * d
    ka0 = g_b_end
    va0 = ka0 + da_kv
    kb0 = va0 + da_kv
    vb0 = kb0 + db
    dup = lambda c0: np.concatenate(
        [np.tile(np.arange(c0 + h * HEAD_DIM, c0 + (h + 1) * HEAD_DIM), 2) for h in range(HA_KV)])
    cols = np.concatenate([np.arange(0, g_b_end), dup(ka0), dup(va0), np.arange(kb0, vb0 + db)])
    segs = ((0, da_q), (da_q, db), (q_b_end, 2 * d), (g_b_end, 2 * da_kv),
            (g_b_end + 2 * da_kv, 2 * da_kv), (g_b_end + 4 * da_kv, db), (g_b_end + 4 * da_kv + db, db))
    return cols, segs


def kernel(x_prompt, x_sample, meta_tokens, norm1_g, w_in, attn_sink, rel_pos_bias, w_branch_a,
           w_branch_b, w_out, norm2_g, router_w, router_b, w_gate_up, b_gate_up, w_down, b_down,
           final_norm_g):
    assert norm1_g.shape[0] == 1, "single-layer trunk"
    bp, lp, d = x_prompt.shape
    bs, ls, _ = x_sample.shape
    npt, nst = bp * lp, bs * ls
    assert npt % ROW_TILE == 0 and nst % ROW_TILE == 0
    xp = x_prompt.reshape(npt, d)
    xs = x_sample.reshape(nst, d)

    cols, segs = _inproj_columns(d)
    w_p = w_in[0][:, cols].astype(BF16)
    g1 = norm1_g[0].reshape(1, d)
    qa, qb, gates, ka2, va2, kb, vb = _inproj(xp, xs, g1, w_p, segs)
    kv0 = segs[3][0]
    meta_kv = _meta_proj(meta_tokens, g1, w_p[:, kv0:])
    w_kv = 2 * HA_KV * HEAD_DIM
    meta_kv = _pad_meta(meta_kv)
    km2, vm2 = meta_kv[:, :w_kv], meta_kv[:, w_kv:2 * w_kv]
    kbm, vbm = meta_kv[:, 2 * w_kv:2 * w_kv + HB * HEAD_DIM], meta_kv[:, 2 * w_kv + HB * HEAD_DIM:]

    oa = _window_attention(qa, ka2, va2, km2, vm2, attn_sink[0].astype(F32), npt, lp, ls)
    ob = _nbr_attention(qb, kb, vb, kbm, vbm, rel_pos_bias[0], npt, lp, ls)

    x1, h2, top_idx, gate, rank, counts = _merge(
        oa, ob, gates, xp, xs, w_branch_a[0].astype(BF16), w_branch_b[0].astype(BF16),
        w_out[0].astype(BF16), norm2_g[0].reshape(1, d), router_w[0], router_b[0].reshape(1, -1))

    slot, slot_tok, block_e, n_used = _route(top_idx, rank, counts, FFN_BM)
    bgu = b_gate_up[0]
    y_slots = _expert_ffn(
        h2, slot_tok, block_e, n_used, _wprep(w_gate_up[0]), w_down[0].astype(BF16),
        bgu[:, None, 0::2], bgu[:, None, 1::2], b_down[0][:, None, :])

    yp, ys = _combine(y_slots, slot, gate, x1, final_norm_g.reshape(1, d), npt)
    return yp.reshape(bp, lp, d), ys.reshape(bs, ls, d)
```

```python
import functools

import numpy as np
import jax
import jax.numpy as jnp
from jax import lax
from jax.experimental import pallas as pl
from jax.experimental.pallas import tpu as pltpu

F32 = jnp.float32
BF16 = jnp.bfloat16

N_META = 16
GRID_W = 64
HA_Q = 8
HA_KV = 2
HEAD_DIM = 64
WINDOW = 128
HB = 8
NB_ROWS = 8
NB_COLS = 16
N_EXPERTS = 32
TOP_K = 4
SWIGLU_ALPHA = 1.702
SWIGLU_LIMIT = 7.0
NORM_EPS = 1e-5
NEG_INF = -1e30
SCALE = HEAD_DIM ** -0.5

LANE = 128
SUB = 8
ROW_TILE = 512
WIN_TQ = 256
NBR_ROWS = 4
FFN_BM = 512
DMA_UNROLL = 8
CMB_TM = 128
VMEM_LIMIT = 56 * 1024 * 1024


def _rms(x, g):
    return x * lax.rsqrt(jnp.mean(x * x, axis=-1, keepdims=True) + NORM_EPS) * g


def _sigmoid(z):
    return 1.0 / (1.0 + jnp.exp(-z))


def _params(*sem):
    return pltpu.CompilerParams(dimension_semantics=sem, vmem_limit_bytes=VMEM_LIMIT)


def _inproj_kernel(npb, segs, xp_ref, xs_ref, g_ref, w_ref, *o_refs):
    i = pl.program_id(0)
    x = jnp.where(i < npb, xp_ref[...], xs_ref[...])
    h = _rms(x, g_ref[...]).astype(BF16)
    for (c0, width), o_ref in zip(segs, o_refs):
        for off in range(0, width, 512):
            n = min(512, width - off)
            o_ref[:, off:off + n] = jnp.dot(
                h, w_ref[:, c0 + off:c0 + off + n], preferred_element_type=F32).astype(BF16)


def _inproj(xp, xs, g, w, segs):
    npt, d = xp.shape
    t = npt + xs.shape[0]
    tm = ROW_TILE
    npb = npt // tm
    return pl.pallas_call(
        functools.partial(_inproj_kernel, npb, segs),
        grid=(t // tm,),
        in_specs=[
            pl.BlockSpec((tm, d), lambda i: (jnp.minimum(i, npb - 1), 0)),
            pl.BlockSpec((tm, d), lambda i: (jnp.maximum(i - npb, 0), 0)),
            pl.BlockSpec((1, d), lambda i: (0, 0)),
            pl.BlockSpec(w.shape, lambda i: (0, 0)),
        ],
        out_specs=[pl.BlockSpec((tm, width), lambda i: (i, 0)) for _, width in segs],
        out_shape=[jax.ShapeDtypeStruct((t, width), BF16) for _, width in segs],
        compiler_params=_params("arbitrary"),
        name="inproj",
    )(xp, xs, g, w)


def _meta_kernel(m_ref, g_ref, w_ref, o_ref):
    h = _rms(m_ref[...], g_ref[...]).astype(BF16)
    o_ref[...] = jnp.dot(h, w_ref[...], preferred_element_type=F32).astype(BF16)


def _meta_proj(meta, g, w):
    return pl.pallas_call(
        _meta_kernel,
        out_shape=jax.ShapeDtypeStruct((meta.shape[0], w.shape[1]), BF16),
        name="meta_proj",
    )(meta, g, w)


def _window_bias(tq):
    slopes = 2.0 ** (-8.0 * np.arange(1, HA_Q + 1) / HA_Q)
    q = np.arange(tq)[:, None]
    col = np.arange(tq + 2 * WINDOW)[None, :]
    dist = np.abs(col - WINDOW - q)
    band = np.where((dist <= WINDOW)[None], -slopes[:, None, None] * dist[None], NEG_INF)
    full = np.concatenate([band, np.broadcast_to(_meta_cols(), (HA_Q, tq, LANE))], axis=-1)
    return jnp.asarray(full.reshape(HA_Q // 2, 2 * tq, -1), F32)


def _meta_cols():
    return np.where(np.arange(LANE) < N_META, 0.0, NEG_INF)


def _pad_meta(a):
    return jnp.pad(a, ((0, LANE - a.shape[0]), (0, 0)))


def _seq_pos(start, npt, lp, ls):
    is_p = start < npt
    pos = jnp.where(is_p, lax.rem(start, lp), lax.rem(jnp.maximum(start - npt, 0), ls))
    return pos, jnp.where(is_p, lp, ls)


def _window_kernel(npt, lp, ls, sink_ref, qa_ref, kp_ref, kc_ref, kn_ref, vp_ref, vc_ref, vn_ref,
                   km_ref, vm_ref, bias_ref, o_ref):
    tq = qa_ref.shape[0]
    nk = tq + 2 * WINDOW
    pos, seq_len = _seq_pos(pl.program_id(0) * tq, npt, lp, ls)
    pen_prev = jnp.where(pos == 0, NEG_INF, 0.0).astype(F32)
    pen_next = jnp.where(pos + tq == seq_len, NEG_INF, 0.0).astype(F32)
    col = lax.broadcasted_iota(jnp.int32, (1, nk + LANE), 1)
    pen = (jnp.where(col < WINDOW, pen_prev, 0.0)
           + jnp.where((col >= WINDOW + tq) & (col < nk), pen_next, 0.0))
    lo = lax.broadcasted_iota(jnp.int32, (1, LANE), 1) < HEAD_DIM
    top = lax.broadcasted_iota(jnp.int32, (2 * tq, 1), 0) < tq
    for i in range(HA_Q // 2):
        hk = (2 * i) // (HA_Q // HA_KV)
        sl = slice(i * LANE, (i + 1) * LANE)
        ksl = slice(hk * LANE, (hk + 1) * LANE)
        q = qa_ref[:, sl] * jnp.asarray(SCALE, BF16)
        zero = jnp.zeros_like(q)
        qq = jnp.concatenate([jnp.where(lo, q, zero), jnp.where(lo, zero, q)], axis=0)
        k_all = jnp.concatenate([kp_ref[:, ksl], kc_ref[:, ksl], kn_ref[:, ksl], km_ref[:, ksl]], axis=0)
        v_all = jnp.concatenate([vp_ref[:, ksl], vc_ref[:, ksl], vn_ref[:, ksl], vm_ref[:, ksl]], axis=0)
        s = lax.dot_general(qq, k_all, (((1,), (1,)), ((), ())), preferred_element_type=F32)
        s = s + bias_ref[i] + pen
        sink = jnp.where(top, sink_ref[2 * i], sink_ref[2 * i + 1])
        m = jnp.maximum(jnp.max(s, axis=-1, keepdims=True), sink)
        p = jnp.exp(s - m)
        denom = jnp.sum(p, axis=-1, keepdims=True) + jnp.exp(sink - m)
        o = jnp.dot(p.astype(BF16), v_all, preferred_element_type=F32) / denom
        o_ref[:, sl] = jnp.where(lo, o[:tq], o[tq:]).astype(BF16)


def _window_attention(qa, ka2, va2, km2, vm2, sink, npt, lp, ls):
    t = qa.shape[0]
    tq = WIN_TQ
    assert lp % tq == 0 and ls % tq == 0 and tq == 2 * WINDOW
    nhalf = t // WINDOW
    bias = _window_bias(tq)
    kw = ka2.shape[1]
    prev = pl.BlockSpec((WINDOW, kw), lambda j: (jnp.maximum(2 * j - 1, 0), 0))
    cur = pl.BlockSpec((tq, kw), lambda j: (j, 0))
    nxt = pl.BlockSpec((WINDOW, kw), lambda j: (jnp.minimum(2 * j + 2, nhalf - 1), 0))
    whole = lambda a: pl.BlockSpec(a.shape, lambda j: (0,) * a.ndim)
    return pl.pallas_call(
        functools.partial(_window_kernel, npt, lp, ls),
        grid=(t // tq,),
        in_specs=[
            pl.BlockSpec(memory_space=pltpu.SMEM),
            pl.BlockSpec((tq, qa.shape[1]), lambda j: (j, 0)),
            prev, cur, nxt, prev, cur, nxt,
            whole(km2), whole(vm2), whole(bias),
        ],
        out_specs=pl.BlockSpec((tq, qa.shape[1]), lambda j: (j, 0)),
        out_shape=jax.ShapeDtypeStruct(qa.shape, BF16),
        compiler_params=_params("arbitrary"),
        name="window_attn",
    )(sink, qa, ka2, ka2, ka2, va2, va2, va2, km2, vm2, bias)


def _nbr_tables(rpb):
    r, w = NBR_ROWS, GRID_W
    qc = np.arange(w)[:, None]
    kcol = np.arange(w)[None, :]
    col_start = np.clip(qc - NB_COLS // 2, 0, w - NB_COLS)
    col_ok = (kcol >= col_start) & (kcol < col_start + NB_COLS)
    sel = np.zeros((2 * NB_COLS - 1, w, w), np.float32)
    ci, ki = np.nonzero(col_ok)
    sel[ki - ci + NB_COLS - 1, ci, ki] = 1.0
    per_row = jnp.einsum("hrj,jck->hrck", rpb.astype(F32), jnp.asarray(sel), precision=lax.Precision.HIGHEST)
    per_row = jnp.where(jnp.asarray(col_ok), per_row, NEG_INF)
    off = NB_ROWS - 1 - NBR_ROWS
    vals = jnp.concatenate(
        [jnp.concatenate([per_row[:, b - a + off] for b in range(3 * r)], axis=-1) for a in range(r)], axis=1)
    meta = jnp.broadcast_to(jnp.asarray(_meta_cols(), F32), (HB, r * w, LANE))
    bias = jnp.concatenate([vals, meta], axis=-1)
    a = np.repeat(np.arange(r), w)[:, None]
    b = np.repeat(np.arange(3 * r), w)[None, :]
    first = NBR_ROWS
    ok = np.stack([
        (b >= first) & (b < first + NB_ROWS) & (a >= 0),
        (b >= a) & (b < a + NB_ROWS),
        (b >= 0) & (b < NB_ROWS) & (a >= 0),
    ])
    ok = np.concatenate([ok, np.ones((3, r * w, LANE), bool)], axis=-1)
    return bias, jnp.asarray(np.where(ok, 0.0, NEG_INF), F32)


def _nbr_kernel(npg, gp, gs, qb_ref, kp_ref, kc_ref, kn_ref, vp_ref, vc_ref, vn_ref, km_ref, vm_ref,
                bias_ref, mask_ref, o_ref):
    tq = qb_ref.shape[0]
    g = pl.program_id(0)
    is_p = g < npg
    per_seq = jnp.where(is_p, gp, gs)
    r = jnp.where(is_p, lax.rem(g, gp), lax.rem(jnp.maximum(g - npg, 0), gs))
    pat = jnp.where(r == 0, 0, jnp.where(r == per_seq - 1, 2, 1))
    row_mask = mask_ref[pat]
    lo = lax.broadcasted_iota(jnp.int32, (1, LANE), 1) < HEAD_DIM
    for i in range(HB // 2):
        sl = slice(i * LANE, (i + 1) * LANE)
        q = qb_ref[:, sl] * jnp.asarray(SCALE, BF16)
        zero = jnp.zeros_like(q)
        qq = jnp.concatenate([jnp.where(lo, q, zero), jnp.where(lo, zero, q)], axis=0)
        k_all = jnp.concatenate([kp_ref[:, sl], kc_ref[:, sl], kn_ref[:, sl], km_ref[:, sl]], axis=0)
        v_all = jnp.concatenate([vp_ref[:, sl], vc_ref[:, sl], vn_ref[:, sl], vm_ref[:, sl]], axis=0)
        s = lax.dot_general(qq, k_all, (((1,), (1,)), ((), ())), preferred_element_type=F32)
        halves = []
        for hf in range(2):
            sh = s[hf * tq:(hf + 1) * tq] + bias_ref[2 * i + hf] + row_mask
            m = jnp.max(sh, axis=-1, keepdims=True)
            p = jnp.exp(sh - m)
            denom = jnp.sum(p, axis=-1, keepdims=True)
            halves.append(jnp.dot(p.astype(BF16), v_all, preferred_element_type=F32) / denom)
        o_ref[:, sl] = jnp.where(lo, halves[0], halves[1]).astype(BF16)


def _nbr_attention(qb, kb, vb, kbm, vbm, rpb, npt, lp, ls):
    t, width = qb.shape
    tq = NBR_ROWS * GRID_W
    rows_p, rows_s = lp // GRID_W, ls // GRID_W
    assert rows_p % NBR_ROWS == 0 and rows_s % NBR_ROWS == 0 and min(rows_p, rows_s) >= NB_ROWS
    assert NB_ROWS == 2 * NBR_ROWS
    ng = t // tq
    bias, mask = _nbr_tables(rpb)
    prev = pl.BlockSpec((tq, width), lambda g: (jnp.maximum(g - 1, 0), 0))
    cur = pl.BlockSpec((tq, width), lambda g: (g, 0))
    nxt = pl.BlockSpec((tq, width), lambda g: (jnp.minimum(g + 1, ng - 1), 0))
    whole = lambda a: pl.BlockSpec(a.shape, lambda g: (0,) * a.ndim)
    return pl.pallas_call(
        functools.partial(_nbr_kernel, npt // tq, rows_p // NBR_ROWS, rows_s // NBR_ROWS),
        grid=(ng,),
        in_specs=[cur, prev, cur, nxt, prev, cur, nxt, whole(kbm), whole(vbm), whole(bias), whole(mask)],
        out_specs=cur,
        out_shape=jax.ShapeDtypeStruct(qb.shape, BF16),
        compiler_params=_params("arbitrary"),
        name="nbr_attn",
    )(qb, kb, kb, kb, vb, vb, vb, kbm, vbm, bias, mask)


def _merge_kernel(npb, oa_ref, ob_ref, g_ref, xp_ref, xs_ref, wa_ref, wb_ref, wo_ref, g2_ref,
                  rw_ref, rb_ref, tri_ref, x1_ref, h2_ref, idx_ref, gate_ref, rank_ref, cnt_ref, carry):
    i = pl.program_id(0)
    d = x1_ref.shape[1]
    x = jnp.where(i < npb, xp_ref[...], xs_ref[...])
    a = jnp.dot(oa_ref[...], wa_ref[...], preferred_element_type=F32)
    b = jnp.dot(ob_ref[...], wb_ref[...], preferred_element_type=F32)
    mix = _sigmoid(g_ref[:, :d].astype(F32)) * a + _sigmoid(g_ref[:, d:].astype(F32)) * b
    x1 = x + jnp.dot(mix.astype(BF16), wo_ref[...], preferred_element_type=F32)
    x1_ref[...] = x1
    h2 = _rms(x1, g2_ref[...])
    _store_token_tiled(h2_ref, h2)
    h_hi = h2.astype(BF16)
    h_lo = (h2 - h_hi.astype(F32)).astype(BF16)
    r_hi = jnp.dot(h_hi, rw_ref[...], preferred_element_type=F32)
    r_lo = jnp.dot(h_lo, rw_ref[:, :LANE], preferred_element_type=F32)
    logits = r_hi[:, :LANE] + r_hi[:, LANE:] + r_lo + rb_ref[...]

    lane = lax.broadcasted_iota(jnp.int32, logits.shape, 1).astype(F32)
    work = logits
    member = jnp.zeros(logits.shape, F32)
    vals, idxs = [], []
    for _ in range(TOP_K):
        m = jnp.max(work, axis=-1, keepdims=True)
        idx = jnp.min(jnp.where(work == m, lane, float(LANE)), axis=-1, keepdims=True)
        hit = lane == idx
        work = jnp.where(hit, -jnp.inf, work)
        member = jnp.where(hit, 1.0, member)
        vals.append(m)
        idxs.append(idx)
    e = [jnp.exp(v - vals[0]) for v in vals]
    denom = functools.reduce(lambda p, q: p + q, e)
    gate_ref[...] = jnp.concatenate(e, axis=1) / denom
    idx_ref[...] = jnp.concatenate(idxs, axis=1).astype(jnp.int32)

    @pl.when(i == 0)
    def _():
        carry[...] = jnp.zeros_like(carry)

    before = jnp.dot(tri_ref[...], member.astype(BF16), preferred_element_type=F32) + carry[...]
    ranks = [jnp.sum(jnp.where(lane == idx, before, 0.0), axis=-1, keepdims=True) for idx in idxs]
    rank_ref[...] = jnp.concatenate(ranks, axis=1).astype(jnp.int32)
    carry[...] += jnp.sum(member, axis=0, keepdims=True)
    cnt_ref[...] = carry[...].astype(jnp.int32)


def _merge(oa, ob, gates, xp, xs, wa, wb, wo, g2, rw, rb):
    npt, d = xp.shape
    t = npt + xs.shape[0]
    tm = ROW_TILE
    npb = npt // tm
    tri = jnp.asarray(np.tril(np.ones((tm, tm)), -1), BF16)
    row = lambda width: pl.BlockSpec((tm, width), lambda i: (i, 0))
    whole = lambda a: pl.BlockSpec(a.shape, lambda i: (0,) * a.ndim)
    return pl.pallas_call(
        functools.partial(_merge_kernel, npb),
        grid=(t // tm,),
        in_specs=[
            row(oa.shape[1]), row(ob.shape[1]), row(gates.shape[1]),
            pl.BlockSpec((tm, d), lambda i: (jnp.minimum(i, npb - 1), 0)),
            pl.BlockSpec((tm, d), lambda i: (jnp.maximum(i - npb, 0), 0)),
            whole(wa), whole(wb), whole(wo), whole(g2), whole(rw), whole(rb), whole(tri),
        ],
        out_specs=[row(d), pl.BlockSpec((tm * SUB, LANE), lambda i: (i, 0)),
                   row(TOP_K), row(TOP_K), row(TOP_K),
                   pl.BlockSpec((1, LANE), lambda i: (0, 0))],
        out_shape=[jax.ShapeDtypeStruct((t, d), F32), jax.ShapeDtypeStruct((t * SUB, LANE), F32),
                   jax.ShapeDtypeStruct((t, TOP_K), jnp.int32), jax.ShapeDtypeStruct((t, TOP_K), F32),
                   jax.ShapeDtypeStruct((t, TOP_K), jnp.int32),
                   jax.ShapeDtypeStruct((1, LANE), jnp.int32)],
        scratch_shapes=[pltpu.VMEM((1, LANE), F32)],
        compiler_params=_params("arbitrary"),
        name="merge_router",
    )(oa, ob, gates, xp, xs, wa, wb, wo, g2, rw, rb, tri)


def _route(top_idx, rank, counts, bm):
    t = top_idx.shape[0]
    a = t * TOP_K
    counts = counts.reshape(-1)[:N_EXPERTS]
    padded = (counts + bm - 1) // bm * bm
    pad_end = jnp.cumsum(padded)
    pad_start = pad_end - padded
    n_blocks = -(-a // bm) + N_EXPERTS
    onehot = top_idx[..., None] == jnp.arange(N_EXPERTS, dtype=jnp.int32)
    slot = rank + jnp.sum(jnp.where(onehot, pad_start, 0), axis=-1)
    tok = jnp.broadcast_to(jnp.arange(t, dtype=jnp.int32)[:, None], (t, TOP_K))
    slot_tok = jnp.zeros((n_blocks * bm,), jnp.int32).at[slot.reshape(-1)].set(tok.reshape(-1))
    starts = jnp.arange(n_blocks, dtype=jnp.int32) * bm
    block_e = jnp.minimum(jnp.sum(starts[:, None] >= pad_end[None, :], axis=-1), N_EXPERTS - 1)
    n_used = (pad_end[-1] // bm).astype(jnp.int32).reshape(1)
    return slot.reshape(-1), slot_tok, block_e.astype(jnp.int32), n_used


def _wprep_kernel(w_ref, eo_ref, o_ref):
    half = o_ref.shape[2] // 2
    two = 2 * LANE
    for c in range(w_ref.shape[2] // two):
        blk = w_ref[0, :, c * two:(c + 1) * two].astype(BF16)
        r = jnp.dot(blk, eo_ref[...], preferred_element_type=F32)
        o_ref[0, :, c * LANE:(c + 1) * LANE] = r[:, :LANE].astype(BF16)
        o_ref[0, :, half + c * LANE:half + (c + 1) * LANE] = r[:, LANE:].astype(BF16)


def _wprep(w):
    ne, d, f2 = w.shape
    eo = np.zeros((2 * LANE, 2 * LANE))
    eo[2 * np.arange(LANE), np.arange(LANE)] = 1.0
    eo[2 * np.arange(LANE) + 1, LANE + np.arange(LANE)] = 1.0
    return pl.pallas_call(
        _wprep_kernel,
        grid=(ne,),
        in_specs=[pl.BlockSpec((1, d, f2), lambda e: (e, 0, 0)),
                  pl.BlockSpec((2 * LANE, 2 * LANE), lambda e: (0, 0))],
        out_specs=pl.BlockSpec((1, d, f2), lambda e: (e, 0, 0)),
        out_shape=jax.ShapeDtypeStruct(w.shape, BF16),
        compiler_params=_params("arbitrary"),
        name="expert_weight_layout",
    )(w, jnp.asarray(eo, BF16))


def _tile_rows(r):
    return pl.ds(pl.multiple_of(r * SUB, SUB), SUB)


def _store_token_tiled(ref, val):
    n = val.shape[0]
    for c in range(SUB):
        ref[pl.ds(c, n, stride=SUB), :] = val[:, c * LANE:(c + 1) * LANE]


def _load_token_tiled(ref, n):
    return jnp.concatenate([ref[pl.ds(c, n, stride=SUB), :] for c in range(SUB)], axis=1)


def _ffn_kernel(be_ref, nu_ref, tok_ref, tokn_ref, h2_hbm, wg_ref, wu_ref, wd_ref, bg_ref, bu_ref,
                bd_ref, y_ref, xbuf, sem):
    bm = y_ref.shape[0] // SUB
    b = pl.program_id(0)
    n_used = nu_ref[0]
    slot = lax.rem(b, 2)

    def issue(t_ref, s):
        def body(g, carry):
            for u in range(DMA_UNROLL):
                j = g * DMA_UNROLL + u
                pltpu.make_async_copy(h2_hbm.at[_tile_rows(t_ref[0, 0, j])], xbuf.at[s, _tile_rows(j)],
                                      sem.at[s]).start(priority=u % 2)
            return carry
        lax.fori_loop(0, bm // DMA_UNROLL, body, 0)

    @pl.when(b == 0)
    def _():
        issue(tok_ref, 0)

    @pl.when(b + 1 < n_used)
    def _():
        issue(tokn_ref, 1 - slot)

    @pl.when(b < n_used)
    def _():
        pltpu.make_async_copy(h2_hbm.at[pl.ds(0, bm * SUB)], xbuf.at[slot], sem.at[slot]).wait()
        x = _load_token_tiled(xbuf.at[slot], bm).astype(BF16)
        hg = jnp.dot(x, wg_ref[0], preferred_element_type=F32) + bg_ref[0]
        hu = jnp.dot(x, wu_ref[0], preferred_element_type=F32) + bu_ref[0]
        glu = jnp.minimum(hg, SWIGLU_LIMIT)
        lin = jnp.clip(hu, -SWIGLU_LIMIT, SWIGLU_LIMIT)
        act = glu * _sigmoid(SWIGLU_ALPHA * glu) * (lin + 1.0)
        y = jnp.dot(act.astype(BF16), wd_ref[0], preferred_element_type=F32) + bd_ref[0]
        _store_token_tiled(y_ref, y)

    @pl.when(b >= n_used)
    def _():
        y_ref[...] = jnp.zeros_like(y_ref)


def _expert_ffn(h2, slot_tok, block_e, n_used, wgu, wd, bg, bu, bd):
    bm = FFN_BM
    d, f = wd.shape[2], wd.shape[1]
    assert d == SUB * LANE
    nb = block_e.shape[0]
    tok3 = slot_tok.reshape(nb, 1, bm)
    wspec = lambda a: pl.BlockSpec((1,) + a.shape[1:], lambda b, be, nu: (be[b], 0, 0))
    wg_spec = pl.BlockSpec((1, d, f), lambda b, be, nu: (be[b], 0, 0))
    wu_spec = pl.BlockSpec((1, d, f), lambda b, be, nu: (be[b], 0, 1))
    smem_blk = lambda f: pl.BlockSpec((1, 1, bm), f, memory_space=pltpu.SMEM)
    return pl.pallas_call(
        _ffn_kernel,
        grid_spec=pltpu.PrefetchScalarGridSpec(
            num_scalar_prefetch=2,
            grid=(nb,),
            in_specs=[
                smem_blk(lambda b, be, nu: (b, 0, 0)),
                smem_blk(lambda b, be, nu: (jnp.minimum(b + 1, nb - 1), 0, 0)),
                pl.BlockSpec(memory_space=pl.ANY),
                wg_spec, wu_spec, wspec(wd), wspec(bg), wspec(bu), wspec(bd),
            ],
            out_specs=pl.BlockSpec((bm * SUB, LANE), lambda b, be, nu: (b, 0)),
            scratch_shapes=[pltpu.VMEM((2, bm * SUB, LANE), F32), pltpu.SemaphoreType.DMA((2,))],
        ),
        out_shape=jax.ShapeDtypeStruct((nb * bm * SUB, LANE), F32),
        compiler_params=_params("arbitrary"),
        name="expert_ffn",
    )(block_e, n_used, tok3, tok3, h2, wgu, wgu, wd, bg, bu, bd)


def _combine_kernel(npb, sl_ref, sln_ref, gate_ref, x1_ref, gf_ref, y_hbm, yp_ref, ys_ref, ybuf, sem):
    tm = x1_ref.shape[0]
    i = pl.program_id(0)
    n = pl.num_programs(0)
    slot = lax.rem(i, 2)

    def issue(s_ref, s):
        def body(tok, carry):
            for k in range(TOP_K):
                src = y_hbm.at[_tile_rows(s_ref[0, 0, tok * TOP_K + k])]
                pltpu.make_async_copy(src, ybuf.at[s, k, _tile_rows(tok)], sem.at[s]).start(priority=k % 2)
            return carry
        lax.fori_loop(0, tm, body, 0, unroll=2)

    @pl.when(i == 0)
    def _():
        issue(sl_ref, 0)

    @pl.when(i + 1 < n)
    def _():
        issue(sln_ref, 1 - slot)

    for k in range(TOP_K):
        pltpu.make_async_copy(y_hbm.at[pl.ds(0, tm * SUB)], ybuf.at[slot, k], sem.at[slot]).wait()
    y = jnp.zeros(x1_ref.shape, F32)
    gate = gate_ref[...]
    for k in range(TOP_K):
        y = y + gate[:, k:k + 1] * _load_token_tiled(ybuf.at[slot, k], tm)
    out = _rms(x1_ref[...] + y, gf_ref[...])

    @pl.when(i < npb)
    def _():
        yp_ref[...] = out

    @pl.when(i >= npb)
    def _():
        ys_ref[...] = out


def _combine(y_slots, slot, gate, x1, gf, npt):
    t, d = x1.shape
    tm = CMB_TM
    nt = t // tm
    npb = npt // tm
    slot3 = slot.reshape(nt, 1, tm * TOP_K)
    smem_blk = lambda f: pl.BlockSpec((1, 1, tm * TOP_K), f, memory_space=pltpu.SMEM)
    return pl.pallas_call(
        functools.partial(_combine_kernel, npb),
        grid=(nt,),
        in_specs=[
            smem_blk(lambda i: (i, 0, 0)),
            smem_blk(lambda i: (jnp.minimum(i + 1, nt - 1), 0, 0)),
            pl.BlockSpec((tm, TOP_K), lambda i: (i, 0)),
            pl.BlockSpec((tm, d), lambda i: (i, 0)),
            pl.BlockSpec((1, d), lambda i: (0, 0)),
            pl.BlockSpec(memory_space=pl.ANY),
        ],
        out_specs=[
            pl.BlockSpec((tm, d), lambda i: (jnp.minimum(i, npb - 1), 0)),
            pl.BlockSpec((tm, d), lambda i: (jnp.maximum(i - npb, 0), 0)),
        ],
        out_shape=[jax.ShapeDtypeStruct((npt, d), F32), jax.ShapeDtypeStruct((t - npt, d), F32)],
        scratch_shapes=[pltpu.VMEM((2, TOP_K, tm * SUB, LANE), F32), pltpu.SemaphoreType.DMA((2,))],
        compiler_params=_params("arbitrary"),
        name="combine_norm",
    )(slot3, slot3, gate, x1, gf, y_slots)


def _inproj_columns(d):
    da_q, db = HA_Q * HEAD_DIM, HB * HEAD_DIM
    da_kv = HA_KV * HEAD_DIM
    q_b_end = da_q + db
    g_b_end = q_b_end + 2 * d
    ka0 = g_b_end
    va0 = ka0 + da_kv
    kb0 = va0 + da_kv
    vb0 = kb0 + db
    dup = lambda c0: np.concatenate(
        [np.tile(np.arange(c0 + h * HEAD_DIM, c0 + (h + 1) * HEAD_DIM), 2) for h in range(HA_KV)])
    cols = np.concatenate([np.arange(0, g_b_end), dup(ka0), dup(va0), np.arange(kb0, vb0 + db)])
    segs = ((0, da_q), (da_q, db), (q_b_end, 2 * d), (g_b_end, 2 * da_kv),
            (g_b_end + 2 * da_kv, 2 * da_kv), (g_b_end + 4 * da_kv, db), (g_b_end + 4 * da_kv + db, db))
    return cols, segs


def kernel(x_prompt, x_sample, meta_tokens, norm1_g, w_in, attn_sink, rel_pos_bias, w_branch_a,
           w_branch_b, w_out, norm2_g, router_w, router_b, w_gate_up, b_gate_up, w_down, b_down,
           final_norm_g):
    assert norm1_g.shape[0] == 1, "single-layer trunk"
    bp, lp, d = x_prompt.shape
    bs, ls, _ = x_sample.shape
    npt, nst = bp * lp, bs * ls
    assert npt % ROW_TILE == 0 and nst % ROW_TILE == 0
    xp = x_prompt.reshape(npt, d)
    xs = x_sample.reshape(nst, d)

    cols, segs = _inproj_columns(d)
    w_p = w_in[0][:, cols].astype(BF16)
    g1 = norm1_g[0].reshape(1, d)
    qa, qb, gates, ka2, va2, kb, vb = _inproj(xp, xs, g1, w_p, segs)
    kv0 = segs[3][0]
    meta_kv = _meta_proj(meta_tokens, g1, w_p[:, kv0:])
    w_kv = 2 * HA_KV * HEAD_DIM
    meta_kv = _pad_meta(meta_kv)
    km2, vm2 = meta_kv[:, :w_kv], meta_kv[:, w_kv:2 * w_kv]
    kbm, vbm = meta_kv[:, 2 * w_kv:2 * w_kv + HB * HEAD_DIM], meta_kv[:, 2 * w_kv + HB * HEAD_DIM:]

    oa = _window_attention(qa, ka2, va2, km2, vm2, attn_sink[0].astype(F32), npt, lp, ls)
    ob = _nbr_attention(qb, kb, vb, kbm, vbm, rel_pos_bias[0], npt, lp, ls)

    rw = router_w[0].astype(F32)
    rw_hi = rw.astype(BF16)
    rw_lo = (rw - rw_hi.astype(F32)).astype(BF16)
    lane_pad = lambda a, fill: jnp.pad(a, ((0, 0), (0, LANE - a.shape[1])), constant_values=fill)
    rw2 = jnp.concatenate([lane_pad(rw_hi, 0), lane_pad(rw_lo, 0)], axis=1)
    rb = lane_pad(router_b[0].reshape(1, -1).astype(F32), -jnp.inf)
    x1, h2, top_idx, gate, rank, counts = _merge(
        oa, ob, gates, xp, xs, w_branch_a[0].astype(BF16), w_branch_b[0].astype(BF16),
        w_out[0].astype(BF16), norm2_g[0].reshape(1, d), rw2, rb)

    slot, slot_tok, block_e, n_used = _route(top_idx, rank, counts, FFN_BM)
    bgu = b_gate_up[0]
    y_slots = _expert_ffn(
        h2, slot_tok, block_e, n_used, _wprep(w_gate_up[0]), w_down[0].astype(BF16),
        bgu[:, None, 0::2], bgu[:, None, 1::2], b_down[0][:, None, :])

    yp, ys = _combine(y_slots, slot, gate, x1, final_norm_g.reshape(1, d), npt)
    return yp.reshape(bp, lp, d), ys.reshape(bs, ls, d)
```

```python
import functools

import numpy as np
import jax
import jax.numpy as jnp
from jax import lax
from jax.experimental import pallas as pl
from jax.experimental.pallas import tpu as pltpu

F32 = jnp.float32
BF16 = jnp.bfloat16

N_META = 16
GRID_W = 64
HA_Q = 8
HA_KV = 2
HEAD_DIM = 64
WINDOW = 128
HB = 8
NB_ROWS = 8
NB_COLS = 16
N_EXPERTS = 32
TOP_K = 4
SWIGLU_ALPHA = 1.702
SWIGLU_LIMIT = 7.0
NORM_EPS = 1e-5
NEG_INF = -1e30
SCALE = HEAD_DIM ** -0.5

LANE = 128
SUB = 8
ROW_TILE = 512
WIN_TQ = 256
NBR_ROWS = 4
FFN_BM = 512
DMA_UNROLL = 8
CMB_TM = 128
VMEM_LIMIT = 56 * 1024 * 1024


def _rms(x, g):
    return x * lax.rsqrt(jnp.mean(x * x, axis=-1, keepdims=True) + NORM_EPS) * g


def _bf16_part(x):
    bits = lax.bitcast_convert_type(x, jnp.uint32) & jnp.uint32(0xFFFF0000)
    return lax.bitcast_convert_type(bits, F32)


def _sigmoid(z):
    return 1.0 / (1.0 + jnp.exp(-z))


def _params(*sem):
    return pltpu.CompilerParams(dimension_semantics=sem, vmem_limit_bytes=VMEM_LIMIT)


def _inproj_kernel(npb, segs, xp_ref, xs_ref, g_ref, w_ref, *o_refs):
    i = pl.program_id(0)
    x = jnp.where(i < npb, xp_ref[...], xs_ref[...])
    h = _rms(x, g_ref[...]).astype(BF16)
    for (c0, width), o_ref in zip(segs, o_refs):
        for off in range(0, width, 512):
            n = min(512, width - off)
            o_ref[:, off:off + n] = jnp.dot(
                h, w_ref[:, c0 + off:c0 + off + n], preferred_element_type=F32).astype(BF16)


def _inproj(xp, xs, g, w, segs):
    npt, d = xp.shape
    t = npt + xs.shape[0]
    tm = ROW_TILE
    npb = npt // tm
    return pl.pallas_call(
        functools.partial(_inproj_kernel, npb, segs),
        grid=(t // tm,),
        in_specs=[
            pl.BlockSpec((tm, d), lambda i: (jnp.minimum(i, npb - 1), 0)),
            pl.BlockSpec((tm, d), lambda i: (jnp.maximum(i - npb, 0), 0)),
            pl.BlockSpec((1, d), lambda i: (0, 0)),
            pl.BlockSpec(w.shape, lambda i: (0, 0)),
        ],
        out_specs=[pl.BlockSpec((tm, width), lambda i: (i, 0)) for _, width in segs],
        out_shape=[jax.ShapeDtypeStruct((t, width), BF16) for _, width in segs],
        compiler_params=_params("arbitrary"),
        name="inproj",
    )(xp, xs, g, w)


def _meta_kernel(m_ref, g_ref, w_ref, o_ref):
    h = _rms(m_ref[...], g_ref[...]).astype(BF16)
    o_ref[...] = jnp.dot(h, w_ref[...], preferred_element_type=F32).astype(BF16)


def _meta_proj(meta, g, w):
    return pl.pallas_call(
        _meta_kernel,
        out_shape=jax.ShapeDtypeStruct((meta.shape[0], w.shape[1]), BF16),
        name="meta_proj",
    )(meta, g, w)


def _window_bias(tq):
    slopes = 2.0 ** (-8.0 * np.arange(1, HA_Q + 1) / HA_Q)
    q = np.arange(tq)[:, None]
    col = np.arange(tq + 2 * WINDOW)[None, :]
    dist = np.abs(col - WINDOW - q)
    band = np.where((dist <= WINDOW)[None], -slopes[:, None, None] * dist[None], NEG_INF)
    full = np.concatenate([band, np.broadcast_to(_meta_cols(), (HA_Q, tq, LANE))], axis=-1)
    return jnp.asarray(full.reshape(HA_Q // 2, 2 * tq, -1), F32)


def _meta_cols():
    return np.where(np.arange(LANE) < N_META, 0.0, NEG_INF)


def _pad_meta(a):
    return jnp.pad(a, ((0, LANE - a.shape[0]), (0, 0)))


def _seq_pos(start, npt, lp, ls):
    is_p = start < npt
    pos = jnp.where(is_p, lax.rem(start, lp), lax.rem(jnp.maximum(start - npt, 0), ls))
    return pos, jnp.where(is_p, lp, ls)


def _window_kernel(npt, lp, ls, sink_ref, qa_ref, kp_ref, kc_ref, kn_ref, vp_ref, vc_ref, vn_ref,
                   km_ref, vm_ref, bias_ref, o_ref):
    tq = qa_ref.shape[0]
    nk = tq + 2 * WINDOW
    pos, seq_len = _seq_pos(pl.program_id(0) * tq, npt, lp, ls)
    pen_prev = jnp.where(pos == 0, NEG_INF, 0.0).astype(F32)
    pen_next = jnp.where(pos + tq == seq_len, NEG_INF, 0.0).astype(F32)
    col = lax.broadcasted_iota(jnp.int32, (1, nk + LANE), 1)
    pen = (jnp.where(col < WINDOW, pen_prev, 0.0)
           + jnp.where((col >= WINDOW + tq) & (col < nk), pen_next, 0.0))
    lo = lax.broadcasted_iota(jnp.int32, (1, LANE), 1) < HEAD_DIM
    top = lax.broadcasted_iota(jnp.int32, (2 * tq, 1), 0) < tq
    for i in range(HA_Q // 2):
        hk = (2 * i) // (HA_Q // HA_KV)
        sl = slice(i * LANE, (i + 1) * LANE)
        ksl = slice(hk * LANE, (hk + 1) * LANE)
        q = qa_ref[:, sl] * jnp.asarray(SCALE, BF16)
        zero = jnp.zeros_like(q)
        qq = jnp.concatenate([jnp.where(lo, q, zero), jnp.where(lo, zero, q)], axis=0)
        k_all = jnp.concatenate([kp_ref[:, ksl], kc_ref[:, ksl], kn_ref[:, ksl], km_ref[:, ksl]], axis=0)
        v_all = jnp.concatenate([vp_ref[:, ksl], vc_ref[:, ksl], vn_ref[:, ksl], vm_ref[:, ksl]], axis=0)
        s = lax.dot_general(qq, k_all, (((1,), (1,)), ((), ())), preferred_element_type=F32)
        s = s + bias_ref[i] + pen
        sink = jnp.where(top, sink_ref[2 * i], sink_ref[2 * i + 1])
        m = jnp.maximum(jnp.max(s, axis=-1, keepdims=True), sink)
        p = jnp.exp(s - m)
        denom = jnp.sum(p, axis=-1, keepdims=True) + jnp.exp(sink - m)
        o = jnp.dot(p.astype(BF16), v_all, preferred_element_type=F32) / denom
        o_ref[:, sl] = jnp.where(lo, o[:tq], o[tq:]).astype(BF16)


def _window_attention(qa, ka2, va2, km2, vm2, sink, npt, lp, ls):
    t = qa.shape[0]
    tq = WIN_TQ
    assert lp % tq == 0 and ls % tq == 0 and tq == 2 * WINDOW
    nhalf = t // WINDOW
    bias = _window_bias(tq)
    kw = ka2.shape[1]
    prev = pl.BlockSpec((WINDOW, kw), lambda j: (jnp.maximum(2 * j - 1, 0), 0))
    cur = pl.BlockSpec((tq, kw), lambda j: (j, 0))
    nxt = pl.BlockSpec((WINDOW, kw), lambda j: (jnp.minimum(2 * j + 2, nhalf - 1), 0))
    whole = lambda a: pl.BlockSpec(a.shape, lambda j: (0,) * a.ndim)
    return pl.pallas_call(
        functools.partial(_window_kernel, npt, lp, ls),
        grid=(t // tq,),
        in_specs=[
            pl.BlockSpec(memory_space=pltpu.SMEM),
            pl.BlockSpec((tq, qa.shape[1]), lambda j: (j, 0)),
            prev, cur, nxt, prev, cur, nxt,
            whole(km2), whole(vm2), whole(bias),
        ],
        out_specs=pl.BlockSpec((tq, qa.shape[1]), lambda j: (j, 0)),
        out_shape=jax.ShapeDtypeStruct(qa.shape, BF16),
        compiler_params=_params("arbitrary"),
        name="window_attn",
    )(sink, qa, ka2, ka2, ka2, va2, va2, va2, km2, vm2, bias)


def _nbr_tables(rpb):
    r, w = NBR_ROWS, GRID_W
    qc = np.arange(w)[:, None]
    kcol = np.arange(w)[None, :]
    col_start = np.clip(qc - NB_COLS // 2, 0, w - NB_COLS)
    col_ok = (kcol >= col_start) & (kcol < col_start + NB_COLS)
    sel = np.zeros((2 * NB_COLS - 1, w, w), np.float32)
    ci, ki = np.nonzero(col_ok)
    sel[ki - ci + NB_COLS - 1, ci, ki] = 1.0
    per_row = jnp.einsum("hrj,jck->hrck", rpb.astype(F32), jnp.asarray(sel), precision=lax.Precision.HIGHEST)
    per_row = jnp.where(jnp.asarray(col_ok), per_row, NEG_INF)
    off = NB_ROWS - 1 - NBR_ROWS
    vals = jnp.concatenate(
        [jnp.concatenate([per_row[:, b - a + off] for b in range(3 * r)], axis=-1) for a in range(r)], axis=1)
    meta = jnp.broadcast_to(jnp.asarray(_meta_cols(), F32), (HB, r * w, LANE))
    bias = jnp.concatenate([vals, meta], axis=-1)
    a = np.repeat(np.arange(r), w)[:, None]
    b = np.repeat(np.arange(3 * r), w)[None, :]
    first = NBR_ROWS
    ok = np.stack([
        (b >= first) & (b < first + NB_ROWS) & (a >= 0),
        (b >= a) & (b < a + NB_ROWS),
        (b >= 0) & (b < NB_ROWS) & (a >= 0),
    ])
    ok = np.concatenate([ok, np.ones((3, r * w, LANE), bool)], axis=-1)
    return bias, jnp.asarray(np.where(ok, 0.0, NEG_INF), F32)


def _nbr_kernel(npg, gp, gs, qb_ref, kp_ref, kc_ref, kn_ref, vp_ref, vc_ref, vn_ref, km_ref, vm_ref,
                bias_ref, mask_ref, o_ref):
    tq = qb_ref.shape[0]
    g = pl.program_id(0)
    is_p = g < npg
    per_seq = jnp.where(is_p, gp, gs)
    r = jnp.where(is_p, lax.rem(g, gp), lax.rem(jnp.maximum(g - npg, 0), gs))
    pat = jnp.where(r == 0, 0, jnp.where(r == per_seq - 1, 2, 1))
    row_mask = mask_ref[pat]
    lo = lax.broadcasted_iota(jnp.int32, (1, LANE), 1) < HEAD_DIM
    for i in range(HB // 2):
        sl = slice(i * LANE, (i + 1) * LANE)
        q = qb_ref[:, sl] * jnp.asarray(SCALE, BF16)
        zero = jnp.zeros_like(q)
        qq = jnp.concatenate([jnp.where(lo, q, zero), jnp.where(lo, zero, q)], axis=0)
        k_all = jnp.concatenate([kp_ref[:, sl], kc_ref[:, sl], kn_ref[:, sl], km_ref[:, sl]], axis=0)
        v_all = jnp.concatenate([vp_ref[:, sl], vc_ref[:, sl], vn_ref[:, sl], vm_ref[:, sl]], axis=0)
        s = lax.dot_general(qq, k_all, (((1,), (1,)), ((), ())), preferred_element_type=F32)
        halves = []
        for hf in range(2):
            sh = s[hf * tq:(hf + 1) * tq] + bias_ref[2 * i + hf] + row_mask
            m = jnp.max(sh, axis=-1, keepdims=True)
            p = jnp.exp(sh - m)
            denom = jnp.sum(p, axis=-1, keepdims=True)
            halves.append(jnp.dot(p.astype(BF16), v_all, preferred_element_type=F32) / denom)
        o_ref[:, sl] = jnp.where(lo, halves[0], halves[1]).astype(BF16)


def _nbr_attention(qb, kb, vb, kbm, vbm, rpb, npt, lp, ls):
    t, width = qb.shape
    tq = NBR_ROWS * GRID_W
    rows_p, rows_s = lp // GRID_W, ls // GRID_W
    assert rows_p % NBR_ROWS == 0 and rows_s % NBR_ROWS == 0 and min(rows_p, rows_s) >= NB_ROWS
    assert NB_ROWS == 2 * NBR_ROWS
    ng = t // tq
    bias, mask = _nbr_tables(rpb)
    prev = pl.BlockSpec((tq, width), lambda g: (jnp.maximum(g - 1, 0), 0))
    cur = pl.BlockSpec((tq, width), lambda g: (g, 0))
    nxt = pl.BlockSpec((tq, width), lambda g: (jnp.minimum(g + 1, ng - 1), 0))
    whole = lambda a: pl.BlockSpec(a.shape, lambda g: (0,) * a.ndim)
    return pl.pallas_call(
        functools.partial(_nbr_kernel, npt // tq, rows_p // NBR_ROWS, rows_s // NBR_ROWS),
        grid=(ng,),
        in_specs=[cur, prev, cur, nxt, prev, cur, nxt, whole(kbm), whole(vbm), whole(bias), whole(mask)],
        out_specs=cur,
        out_shape=jax.ShapeDtypeStruct(qb.shape, BF16),
        compiler_params=_params("arbitrary"),
        name="nbr_attn",
    )(qb, kb, kb, kb, vb, vb, vb, kbm, vbm, bias, mask)


def _merge_kernel(npb, oa_ref, ob_ref, g_ref, xp_ref, xs_ref, wa_ref, wb_ref, wo_ref, g2_ref,
                  rw_ref, rb_ref, tri_ref, x1_ref, h2_ref, idx_ref, gate_ref, rank_ref, cnt_ref, carry):
    i = pl.program_id(0)
    d = x1_ref.shape[1]
    x = jnp.where(i < npb, xp_ref[...], xs_ref[...])
    a = jnp.dot(oa_ref[...], wa_ref[...], preferred_element_type=F32)
    b = jnp.dot(ob_ref[...], wb_ref[...], preferred_element_type=F32)
    mix = _sigmoid(g_ref[:, :d].astype(F32)) * a + _sigmoid(g_ref[:, d:].astype(F32)) * b
    x1 = x + jnp.dot(mix.astype(BF16), wo_ref[...], preferred_element_type=F32)
    x1_ref[...] = x1
    h2 = _rms(x1, g2_ref[...])
    _store_token_tiled(h2_ref, h2)
    h_top = _bf16_part(h2)
    h_hi = h_top.astype(BF16)
    h_lo = (h2 - h_top).astype(BF16)
    r_hi = jnp.dot(h_hi, rw_ref[...], preferred_element_type=F32)
    r_lo = jnp.dot(h_lo, rw_ref[:, :LANE], preferred_element_type=F32)
    logits = r_hi[:, :LANE] + r_hi[:, LANE:] + r_lo + rb_ref[...]

    lane = lax.broadcasted_iota(jnp.int32, logits.shape, 1).astype(F32)
    work = logits
    member = jnp.zeros(logits.shape, F32)
    vals, idxs = [], []
    for _ in range(TOP_K):
        m = jnp.max(work, axis=-1, keepdims=True)
        idx = jnp.min(jnp.where(work == m, lane, float(LANE)), axis=-1, keepdims=True)
        hit = lane == idx
        work = jnp.where(hit, -jnp.inf, work)
        member = jnp.where(hit, 1.0, member)
        vals.append(m)
        idxs.append(idx)
    e = [jnp.exp(v - vals[0]) for v in vals]
    denom = functools.reduce(lambda p, q: p + q, e)
    gate_ref[...] = jnp.concatenate(e, axis=1) / denom
    idx_ref[...] = jnp.concatenate(idxs, axis=1).astype(jnp.int32)

    @pl.when(i == 0)
    def _():
        carry[...] = jnp.zeros_like(carry)

    before = jnp.dot(tri_ref[...], member.astype(BF16), preferred_element_type=F32) + carry[...]
    ranks = [jnp.sum(jnp.where(lane == idx, before, 0.0), axis=-1, keepdims=True) for idx in idxs]
    rank_ref[...] = jnp.concatenate(ranks, axis=1).astype(jnp.int32)
    carry[...] += jnp.sum(member, axis=0, keepdims=True)
    cnt_ref[...] = carry[...].astype(jnp.int32)


def _merge(oa, ob, gates, xp, xs, wa, wb, wo, g2, rw, rb):
    npt, d = xp.shape
    t = npt + xs.shape[0]
    tm = ROW_TILE
    npb = npt // tm
    tri = jnp.asarray(np.tril(np.ones((tm, tm)), -1), BF16)
    row = lambda width: pl.BlockSpec((tm, width), lambda i: (i, 0))
    whole = lambda a: pl.BlockSpec(a.shape, lambda i: (0,) * a.ndim)
    return pl.pallas_call(
        functools.partial(_merge_kernel, npb),
        grid=(t // tm,),
        in_specs=[
            row(oa.shape[1]), row(ob.shape[1]), row(gates.shape[1]),
            pl.BlockSpec((tm, d), lambda i: (jnp.minimum(i, npb - 1), 0)),
            pl.BlockSpec((tm, d), lambda i: (jnp.maximum(i - npb, 0), 0)),
            whole(wa), whole(wb), whole(wo), whole(g2), whole(rw), whole(rb), whole(tri),
        ],
        out_specs=[row(d), pl.BlockSpec((tm * SUB, LANE), lambda i: (i, 0)),
                   row(TOP_K), row(TOP_K), row(TOP_K),
                   pl.BlockSpec((1, LANE), lambda i: (0, 0))],
        out_shape=[jax.ShapeDtypeStruct((t, d), F32), jax.ShapeDtypeStruct((t * SUB, LANE), F32),
                   jax.ShapeDtypeStruct((t, TOP_K), jnp.int32), jax.ShapeDtypeStruct((t, TOP_K), F32),
                   jax.ShapeDtypeStruct((t, TOP_K), jnp.int32),
                   jax.ShapeDtypeStruct((1, LANE), jnp.int32)],
        scratch_shapes=[pltpu.VMEM((1, LANE), F32)],
        compiler_params=_params("arbitrary"),
        name="merge_router",
    )(oa, ob, gates, xp, xs, wa, wb, wo, g2, rw, rb, tri)


def _route(top_idx, rank, counts, bm):
    t = top_idx.shape[0]
    a = t * TOP_K
    counts = counts.reshape(-1)[:N_EXPERTS]
    padded = (counts + bm - 1) // bm * bm
    pad_end = jnp.cumsum(padded)
    pad_start = pad_end - padded
    n_blocks = -(-a // bm) + N_EXPERTS + 1
    onehot = top_idx[..., None] == jnp.arange(N_EXPERTS, dtype=jnp.int32)
    slot = rank + jnp.sum(jnp.where(onehot, pad_start, 0), axis=-1)
    tok = jnp.broadcast_to(jnp.arange(t, dtype=jnp.int32)[:, None], (t, TOP_K))
    slot_tok = jnp.zeros((n_blocks * bm,), jnp.int32).at[slot.reshape(-1)].set(tok.reshape(-1))
    starts = jnp.arange(n_blocks, dtype=jnp.int32) * bm
    block_e = jnp.minimum(jnp.sum(starts[:, None] >= pad_end[None, :], axis=-1), N_EXPERTS - 1)
    n_used = (pad_end[-1] // bm).astype(jnp.int32).reshape(1)
    return slot.reshape(-1), slot_tok, block_e.astype(jnp.int32), n_used


def _wprep_kernel(w_ref, eo_ref, o_ref):
    half = o_ref.shape[2] // 2
    two = 2 * LANE
    for c in range(w_ref.shape[2] // two):
        blk = w_ref[0, :, c * two:(c + 1) * two].astype(BF16)
        r = jnp.dot(blk, eo_ref[...], preferred_element_type=F32)
        o_ref[0, :, c * LANE:(c + 1) * LANE] = r[:, :LANE].astype(BF16)
        o_ref[0, :, half + c * LANE:half + (c + 1) * LANE] = r[:, LANE:].astype(BF16)


def _wprep(w):
    ne, d, f2 = w.shape
    eo = np.zeros((2 * LANE, 2 * LANE))
    eo[2 * np.arange(LANE), np.arange(LANE)] = 1.0
    eo[2 * np.arange(LANE) + 1, LANE + np.arange(LANE)] = 1.0
    return pl.pallas_call(
        _wprep_kernel,
        grid=(ne,),
        in_specs=[pl.BlockSpec((1, d, f2), lambda e: (e, 0, 0)),
                  pl.BlockSpec((2 * LANE, 2 * LANE), lambda e: (0, 0))],
        out_specs=pl.BlockSpec((1, d, f2), lambda e: (e, 0, 0)),
        out_shape=jax.ShapeDtypeStruct(w.shape, BF16),
        compiler_params=_params("arbitrary"),
        name="expert_weight_layout",
    )(w, jnp.asarray(eo, BF16))


def _tile_rows(r):
    return pl.ds(pl.multiple_of(r * SUB, SUB), SUB)


def _store_token_tiled(ref, val):
    n = val.shape[0]
    for c in range(SUB):
        ref[pl.ds(c, n, stride=SUB), :] = val[:, c * LANE:(c + 1) * LANE]


def _load_token_tiled(ref, n):
    return jnp.concatenate([ref[pl.ds(c, n, stride=SUB), :] for c in range(SUB)], axis=1)


def _ffn_kernel(be_ref, nu_ref, tok_ref, tokn_ref, h2_hbm, wg_ref, wu_ref, wd_ref, bg_ref, bu_ref,
                bd_ref, y_ref, xbuf, sem):
    bm = y_ref.shape[0] // SUB
    b = pl.program_id(0)
    n_used = nu_ref[0]
    slot = lax.rem(b, 2)

    def issue(t_ref, s):
        def body(g, carry):
            for u in range(DMA_UNROLL):
                j = g * DMA_UNROLL + u
                pltpu.make_async_copy(h2_hbm.at[_tile_rows(t_ref[0, 0, j])], xbuf.at[s, _tile_rows(j)],
                                      sem.at[s]).start(priority=u % 2)
            return carry
        lax.fori_loop(0, bm // DMA_UNROLL, body, 0)

    def drain(s):
        pltpu.make_async_copy(h2_hbm.at[pl.ds(0, bm * SUB)], xbuf.at[s], sem.at[s]).wait()

    @pl.when(b == 0)
    def _():
        issue(tok_ref, 0)

    @pl.when(b == n_used)
    def _():
        drain(slot)

    @pl.when(b < n_used)
    def _():
        drain(slot)
        x = _load_token_tiled(xbuf.at[slot], bm).astype(BF16)
        hg = jnp.dot(x, wg_ref[0], preferred_element_type=F32) + bg_ref[0]
        hu = jnp.dot(x, wu_ref[0], preferred_element_type=F32) + bu_ref[0]
        for j in range(bm):
            pltpu.make_async_copy(h2_hbm.at[_tile_rows(tokn_ref[0, 0, j])], xbuf.at[1 - slot, _tile_rows(j)],
                                  sem.at[1 - slot]).start(priority=j % 2)
        glu = jnp.minimum(hg, SWIGLU_LIMIT)
        lin = jnp.clip(hu, -SWIGLU_LIMIT, SWIGLU_LIMIT)
        act = glu * _sigmoid(SWIGLU_ALPHA * glu) * (lin + 1.0)
        y = jnp.dot(act.astype(BF16), wd_ref[0], preferred_element_type=F32) + bd_ref[0]
        _store_token_tiled(y_ref, y)

    @pl.when(b >= n_used)
    def _():
        y_ref[...] = jnp.zeros_like(y_ref)


def _expert_ffn(h2, slot_tok, block_e, n_used, wgu, wd, bg, bu, bd):
    bm = FFN_BM
    d, f = wd.shape[2], wd.shape[1]
    assert d == SUB * LANE
    nb = block_e.shape[0]
    tok3 = slot_tok.reshape(nb, 1, bm)
    wspec = lambda a: pl.BlockSpec((1,) + a.shape[1:], lambda b, be, nu: (be[b], 0, 0))
    wg_spec = pl.BlockSpec((1, d, f), lambda b, be, nu: (be[b], 0, 0))
    wu_spec = pl.BlockSpec((1, d, f), lambda b, be, nu: (be[b], 0, 1))
    smem_blk = lambda f: pl.BlockSpec((1, 1, bm), f, memory_space=pltpu.SMEM)
    return pl.pallas_call(
        _ffn_kernel,
        grid_spec=pltpu.PrefetchScalarGridSpec(
            num_scalar_prefetch=2,
            grid=(nb,),
            in_specs=[
                smem_blk(lambda b, be, nu: (b, 0, 0)),
                smem_blk(lambda b, be, nu: (jnp.minimum(b + 1, nb - 1), 0, 0)),
                pl.BlockSpec(memory_space=pl.ANY),
                wg_spec, wu_spec, wspec(wd), wspec(bg), wspec(bu), wspec(bd),
            ],
            out_specs=pl.BlockSpec((bm * SUB, LANE), lambda b, be, nu: (b, 0)),
            scratch_shapes=[pltpu.VMEM((2, bm * SUB, LANE), F32), pltpu.SemaphoreType.DMA((2,))],
        ),
        out_shape=jax.ShapeDtypeStruct((nb * bm * SUB, LANE), F32),
        compiler_params=_params("arbitrary"),
        name="expert_ffn",
    )(block_e, n_used, tok3, tok3, h2, wgu, wgu, wd, bg, bu, bd)


def _combine_kernel(npb, sl_ref, sln_ref, gate_ref, x1_ref, gf_ref, y_hbm, yp_ref, ys_ref, ybuf, sem):
    tm = x1_ref.shape[0]
    i = pl.program_id(0)
    n = pl.num_programs(0)
    slot = lax.rem(i, 2)

    def issue(s_ref, s):
        def body(tok, carry):
            for k in range(TOP_K):
                src = y_hbm.at[_tile_rows(s_ref[0, 0, tok * TOP_K + k])]
                pltpu.make_async_copy(src, ybuf.at[s, k, _tile_rows(tok)], sem.at[s]).start(priority=k % 2)
            return carry
        lax.fori_loop(0, tm, body, 0, unroll=2)

    @pl.when(i == 0)
    def _():
        issue(sl_ref, 0)

    @pl.when(i + 1 < n)
    def _():
        issue(sln_ref, 1 - slot)

    for k in range(TOP_K):
        pltpu.make_async_copy(y_hbm.at[pl.ds(0, tm * SUB)], ybuf.at[slot, k], sem.at[slot]).wait()
    y = jnp.zeros(x1_ref.shape, F32)
    gate = gate_ref[...]
    for k in range(TOP_K):
        y = y + gate[:, k:k + 1] * _load_token_tiled(ybuf.at[slot, k], tm)
    out = _rms(x1_ref[...] + y, gf_ref[...])

    @pl.when(i < npb)
    def _():
        yp_ref[...] = out

    @pl.when(i >= npb)
    def _():
        ys_ref[...] = out


def _combine(y_slots, slot, gate, x1, gf, npt):
    t, d = x1.shape
    tm = CMB_TM
    nt = t // tm
    npb = npt // tm
    slot3 = slot.reshape(nt, 1, tm * TOP_K)
    smem_blk = lambda f: pl.BlockSpec((1, 1, tm * TOP_K), f, memory_space=pltpu.SMEM)
    return pl.pallas_call(
        functools.partial(_combine_kernel, npb),
        grid=(nt,),
        in_specs=[
            smem_blk(lambda i: (i, 0, 0)),
            smem_blk(lambda i: (jnp.minimum(i + 1, nt - 1), 0, 0)),
            pl.BlockSpec((tm, TOP_K), lambda i: (i, 0)),
            pl.BlockSpec((tm, d), lambda i: (i, 0)),
            pl.BlockSpec((1, d), lambda i: (0, 0)),
            pl.BlockSpec(memory_space=pl.ANY),
        ],
        out_specs=[
            pl.BlockSpec((tm, d), lambda i: (jnp.minimum(i, npb - 1), 0)),
            pl.BlockSpec((tm, d), lambda i: (jnp.maximum(i - npb, 0), 0)),
        ],
        out_shape=[jax.ShapeDtypeStruct((npt, d), F32), jax.ShapeDtypeStruct((t - npt, d), F32)],
        scratch_shapes=[pltpu.VMEM((2, TOP_K, tm * SUB, LANE), F32), pltpu.SemaphoreType.DMA((2,))],
        compiler_params=_params("arbitrary"),
        name="combine_norm",
    )(slot3, slot3, gate, x1, gf, y_slots)


def _inproj_columns(d):
    da_q, db = HA_Q * HEAD_DIM, HB * HEAD_DIM
    da_kv = HA_KV * HEAD_DIM
    q_b_end = da_q + db
    g_b_end = q_b_end + 2 * d
    ka0 = g_b_end
    va0 = ka0 + da_kv
    kb0 = va0 + da_kv
    vb0 = kb0 + db
    dup = lambda c0: np.concatenate(
        [np.tile(np.arange(c0 + h * HEAD_DIM, c0 + (h + 1) * HEAD_DIM), 2) for h in range(HA_KV)])
    cols = np.concatenate([np.arange(0, g_b_end), dup(ka0), dup(va0), np.arange(kb0, vb0 + db)])
    segs = ((0, da_q), (da_q, db), (q_b_end, 2 * d), (g_b_end, 2 * da_kv),
            (g_b_end + 2 * da_kv, 2 * da_kv), (g_b_end + 4 * da_kv, db), (g_b_end + 4 * da_kv + db, db))
    return cols, segs


def kernel(x_prompt, x_sample, meta_tokens, norm1_g, w_in, attn_sink, rel_pos_bias, w_branch_a,
           w_branch_b, w_out, norm2_g, router_w, router_b, w_gate_up, b_gate_up, w_down, b_down,
           final_norm_g):
    assert norm1_g.shape[0] == 1, "single-layer trunk"
    bp, lp, d = x_prompt.shape
    bs, ls, _ = x_sample.shape
    npt, nst = bp * lp, bs * ls
    assert npt % ROW_TILE == 0 and nst % ROW_TILE == 0
    xp = x_prompt.reshape(npt, d)
    xs = x_sample.reshape(nst, d)

    cols, segs = _inproj_columns(d)
    w_p = w_in[0][:, cols].astype(BF16)
    g1 = norm1_g[0].reshape(1, d)
    qa, qb, gates, ka2, va2, kb, vb = _inproj(xp, xs, g1, w_p, segs)
    kv0 = segs[3][0]
    meta_kv = _meta_proj(meta_tokens, g1, w_p[:, kv0:])
    w_kv = 2 * HA_KV * HEAD_DIM
    meta_kv = _pad_meta(meta_kv)
    km2, vm2 = meta_kv[:, :w_kv], meta_kv[:, w_kv:2 * w_kv]
    kbm, vbm = meta_kv[:, 2 * w_kv:2 * w_kv + HB * HEAD_DIM], meta_kv[:, 2 * w_kv + HB * HEAD_DIM:]

    oa = _window_attention(qa, ka2, va2, km2, vm2, attn_sink[0].astype(F32), npt, lp, ls)
    ob = _nbr_attention(qb, kb, vb, kbm, vbm, rel_pos_bias[0], npt, lp, ls)

    rw = router_w[0].astype(F32)
    rw_top = _bf16_part(rw)
    rw_hi = rw_top.astype(BF16)
    rw_lo = (rw - rw_top).astype(BF16)
    lane_pad = lambda a, fill: jnp.pad(a, ((0, 0), (0, LANE - a.shape[1])), constant_values=fill)
    rw2 = jnp.concatenate([lane_pad(rw_hi, 0), lane_pad(rw_lo, 0)], axis=1)
    rb = lane_pad(router_b[0].reshape(1, -1).astype(F32), -jnp.inf)
    x1, h2, top_idx, gate, rank, counts = _merge(
        oa, ob, gates, xp, xs, w_branch_a[0].astype(BF16), w_branch_b[0].astype(BF16),
        w_out[0].astype(BF16), norm2_g[0].reshape(1, d), rw2, rb)

    slot, slot_tok, block_e, n_used = _route(top_idx, rank, counts, FFN_BM)
    bgu = b_gate_up[0]
    y_slots = _expert_ffn(
        h2, slot_tok, block_e, n_used, _wprep(w_gate_up[0]), w_down[0].astype(BF16),
        bgu[:, None, 0::2], bgu[:, None, 1::2], b_down[0][:, None, :])

    yp, ys = _combine(y_slots, slot, gate, x1, final_norm_g.reshape(1, d), npt)
    return yp.reshape(bp, lp, d), ys.reshape(bs, ls, d)
```

```python
import functools

import numpy as np
import jax
import jax.numpy as jnp
from jax import lax
from jax.experimental import pallas as pl
from jax.experimental.pallas import tpu as pltpu

F32 = jnp.float32
BF16 = jnp.bfloat16

N_META = 16
GRID_W = 64
HA_Q = 8
HA_KV = 2
HEAD_DIM = 64
WINDOW = 128
HB = 8
NB_ROWS = 8
NB_COLS = 16
N_EXPERTS = 32
TOP_K = 4
SWIGLU_ALPHA = 1.702
SWIGLU_LIMIT = 7.0
NORM_EPS = 1e-5
NEG_INF = -1e30
SCALE = HEAD_DIM ** -0.5

LANE = 128
SUB = 8
ROW_TILE = 512
WIN_TQ = 256
NBR_ROWS = 4
FFN_BM = 512
DMA_UNROLL = 8
CMB_TM = 128
VMEM_LIMIT = 56 * 1024 * 1024


def _rms(x, g):
    return x * lax.rsqrt(jnp.mean(x * x, axis=-1, keepdims=True) + NORM_EPS) * g


def _bf16_part(x):
    bits = lax.bitcast_convert_type(x, jnp.uint32) & jnp.uint32(0xFFFF0000)
    return lax.bitcast_convert_type(bits, F32)


def _sigmoid(z):
    return 1.0 / (1.0 + jnp.exp(-z))


def _params(*sem):
    return pltpu.CompilerParams(dimension_semantics=sem, vmem_limit_bytes=VMEM_LIMIT)


def _inproj_kernel(npb, segs, xp_ref, xs_ref, g_ref, w_ref, *o_refs):
    i = pl.program_id(0)
    x = jnp.where(i < npb, xp_ref[...], xs_ref[...])
    h = _rms(x, g_ref[...]).astype(BF16)
    for (c0, width), o_ref in zip(segs, o_refs):
        for off in range(0, width, 512):
            n = min(512, width - off)
            o_ref[:, off:off + n] = jnp.dot(
                h, w_ref[:, c0 + off:c0 + off + n], preferred_element_type=F32).astype(BF16)


def _inproj(xp, xs, g, w, segs):
    npt, d = xp.shape
    t = npt + xs.shape[0]
    tm = ROW_TILE
    npb = npt // tm
    return pl.pallas_call(
        functools.partial(_inproj_kernel, npb, segs),
        grid=(t // tm,),
        in_specs=[
            pl.BlockSpec((tm, d), lambda i: (jnp.minimum(i, npb - 1), 0)),
            pl.BlockSpec((tm, d), lambda i: (jnp.maximum(i - npb, 0), 0)),
            pl.BlockSpec((1, d), lambda i: (0, 0)),
            pl.BlockSpec(w.shape, lambda i: (0, 0)),
        ],
        out_specs=[pl.BlockSpec((tm, width), lambda i: (i, 0)) for _, width in segs],
        out_shape=[jax.ShapeDtypeStruct((t, width), BF16) for _, width in segs],
        compiler_params=_params("arbitrary"),
        name="inproj",
    )(xp, xs, g, w)


def _meta_kernel(m_ref, g_ref, w_ref, o_ref):
    h = _rms(m_ref[...], g_ref[...]).astype(BF16)
    o_ref[...] = jnp.dot(h, w_ref[...], preferred_element_type=F32).astype(BF16)


def _meta_proj(meta, g, w):
    return pl.pallas_call(
        _meta_kernel,
        out_shape=jax.ShapeDtypeStruct((meta.shape[0], w.shape[1]), BF16),
        name="meta_proj",
    )(meta, g, w)


def _window_bias(tq):
    slopes = 2.0 ** (-8.0 * np.arange(1, HA_Q + 1) / HA_Q)
    q = np.arange(tq)[:, None]
    col = np.arange(tq + 2 * WINDOW)[None, :]
    dist = np.abs(col - WINDOW - q)
    band = np.where((dist <= WINDOW)[None], -slopes[:, None, None] * dist[None], NEG_INF)
    full = np.concatenate([band, np.broadcast_to(_meta_cols(), (HA_Q, tq, LANE))], axis=-1)
    return jnp.asarray(full.reshape(HA_Q // 2, 2 * tq, -1), F32)


def _meta_cols():
    return np.where(np.arange(LANE) < N_META, 0.0, NEG_INF)


def _pad_meta(a):
    return jnp.pad(a, ((0, LANE - a.shape[0]), (0, 0)))


def _seq_pos(start, npt, lp, ls):
    is_p = start < npt
    pos = jnp.where(is_p, lax.rem(start, lp), lax.rem(jnp.maximum(start - npt, 0), ls))
    return pos, jnp.where(is_p, lp, ls)


def _window_kernel(npt, lp, ls, sink_ref, qa_ref, kp_ref, kc_ref, kn_ref, vp_ref, vc_ref, vn_ref,
                   km_ref, vm_ref, bias_ref, o_ref):
    tq = qa_ref.shape[0]
    nk = tq + 2 * WINDOW
    pos, seq_len = _seq_pos(pl.program_id(0) * tq, npt, lp, ls)
    pen_prev = jnp.where(pos == 0, NEG_INF, 0.0).astype(F32)
    pen_next = jnp.where(pos + tq == seq_len, NEG_INF, 0.0).astype(F32)
    col = lax.broadcasted_iota(jnp.int32, (1, nk + LANE), 1)
    pen = (jnp.where(col < WINDOW, pen_prev, 0.0)
           + jnp.where((col >= WINDOW + tq) & (col < nk), pen_next, 0.0))
    lo = lax.broadcasted_iota(jnp.int32, (1, LANE), 1) < HEAD_DIM
    top = lax.broadcasted_iota(jnp.int32, (2 * tq, 1), 0) < tq
    for i in range(HA_Q // 2):
        hk = (2 * i) // (HA_Q // HA_KV)
        sl = slice(i * LANE, (i + 1) * LANE)
        ksl = slice(hk * LANE, (hk + 1) * LANE)
        q = qa_ref[:, sl] * jnp.asarray(SCALE, BF16)
        zero = jnp.zeros_like(q)
        qq = jnp.concatenate([jnp.where(lo, q, zero), jnp.where(lo, zero, q)], axis=0)
        k_all = jnp.concatenate([kp_ref[:, ksl], kc_ref[:, ksl], kn_ref[:, ksl], km_ref[:, ksl]], axis=0)
        v_all = jnp.concatenate([vp_ref[:, ksl], vc_ref[:, ksl], vn_ref[:, ksl], vm_ref[:, ksl]], axis=0)
        s = lax.dot_general(qq, k_all, (((1,), (1,)), ((), ())), preferred_element_type=F32)
        s = s + bias_ref[i] + pen
        sink = jnp.where(top, sink_ref[2 * i], sink_ref[2 * i + 1])
        m = jnp.maximum(jnp.max(s, axis=-1, keepdims=True), sink)
        p = jnp.exp(s - m)
        denom = jnp.sum(p, axis=-1, keepdims=True) + jnp.exp(sink - m)
        o = jnp.dot(p.astype(BF16), v_all, preferred_element_type=F32) / denom
        o_ref[:, sl] = jnp.where(lo, o[:tq], o[tq:]).astype(BF16)


def _window_attention(qa, ka2, va2, km2, vm2, sink, npt, lp, ls):
    t = qa.shape[0]
    tq = WIN_TQ
    assert lp % tq == 0 and ls % tq == 0 and tq == 2 * WINDOW
    nhalf = t // WINDOW
    bias = _window_bias(tq)
    kw = ka2.shape[1]
    prev = pl.BlockSpec((WINDOW, kw), lambda j: (jnp.maximum(2 * j - 1, 0), 0))
    cur = pl.BlockSpec((tq, kw), lambda j: (j, 0))
    nxt = pl.BlockSpec((WINDOW, kw), lambda j: (jnp.minimum(2 * j + 2, nhalf - 1), 0))
    whole = lambda a: pl.BlockSpec(a.shape, lambda j: (0,) * a.ndim)
    return pl.pallas_call(
        functools.partial(_window_kernel, npt, lp, ls),
        grid=(t // tq,),
        in_specs=[
            pl.BlockSpec(memory_space=pltpu.SMEM),
            pl.BlockSpec((tq, qa.shape[1]), lambda j: (j, 0)),
            prev, cur, nxt, prev, cur, nxt,
            whole(km2), whole(vm2), whole(bias),
        ],
        out_specs=pl.BlockSpec((tq, qa.shape[1]), lambda j: (j, 0)),
        out_shape=jax.ShapeDtypeStruct(qa.shape, BF16),
        compiler_params=_params("arbitrary"),
        name="window_attn",
    )(sink, qa, ka2, ka2, ka2, va2, va2, va2, km2, vm2, bias)


def _nbr_tables(rpb):
    r, w = NBR_ROWS, GRID_W
    qc = np.arange(w)[:, None]
    kcol = np.arange(w)[None, :]
    col_start = np.clip(qc - NB_COLS // 2, 0, w - NB_COLS)
    col_ok = (kcol >= col_start) & (kcol < col_start + NB_COLS)
    sel = np.zeros((2 * NB_COLS - 1, w, w), np.float32)
    ci, ki = np.nonzero(col_ok)
    sel[ki - ci + NB_COLS - 1, ci, ki] = 1.0
    per_row = jnp.einsum("hrj,jck->hrck", rpb.astype(F32), jnp.asarray(sel), precision=lax.Precision.HIGHEST)
    per_row = jnp.where(jnp.asarray(col_ok), per_row, NEG_INF)
    off = NB_ROWS - 1 - NBR_ROWS
    vals = jnp.concatenate(
        [jnp.concatenate([per_row[:, b - a + off] for b in range(3 * r)], axis=-1) for a in range(r)], axis=1)
    meta = jnp.broadcast_to(jnp.asarray(_meta_cols(), F32), (HB, r * w, LANE))
    bias = jnp.concatenate([vals, meta], axis=-1)
    a = np.repeat(np.arange(r), w)[:, None]
    b = np.repeat(np.arange(3 * r), w)[None, :]
    first = NBR_ROWS
    ok = np.stack([
        (b >= first) & (b < first + NB_ROWS) & (a >= 0),
        (b >= a) & (b < a + NB_ROWS),
        (b >= 0) & (b < NB_ROWS) & (a >= 0),
    ])
    ok = np.concatenate([ok, np.ones((3, r * w, LANE), bool)], axis=-1)
    return bias, jnp.asarray(np.where(ok, 0.0, NEG_INF), F32)


def _nbr_kernel(npg, gp, gs, qb_ref, kp_ref, kc_ref, kn_ref, vp_ref, vc_ref, vn_ref, km_ref, vm_ref,
                bias_ref, mask_ref, o_ref):
    tq = qb_ref.shape[0]
    g = pl.program_id(0)
    is_p = g < npg
    per_seq = jnp.where(is_p, gp, gs)
    r = jnp.where(is_p, lax.rem(g, gp), lax.rem(jnp.maximum(g - npg, 0), gs))
    pat = jnp.where(r == 0, 0, jnp.where(r == per_seq - 1, 2, 1))
    row_mask = mask_ref[pat]
    lo = lax.broadcasted_iota(jnp.int32, (1, LANE), 1) < HEAD_DIM
    for i in range(HB // 2):
        sl = slice(i * LANE, (i + 1) * LANE)
        q = qb_ref[:, sl] * jnp.asarray(SCALE, BF16)
        zero = jnp.zeros_like(q)
        qq = jnp.concatenate([jnp.where(lo, q, zero), jnp.where(lo, zero, q)], axis=0)
        k_all = jnp.concatenate([kp_ref[:, sl], kc_ref[:, sl], kn_ref[:, sl], km_ref[:, sl]], axis=0)
        v_all = jnp.concatenate([vp_ref[:, sl], vc_ref[:, sl], vn_ref[:, sl], vm_ref[:, sl]], axis=0)
        s = lax.dot_general(qq, k_all, (((1,), (1,)), ((), ())), preferred_element_type=F32)
        halves = []
        for hf in range(2):
            sh = s[hf * tq:(hf + 1) * tq] + bias_ref[2 * i + hf] + row_mask
            m = jnp.max(sh, axis=-1, keepdims=True)
            p = jnp.exp(sh - m)
            denom = jnp.sum(p, axis=-1, keepdims=True)
            halves.append(jnp.dot(p.astype(BF16), v_all, preferred_element_type=F32) / denom)
        o_ref[:, sl] = jnp.where(lo, halves[0], halves[1]).astype(BF16)


def _nbr_attention(qb, kb, vb, kbm, vbm, rpb, npt, lp, ls):
    t, width = qb.shape
    tq = NBR_ROWS * GRID_W
    rows_p, rows_s = lp // GRID_W, ls // GRID_W
    assert rows_p % NBR_ROWS == 0 and rows_s % NBR_ROWS == 0 and min(rows_p, rows_s) >= NB_ROWS
    assert NB_ROWS == 2 * NBR_ROWS
    ng = t // tq
    bias, mask = _nbr_tables(rpb)
    prev = pl.BlockSpec((tq, width), lambda g: (jnp.maximum(g - 1, 0), 0))
    cur = pl.BlockSpec((tq, width), lambda g: (g, 0))
    nxt = pl.BlockSpec((tq, width), lambda g: (jnp.minimum(g + 1, ng - 1), 0))
    whole = lambda a: pl.BlockSpec(a.shape, lambda g: (0,) * a.ndim)
    return pl.pallas_call(
        functools.partial(_nbr_kernel, npt // tq, rows_p // NBR_ROWS, rows_s // NBR_ROWS),
        grid=(ng,),
        in_specs=[cur, prev, cur, nxt, prev, cur, nxt, whole(kbm), whole(vbm), whole(bias), whole(mask)],
        out_specs=cur,
        out_shape=jax.ShapeDtypeStruct(qb.shape, BF16),
        compiler_params=_params("arbitrary"),
        name="nbr_attn",
    )(qb, kb, kb, kb, vb, vb, vb, kbm, vbm, bias, mask)


def _merge_kernel(npb, oa_ref, ob_ref, g_ref, xp_ref, xs_ref, wa_ref, wb_ref, wo_ref, g2_ref,
                  rw_ref, rb_ref, tri_ref, x1_ref, h2_ref, idx_ref, gate_ref, rank_ref, cnt_ref, carry):
    i = pl.program_id(0)
    d = x1_ref.shape[1]
    x = jnp.where(i < npb, xp_ref[...], xs_ref[...])
    a = jnp.dot(oa_ref[...], wa_ref[...], preferred_element_type=F32)
    b = jnp.dot(ob_ref[...], wb_ref[...], preferred_element_type=F32)
    mix = _sigmoid(g_ref[:, :d].astype(F32)) * a + _sigmoid(g_ref[:, d:].astype(F32)) * b
    x1 = x + jnp.dot(mix.astype(BF16), wo_ref[...], preferred_element_type=F32)
    x1_ref[...] = x1
    h2 = _rms(x1, g2_ref[...])
    _store_token_tiled(h2_ref, h2)
    h_top = _bf16_part(h2)
    h_hi = h_top.astype(BF16)
    h_lo = (h2 - h_top).astype(BF16)
    r_hi = jnp.dot(h_hi, rw_ref[...], preferred_element_type=F32)
    r_lo = jnp.dot(h_lo, rw_ref[:, :LANE], preferred_element_type=F32)
    logits = r_hi[:, :LANE] + r_hi[:, LANE:] + r_lo + rb_ref[...]

    lane = lax.broadcasted_iota(jnp.int32, logits.shape, 1).astype(F32)
    work = logits
    member = jnp.zeros(logits.shape, F32)
    vals, idxs = [], []
    for _ in range(TOP_K):
        m = jnp.max(work, axis=-1, keepdims=True)
        idx = jnp.min(jnp.where(work == m, lane, float(LANE)), axis=-1, keepdims=True)
        hit = lane == idx
        work = jnp.where(hit, -jnp.inf, work)
        member = jnp.where(hit, 1.0, member)
        vals.append(m)
        idxs.append(idx)
    e = [jnp.exp(v - vals[0]) for v in vals]
    denom = functools.reduce(lambda p, q: p + q, e)
    gate_ref[...] = jnp.concatenate(e, axis=1) / denom
    idx_ref[...] = jnp.concatenate(idxs, axis=1).astype(jnp.int32)

    @pl.when(i == 0)
    def _():
        carry[...] = jnp.zeros_like(carry)

    before = jnp.dot(tri_ref[...], member.astype(BF16), preferred_element_type=F32) + carry[...]
    ranks = [jnp.sum(jnp.where(lane == idx, before, 0.0), axis=-1, keepdims=True) for idx in idxs]
    rank_ref[...] = jnp.concatenate(ranks, axis=1).astype(jnp.int32)
    carry[...] += jnp.sum(member, axis=0, keepdims=True)
    cnt_ref[...] = carry[...].astype(jnp.int32)


def _merge(oa, ob, gates, xp, xs, wa, wb, wo, g2, rw, rb):
    npt, d = xp.shape
    t = npt + xs.shape[0]
    tm = ROW_TILE
    npb = npt // tm
    tri = jnp.asarray(np.tril(np.ones((tm, tm)), -1), BF16)
    row = lambda width: pl.BlockSpec((tm, width), lambda i: (i, 0))
    whole = lambda a: pl.BlockSpec(a.shape, lambda i: (0,) * a.ndim)
    return pl.pallas_call(
        functools.partial(_merge_kernel, npb),
        grid=(t // tm,),
        in_specs=[
            row(oa.shape[1]), row(ob.shape[1]), row(gates.shape[1]),
            pl.BlockSpec((tm, d), lambda i: (jnp.minimum(i, npb - 1), 0)),
            pl.BlockSpec((tm, d), lambda i: (jnp.maximum(i - npb, 0), 0)),
            whole(wa), whole(wb), whole(wo), whole(g2), whole(rw), whole(rb), whole(tri),
        ],
        out_specs=[row(d), pl.BlockSpec((tm * SUB, LANE), lambda i: (i, 0)),
                   row(TOP_K), row(TOP_K), row(TOP_K),
                   pl.BlockSpec((1, LANE), lambda i: (0, 0))],
        out_shape=[jax.ShapeDtypeStruct((t, d), F32), jax.ShapeDtypeStruct((t * SUB, LANE), F32),
                   jax.ShapeDtypeStruct((t, TOP_K), jnp.int32), jax.ShapeDtypeStruct((t, TOP_K), F32),
                   jax.ShapeDtypeStruct((t, TOP_K), jnp.int32),
                   jax.ShapeDtypeStruct((1, LANE), jnp.int32)],
        scratch_shapes=[pltpu.VMEM((1, LANE), F32)],
        compiler_params=_params("arbitrary"),
        name="merge_router",
    )(oa, ob, gates, xp, xs, wa, wb, wo, g2, rw, rb, tri)


def _route(top_idx, rank, counts, bm):
    t = top_idx.shape[0]
    a = t * TOP_K
    counts = counts.reshape(-1)[:N_EXPERTS]
    padded = (counts + bm - 1) // bm * bm
    pad_end = jnp.cumsum(padded)
    pad_start = pad_end - padded
    n_blocks = -(-a // bm) + N_EXPERTS
    onehot = top_idx[..., None] == jnp.arange(N_EXPERTS, dtype=jnp.int32)
    slot = (rank + jnp.sum(jnp.where(onehot, pad_start, 0), axis=-1)).reshape(-1)
    j = jnp.arange(bm, dtype=jnp.int32)[None, :]
    fill = (pad_start + counts)[:, None] + j
    pad_slots = jnp.where(fill < pad_end[:, None], fill, slot[0]).reshape(-1).astype(jnp.int32)
    starts = jnp.arange(n_blocks, dtype=jnp.int32) * bm
    block_e = jnp.minimum(jnp.sum(starts[:, None] >= pad_end[None, :], axis=-1), N_EXPERTS - 1)
    n_used = (pad_end[-1] // bm).astype(jnp.int32).reshape(1)
    return slot, pad_slots, block_e.astype(jnp.int32), n_used, n_blocks


def _dispatch_kernel(slot_ref, pad_ref, h2_ref, x_hbm, sem):
    tm = h2_ref.shape[0] // SUB
    n_pad = pad_ref.shape[0]

    def body(tok, carry):
        for k in range(TOP_K):
            dst = x_hbm.at[_tile_rows(slot_ref[0, 0, tok * TOP_K + k])]
            pltpu.make_async_copy(h2_ref.at[_tile_rows(tok)], dst, sem).start(priority=k % 2)
        return carry
    lax.fori_loop(0, tm, body, 0, unroll=2)

    def drain_tile():
        pltpu.make_async_copy(h2_ref, x_hbm.at[pl.ds(0, tm * SUB)], sem).wait()

    @pl.when(pl.program_id(0) == 0)
    def _():
        def pad_body(g, carry):
            for u in range(DMA_UNROLL):
                dst = x_hbm.at[_tile_rows(pad_ref[g * DMA_UNROLL + u])]
                pltpu.make_async_copy(h2_ref.at[_tile_rows(0)], dst, sem).start(priority=u % 2)
            return carry
        lax.fori_loop(0, n_pad // DMA_UNROLL, pad_body, 0)
        for _ in range(n_pad // tm):
            drain_tile()

    for _ in range(TOP_K):
        drain_tile()


def _dispatch(h2, slot, pad_slots, n_slots):
    tm = ROW_TILE
    t = h2.shape[0] // SUB
    assert pad_slots.shape[0] % tm == 0
    return pl.pallas_call(
        _dispatch_kernel,
        grid=(t // tm,),
        in_specs=[
            pl.BlockSpec((1, 1, tm * TOP_K), lambda i: (i, 0, 0), memory_space=pltpu.SMEM),
            pl.BlockSpec(memory_space=pltpu.SMEM),
            pl.BlockSpec((tm * SUB, LANE), lambda i: (i, 0)),
        ],
        out_specs=pl.BlockSpec(memory_space=pl.ANY),
        out_shape=jax.ShapeDtypeStruct((n_slots * SUB, LANE), F32),
        scratch_shapes=[pltpu.SemaphoreType.DMA(())],
        compiler_params=_params("arbitrary"),
        name="dispatch",
    )(slot.reshape(t // tm, 1, tm * TOP_K), pad_slots, h2)


def _wprep_kernel(w_ref, eo_ref, o_ref):
    half = o_ref.shape[2] // 2
    two = 2 * LANE
    for c in range(w_ref.shape[2] // two):
        blk = w_ref[0, :, c * two:(c + 1) * two].astype(BF16)
        r = jnp.dot(blk, eo_ref[...], preferred_element_type=F32)
        o_ref[0, :, c * LANE:(c + 1) * LANE] = r[:, :LANE].astype(BF16)
        o_ref[0, :, half + c * LANE:half + (c + 1) * LANE] = r[:, LANE:].astype(BF16)


def _wprep(w):
    ne, d, f2 = w.shape
    eo = np.zeros((2 * LANE, 2 * LANE))
    eo[2 * np.arange(LANE), np.arange(LANE)] = 1.0
    eo[2 * np.arange(LANE) + 1, LANE + np.arange(LANE)] = 1.0
    return pl.pallas_call(
        _wprep_kernel,
        grid=(ne,),
        in_specs=[pl.BlockSpec((1, d, f2), lambda e: (e, 0, 0)),
                  pl.BlockSpec((2 * LANE, 2 * LANE), lambda e: (0, 0))],
        out_specs=pl.BlockSpec((1, d, f2), lambda e: (e, 0, 0)),
        out_shape=jax.ShapeDtypeStruct(w.shape, BF16),
        compiler_params=_params("arbitrary"),
        name="expert_weight_layout",
    )(w, jnp.asarray(eo, BF16))


def _tile_rows(r):
    return pl.ds(pl.multiple_of(r * SUB, SUB), SUB)


def _store_token_tiled(ref, val):
    n = val.shape[0]
    for c in range(SUB):
        ref[pl.ds(c, n, stride=SUB), :] = val[:, c * LANE:(c + 1) * LANE]


def _load_token_tiled(ref, n):
    return jnp.concatenate([ref[pl.ds(c, n, stride=SUB), :] for c in range(SUB)], axis=1)


def _ffn_kernel(be_ref, nu_ref, x_ref, wg_ref, wu_ref, wd_ref, bg_ref, bu_ref, bd_ref, y_ref):
    bm = y_ref.shape[0] // SUB
    b = pl.program_id(0)
    n_used = nu_ref[0]

    @pl.when(b < n_used)
    def _():
        x = _load_token_tiled(x_ref, bm).astype(BF16)
        hg = jnp.dot(x, wg_ref[0], preferred_element_type=F32) + bg_ref[0]
        hu = jnp.dot(x, wu_ref[0], preferred_element_type=F32) + bu_ref[0]
        glu = jnp.minimum(hg, SWIGLU_LIMIT)
        lin = jnp.clip(hu, -SWIGLU_LIMIT, SWIGLU_LIMIT)
        act = glu * _sigmoid(SWIGLU_ALPHA * glu) * (lin + 1.0)
        y = jnp.dot(act.astype(BF16), wd_ref[0], preferred_element_type=F32) + bd_ref[0]
        _store_token_tiled(y_ref, y)

    @pl.when(b >= n_used)
    def _():
        y_ref[...] = jnp.zeros_like(y_ref)


def _expert_ffn(x_slots, block_e, n_used, wgu, wd, bg, bu, bd):
    bm = FFN_BM
    d, f = wd.shape[2], wd.shape[1]
    assert d == SUB * LANE
    nb = block_e.shape[0]
    wspec = lambda a: pl.BlockSpec((1,) + a.shape[1:], lambda b, be, nu: (be[b], 0, 0))
    wg_spec = pl.BlockSpec((1, d, f), lambda b, be, nu: (be[b], 0, 0))
    wu_spec = pl.BlockSpec((1, d, f), lambda b, be, nu: (be[b], 0, 1))
    rows = pl.BlockSpec((bm * SUB, LANE), lambda b, be, nu: (b, 0))
    return pl.pallas_call(
        _ffn_kernel,
        grid_spec=pltpu.PrefetchScalarGridSpec(
            num_scalar_prefetch=2,
            grid=(nb,),
            in_specs=[rows, wg_spec, wu_spec, wspec(wd), wspec(bg), wspec(bu), wspec(bd)],
            out_specs=rows,
        ),
        out_shape=jax.ShapeDtypeStruct((nb * bm * SUB, LANE), F32),
        compiler_params=_params("arbitrary"),
        name="expert_ffn",
    )(block_e, n_used, x_slots, wgu, wgu, wd, bg, bu, bd)


def _combine_kernel(npb, sl_ref, sln_ref, gate_ref, x1_ref, gf_ref, y_hbm, yp_ref, ys_ref, ybuf, sem):
    tm = x1_ref.shape[0]
    i = pl.program_id(0)
    n = pl.num_programs(0)
    slot = lax.rem(i, 2)

    def issue(s_ref, s):
        def body(tok, carry):
            for k in range(TOP_K):
                src = y_hbm.at[_tile_rows(s_ref[0, 0, tok * TOP_K + k])]
                pltpu.make_async_copy(src, ybuf.at[s, k, _tile_rows(tok)], sem.at[s]).start(priority=k % 2)
            return carry
        lax.fori_loop(0, tm, body, 0, unroll=2)

    @pl.when(i == 0)
    def _():
        issue(sl_ref, 0)

    @pl.when(i + 1 < n)
    def _():
        issue(sln_ref, 1 - slot)

    for k in range(TOP_K):
        pltpu.make_async_copy(y_hbm.at[pl.ds(0, tm * SUB)], ybuf.at[slot, k], sem.at[slot]).wait()
    y = jnp.zeros(x1_ref.shape, F32)
    gate = gate_ref[...]
    for k in range(TOP_K):
        y = y + gate[:, k:k + 1] * _load_token_tiled(ybuf.at[slot, k], tm)
    out = _rms(x1_ref[...] + y, gf_ref[...])

    @pl.when(i < npb)
    def _():
        yp_ref[...] = out

    @pl.when(i >= npb)
    def _():
        ys_ref[...] = out


def _combine(y_slots, slot, gate, x1, gf, npt):
    t, d = x1.shape
    tm = CMB_TM
    nt = t // tm
    npb = npt // tm
    slot3 = slot.reshape(nt, 1, tm * TOP_K)
    smem_blk = lambda f: pl.BlockSpec((1, 1, tm * TOP_K), f, memory_space=pltpu.SMEM)
    return pl.pallas_call(
        functools.partial(_combine_kernel, npb),
        grid=(nt,),
        in_specs=[
            smem_blk(lambda i: (i, 0, 0)),
            smem_blk(lambda i: (jnp.minimum(i + 1, nt - 1), 0, 0)),
            pl.BlockSpec((tm, TOP_K), lambda i: (i, 0)),
            pl.BlockSpec((tm, d), lambda i: (i, 0)),
            pl.BlockSpec((1, d), lambda i: (0, 0)),
            pl.BlockSpec(memory_space=pl.ANY),
        ],
        out_specs=[
            pl.BlockSpec((tm, d), lambda i: (jnp.minimum(i, npb - 1), 0)),
            pl.BlockSpec((tm, d), lambda i: (jnp.maximum(i - npb, 0), 0)),
        ],
        out_shape=[jax.ShapeDtypeStruct((npt, d), F32), jax.ShapeDtypeStruct((t - npt, d), F32)],
        scratch_shapes=[pltpu.VMEM((2, TOP_K, tm * SUB, LANE), F32), pltpu.SemaphoreType.DMA((2,))],
        compiler_params=_params("arbitrary"),
        name="combine_norm",
    )(slot3, slot3, gate, x1, gf, y_slots)


def _inproj_columns(d):
    da_q, db = HA_Q * HEAD_DIM, HB * HEAD_DIM
    da_kv = HA_KV * HEAD_DIM
    q_b_end = da_q + db
    g_b_end = q_b_end + 2 * d
    ka0 = g_b_end
    va0 = ka0 + da_kv
    kb0 = va0 + da_kv
    vb0 = kb0 + db
    dup = lambda c0: np.concatenate(
        [np.tile(np.arange(c0 + h * HEAD_DIM, c0 + (h + 1) * HEAD_DIM), 2) for h in range(HA_KV)])
    cols = np.concatenate([np.arange(0, g_b_end), dup(ka0), dup(va0), np.arange(kb0, vb0 + db)])
    segs = ((0, da_q), (da_q, db), (q_b_end, 2 * d), (g_b_end, 2 * da_kv),
            (g_b_end + 2 * da_kv, 2 * da_kv), (g_b_end + 4 * da_kv, db), (g_b_end + 4 * da_kv + db, db))
    return cols, segs


def kernel(x_prompt, x_sample, meta_tokens, norm1_g, w_in, attn_sink, rel_pos_bias, w_branch_a,
           w_branch_b, w_out, norm2_g, router_w, router_b, w_gate_up, b_gate_up, w_down, b_down,
           final_norm_g):
    assert norm1_g.shape[0] == 1, "single-layer trunk"
    bp, lp, d = x_prompt.shape
    bs, ls, _ = x_sample.shape
    npt, nst = bp * lp, bs * ls
    assert npt % ROW_TILE == 0 and nst % ROW_TILE == 0
    xp = x_prompt.reshape(npt, d)
    xs = x_sample.reshape(nst, d)

    cols, segs = _inproj_columns(d)
    w_p = w_in[0][:, cols].astype(BF16)
    g1 = norm1_g[0].reshape(1, d)
    qa, qb, gates, ka2, va2, kb, vb = _inproj(xp, xs, g1, w_p, segs)
    kv0 = segs[3][0]
    meta_kv = _meta_proj(meta_tokens, g1, w_p[:, kv0:])
    w_kv = 2 * HA_KV * HEAD_DIM
    meta_kv = _pad_meta(meta_kv)
    km2, vm2 = meta_kv[:, :w_kv], meta_kv[:, w_kv:2 * w_kv]
    kbm, vbm = meta_kv[:, 2 * w_kv:2 * w_kv + HB * HEAD_DIM], meta_kv[:, 2 * w_kv + HB * HEAD_DIM:]

    oa = _window_attention(qa, ka2, va2, km2, vm2, attn_sink[0].astype(F32), npt, lp, ls)
    ob = _nbr_attention(qb, kb, vb, kbm, vbm, rel_pos_bias[0], npt, lp, ls)

    rw = router_w[0].astype(F32)
    rw_top = _bf16_part(rw)
    rw_hi = rw_top.astype(BF16)
    rw_lo = (rw - rw_top).astype(BF16)
    lane_pad = lambda a, fill: jnp.pad(a, ((0, 0), (0, LANE - a.shape[1])), constant_values=fill)
    rw2 = jnp.concatenate([lane_pad(rw_hi, 0), lane_pad(rw_lo, 0)], axis=1)
    rb = lane_pad(router_b[0].reshape(1, -1).astype(F32), -jnp.inf)
    x1, h2, top_idx, gate, rank, counts = _merge(
        oa, ob, gates, xp, xs, w_branch_a[0].astype(BF16), w_branch_b[0].astype(BF16),
        w_out[0].astype(BF16), norm2_g[0].reshape(1, d), rw2, rb)

    slot, pad_slots, block_e, n_used, n_blocks = _route(top_idx, rank, counts, FFN_BM)
    x_slots = _dispatch(h2, slot, pad_slots, n_blocks * FFN_BM)
    bgu = b_gate_up[0]
    y_slots = _expert_ffn(
        x_slots, block_e, n_used, _wprep(w_gate_up[0]), w_down[0].astype(BF16),
        bgu[:, None, 0::2], bgu[:, None, 1::2], b_down[0][:, None, :])

    yp, ys = _combine(y_slots, slot, gate, x1, final_norm_g.reshape(1, d), npt)
    return yp.reshape(bp, lp, d), ys.reshape(bs, ls, d)
```

```python
import functools

import numpy as np
import jax
import jax.numpy as jnp
from jax import lax
from jax.experimental import pallas as pl
from jax.experimental.pallas import tpu as pltpu

F32 = jnp.float32
BF16 = jnp.bfloat16

N_META = 16
GRID_W = 64
HA_Q = 8
HA_KV = 2
HEAD_DIM = 64
WINDOW = 128
HB = 8
NB_ROWS = 8
NB_COLS = 16
N_EXPERTS = 32
TOP_K = 4
SWIGLU_ALPHA = 1.702
SWIGLU_LIMIT = 7.0
NORM_EPS = 1e-5
NEG_INF = -1e30
SCALE = HEAD_DIM ** -0.5

LANE = 128
SUB = 8
ROW_TILE = 512
WIN_TQ = 256
NBR_ROWS = 4
FFN_BM = 512
DMA_UNROLL = 8
DISPATCH_TILE = ROW_TILE // 2
RUN = 8
MAX_RUNS = DISPATCH_TILE * TOP_K // RUN + N_EXPERTS
CMB_TM = 128
VMEM_LIMIT = 56 * 1024 * 1024


def _rms(x, g):
    return x * lax.rsqrt(jnp.mean(x * x, axis=-1, keepdims=True) + NORM_EPS) * g


def _bf16_part(x):
    bits = lax.bitcast_convert_type(x, jnp.uint32) & jnp.uint32(0xFFFF0000)
    return lax.bitcast_convert_type(bits, F32)


def _sigmoid(z):
    return 1.0 / (1.0 + jnp.exp(-z))


def _params(*sem):
    return pltpu.CompilerParams(dimension_semantics=sem, vmem_limit_bytes=VMEM_LIMIT)


def _inproj_kernel(npb, segs, xp_ref, xs_ref, g_ref, w_ref, *o_refs):
    i = pl.program_id(0)
    x = jnp.where(i < npb, xp_ref[...], xs_ref[...])
    h = _rms(x, g_ref[...]).astype(BF16)
    for (c0, width), o_ref in zip(segs, o_refs):
        for off in range(0, width, 512):
            n = min(512, width - off)
            o_ref[:, off:off + n] = jnp.dot(
                h, w_ref[:, c0 + off:c0 + off + n], preferred_element_type=F32).astype(BF16)


def _inproj(xp, xs, g, w, segs):
    npt, d = xp.shape
    t = npt + xs.shape[0]
    tm = ROW_TILE
    npb = npt // tm
    return pl.pallas_call(
        functools.partial(_inproj_kernel, npb, segs),
        grid=(t // tm,),
        in_specs=[
            pl.BlockSpec((tm, d), lambda i: (jnp.minimum(i, npb - 1), 0)),
            pl.BlockSpec((tm, d), lambda i: (jnp.maximum(i - npb, 0), 0)),
            pl.BlockSpec((1, d), lambda i: (0, 0)),
            pl.BlockSpec(w.shape, lambda i: (0, 0)),
        ],
        out_specs=[pl.BlockSpec((tm, width), lambda i: (i, 0)) for _, width in segs],
        out_shape=[jax.ShapeDtypeStruct((t, width), BF16) for _, width in segs],
        compiler_params=_params("arbitrary"),
        name="inproj",
    )(xp, xs, g, w)


def _meta_kernel(m_ref, g_ref, w_ref, o_ref):
    h = _rms(m_ref[...], g_ref[...]).astype(BF16)
    o_ref[...] = jnp.dot(h, w_ref[...], preferred_element_type=F32).astype(BF16)


def _meta_proj(meta, g, w):
    return pl.pallas_call(
        _meta_kernel,
        out_shape=jax.ShapeDtypeStruct((meta.shape[0], w.shape[1]), BF16),
        name="meta_proj",
    )(meta, g, w)


def _window_bias(tq):
    slopes = 2.0 ** (-8.0 * np.arange(1, HA_Q + 1) / HA_Q)
    q = np.arange(tq)[:, None]
    col = np.arange(tq + 2 * WINDOW)[None, :]
    dist = np.abs(col - WINDOW - q)
    band = np.where((dist <= WINDOW)[None], -slopes[:, None, None] * dist[None], NEG_INF)
    full = np.concatenate([band, np.broadcast_to(_meta_cols(), (HA_Q, tq, LANE))], axis=-1)
    return jnp.asarray(full.reshape(HA_Q // 2, 2 * tq, -1), F32)


def _meta_cols():
    return np.where(np.arange(LANE) < N_META, 0.0, NEG_INF)


def _pad_meta(a):
    return jnp.pad(a, ((0, LANE - a.shape[0]), (0, 0)))


def _seq_pos(start, npt, lp, ls):
    is_p = start < npt
    pos = jnp.where(is_p, lax.rem(start, lp), lax.rem(jnp.maximum(start - npt, 0), ls))
    return pos, jnp.where(is_p, lp, ls)


def _window_kernel(npt, lp, ls, sink_ref, qa_ref, kp_ref, kc_ref, kn_ref, vp_ref, vc_ref, vn_ref,
                   km_ref, vm_ref, bias_ref, o_ref):
    tq = qa_ref.shape[0]
    nk = tq + 2 * WINDOW
    pos, seq_len = _seq_pos(pl.program_id(0) * tq, npt, lp, ls)
    pen_prev = jnp.where(pos == 0, NEG_INF, 0.0).astype(F32)
    pen_next = jnp.where(pos + tq == seq_len, NEG_INF, 0.0).astype(F32)
    col = lax.broadcasted_iota(jnp.int32, (1, nk + LANE), 1)
    pen = (jnp.where(col < WINDOW, pen_prev, 0.0)
           + jnp.where((col >= WINDOW + tq) & (col < nk), pen_next, 0.0))
    lo = lax.broadcasted_iota(jnp.int32, (1, LANE), 1) < HEAD_DIM
    top = lax.broadcasted_iota(jnp.int32, (2 * tq, 1), 0) < tq
    for i in range(HA_Q // 2):
        hk = (2 * i) // (HA_Q // HA_KV)
        sl = slice(i * LANE, (i + 1) * LANE)
        ksl = slice(hk * LANE, (hk + 1) * LANE)
        q = qa_ref[:, sl] * jnp.asarray(SCALE, BF16)
        zero = jnp.zeros_like(q)
        qq = jnp.concatenate([jnp.where(lo, q, zero), jnp.where(lo, zero, q)], axis=0)
        k_all = jnp.concatenate([kp_ref[:, ksl], kc_ref[:, ksl], kn_ref[:, ksl], km_ref[:, ksl]], axis=0)
        v_all = jnp.concatenate([vp_ref[:, ksl], vc_ref[:, ksl], vn_ref[:, ksl], vm_ref[:, ksl]], axis=0)
        s = lax.dot_general(qq, k_all, (((1,), (1,)), ((), ())), preferred_element_type=F32)
        s = s + bias_ref[i] + pen
        sink = jnp.where(top, sink_ref[2 * i], sink_ref[2 * i + 1])
        m = jnp.maximum(jnp.max(s, axis=-1, keepdims=True), sink)
        p = jnp.exp(s - m)
        denom = jnp.sum(p, axis=-1, keepdims=True) + jnp.exp(sink - m)
        o = jnp.dot(p.astype(BF16), v_all, preferred_element_type=F32) / denom
        o_ref[:, sl] = jnp.where(lo, o[:tq], o[tq:]).astype(BF16)


def _window_attention(qa, ka2, va2, km2, vm2, sink, npt, lp, ls):
    t = qa.shape[0]
    tq = WIN_TQ
    assert lp % tq == 0 and ls % tq == 0 and tq == 2 * WINDOW
    nhalf = t // WINDOW
    bias = _window_bias(tq)
    kw = ka2.shape[1]
    prev = pl.BlockSpec((WINDOW, kw), lambda j: (jnp.maximum(2 * j - 1, 0), 0))
    cur = pl.BlockSpec((tq, kw), lambda j: (j, 0))
    nxt = pl.BlockSpec((WINDOW, kw), lambda j: (jnp.minimum(2 * j + 2, nhalf - 1), 0))
    whole = lambda a: pl.BlockSpec(a.shape, lambda j: (0,) * a.ndim)
    return pl.pallas_call(
        functools.partial(_window_kernel, npt, lp, ls),
        grid=(t // tq,),
        in_specs=[
            pl.BlockSpec(memory_space=pltpu.SMEM),
            pl.BlockSpec((tq, qa.shape[1]), lambda j: (j, 0)),
            prev, cur, nxt, prev, cur, nxt,
            whole(km2), whole(vm2), whole(bias),
        ],
        out_specs=pl.BlockSpec((tq, qa.shape[1]), lambda j: (j, 0)),
        out_shape=jax.ShapeDtypeStruct(qa.shape, BF16),
        compiler_params=_params("arbitrary"),
        name="window_attn",
    )(sink, qa, ka2, ka2, ka2, va2, va2, va2, km2, vm2, bias)


def _nbr_tables(rpb):
    r, w = NBR_ROWS, GRID_W
    qc = np.arange(w)[:, None]
    kcol = np.arange(w)[None, :]
    col_start = np.clip(qc - NB_COLS // 2, 0, w - NB_COLS)
    col_ok = (kcol >= col_start) & (kcol < col_start + NB_COLS)
    sel = np.zeros((2 * NB_COLS - 1, w, w), np.float32)
    ci, ki = np.nonzero(col_ok)
    sel[ki - ci + NB_COLS - 1, ci, ki] = 1.0
    per_row = jnp.einsum("hrj,jck->hrck", rpb.astype(F32), jnp.asarray(sel), precision=lax.Precision.HIGHEST)
    per_row = jnp.where(jnp.asarray(col_ok), per_row, NEG_INF)
    off = NB_ROWS - 1 - NBR_ROWS
    vals = jnp.concatenate(
        [jnp.concatenate([per_row[:, b - a + off] for b in range(3 * r)], axis=-1) for a in range(r)], axis=1)
    meta = jnp.broadcast_to(jnp.asarray(_meta_cols(), F32), (HB, r * w, LANE))
    bias = jnp.concatenate([vals, meta], axis=-1)
    a = np.repeat(np.arange(r), w)[:, None]
    b = np.repeat(np.arange(3 * r), w)[None, :]
    first = NBR_ROWS
    ok = np.stack([
        (b >= first) & (b < first + NB_ROWS) & (a >= 0),
        (b >= a) & (b < a + NB_ROWS),
        (b >= 0) & (b < NB_ROWS) & (a >= 0),
    ])
    ok = np.concatenate([ok, np.ones((3, r * w, LANE), bool)], axis=-1)
    return bias, jnp.asarray(np.where(ok, 0.0, NEG_INF), F32)


def _nbr_kernel(npg, gp, gs, qb_ref, kp_ref, kc_ref, kn_ref, vp_ref, vc_ref, vn_ref, km_ref, vm_ref,
                bias_ref, mask_ref, o_ref):
    tq = qb_ref.shape[0]
    g = pl.program_id(0)
    is_p = g < npg
    per_seq = jnp.where(is_p, gp, gs)
    r = jnp.where(is_p, lax.rem(g, gp), lax.rem(jnp.maximum(g - npg, 0), gs))
    pat = jnp.where(r == 0, 0, jnp.where(r == per_seq - 1, 2, 1))
    row_mask = mask_ref[pat]
    lo = lax.broadcasted_iota(jnp.int32, (1, LANE), 1) < HEAD_DIM
    for i in range(HB // 2):
        sl = slice(i * LANE, (i + 1) * LANE)
        q = qb_ref[:, sl] * jnp.asarray(SCALE, BF16)
        zero = jnp.zeros_like(q)
        qq = jnp.concatenate([jnp.where(lo, q, zero), jnp.where(lo, zero, q)], axis=0)
        k_all = jnp.concatenate([kp_ref[:, sl], kc_ref[:, sl], kn_ref[:, sl], km_ref[:, sl]], axis=0)
        v_all = jnp.concatenate([vp_ref[:, sl], vc_ref[:, sl], vn_ref[:, sl], vm_ref[:, sl]], axis=0)
        s = lax.dot_general(qq, k_all, (((1,), (1,)), ((), ())), preferred_element_type=F32)
        halves = []
        for hf in range(2):
            sh = s[hf * tq:(hf + 1) * tq] + bias_ref[2 * i + hf] + row_mask
            m = jnp.max(sh, axis=-1, keepdims=True)
            p = jnp.exp(sh - m)
            denom = jnp.sum(p, axis=-1, keepdims=True)
            halves.append(jnp.dot(p.astype(BF16), v_all, preferred_element_type=F32) / denom)
        o_ref[:, sl] = jnp.where(lo, halves[0], halves[1]).astype(BF16)


def _nbr_attention(qb, kb, vb, kbm, vbm, rpb, npt, lp, ls):
    t, width = qb.shape
    tq = NBR_ROWS * GRID_W
    rows_p, rows_s = lp // GRID_W, ls // GRID_W
    assert rows_p % NBR_ROWS == 0 and rows_s % NBR_ROWS == 0 and min(rows_p, rows_s) >= NB_ROWS
    assert NB_ROWS == 2 * NBR_ROWS
    ng = t // tq
    bias, mask = _nbr_tables(rpb)
    prev = pl.BlockSpec((tq, width), lambda g: (jnp.maximum(g - 1, 0), 0))
    cur = pl.BlockSpec((tq, width), lambda g: (g, 0))
    nxt = pl.BlockSpec((tq, width), lambda g: (jnp.minimum(g + 1, ng - 1), 0))
    whole = lambda a: pl.BlockSpec(a.shape, lambda g: (0,) * a.ndim)
    return pl.pallas_call(
        functools.partial(_nbr_kernel, npt // tq, rows_p // NBR_ROWS, rows_s // NBR_ROWS),
        grid=(ng,),
        in_specs=[cur, prev, cur, nxt, prev, cur, nxt, whole(kbm), whole(vbm), whole(bias), whole(mask)],
        out_specs=cur,
        out_shape=jax.ShapeDtypeStruct(qb.shape, BF16),
        compiler_params=_params("arbitrary"),
        name="nbr_attn",
    )(qb, kb, kb, kb, vb, vb, vb, kbm, vbm, bias, mask)


def _merge_kernel(npb, oa_ref, ob_ref, g_ref, xp_ref, xs_ref, wa_ref, wb_ref, wo_ref, g2_ref,
                  rw_ref, rb_ref, tri_ref, x1_ref, h2_ref, idx_ref, gate_ref, rank_ref, cnt_ref, cs_ref,
                  carry):
    i = pl.program_id(0)
    d = x1_ref.shape[1]
    x = jnp.where(i < npb, xp_ref[...], xs_ref[...])
    a = jnp.dot(oa_ref[...], wa_ref[...], preferred_element_type=F32)
    b = jnp.dot(ob_ref[...], wb_ref[...], preferred_element_type=F32)
    mix = _sigmoid(g_ref[:, :d].astype(F32)) * a + _sigmoid(g_ref[:, d:].astype(F32)) * b
    x1 = x + jnp.dot(mix.astype(BF16), wo_ref[...], preferred_element_type=F32)
    x1_ref[...] = x1
    h2 = _rms(x1, g2_ref[...])
    h2_ref[...] = h2.astype(BF16)
    h_top = _bf16_part(h2)
    h_hi = h_top.astype(BF16)
    h_lo = (h2 - h_top).astype(BF16)
    r_hi = jnp.dot(h_hi, rw_ref[...], preferred_element_type=F32)
    r_lo = jnp.dot(h_lo, rw_ref[:, :LANE], preferred_element_type=F32)
    logits = r_hi[:, :LANE] + r_hi[:, LANE:] + r_lo + rb_ref[...]

    lane = lax.broadcasted_iota(jnp.int32, logits.shape, 1).astype(F32)
    work = logits
    member = jnp.zeros(logits.shape, F32)
    vals, idxs = [], []
    for _ in range(TOP_K):
        m = jnp.max(work, axis=-1, keepdims=True)
        idx = jnp.min(jnp.where(work == m, lane, float(LANE)), axis=-1, keepdims=True)
        hit = lane == idx
        work = jnp.where(hit, -jnp.inf, work)
        member = jnp.where(hit, 1.0, member)
        vals.append(m)
        idxs.append(idx)
    e = [jnp.exp(v - vals[0]) for v in vals]
    denom = functools.reduce(lambda p, q: p + q, e)
    gate_ref[...] = jnp.concatenate(e, axis=1) / denom
    idx_ref[...] = jnp.concatenate(idxs, axis=1).astype(jnp.int32)

    @pl.when(i == 0)
    def _():
        carry[...] = jnp.zeros_like(carry)

    before = jnp.dot(tri_ref[...], member.astype(BF16), preferred_element_type=F32) + carry[...]
    ranks = [jnp.sum(jnp.where(lane == idx, before, 0.0), axis=-1, keepdims=True) for idx in idxs]
    rank_ref[...] = jnp.concatenate(ranks, axis=1).astype(jnp.int32)
    half = member.shape[0] // 2
    c0 = carry[...]
    c1 = c0 + jnp.sum(member[:half], axis=0, keepdims=True)
    cs_ref[0] = jnp.concatenate([c0, c1], axis=0).astype(jnp.int32)
    carry[...] = c1 + jnp.sum(member[half:], axis=0, keepdims=True)
    cnt_ref[...] = carry[...].astype(jnp.int32)


def _merge(oa, ob, gates, xp, xs, wa, wb, wo, g2, rw, rb):
    npt, d = xp.shape
    t = npt + xs.shape[0]
    tm = ROW_TILE
    npb = npt // tm
    tri = jnp.asarray(np.tril(np.ones((tm, tm)), -1), BF16)
    row = lambda width: pl.BlockSpec((tm, width), lambda i: (i, 0))
    whole = lambda a: pl.BlockSpec(a.shape, lambda i: (0,) * a.ndim)
    return pl.pallas_call(
        functools.partial(_merge_kernel, npb),
        grid=(t // tm,),
        in_specs=[
            row(oa.shape[1]), row(ob.shape[1]), row(gates.shape[1]),
            pl.BlockSpec((tm, d), lambda i: (jnp.minimum(i, npb - 1), 0)),
            pl.BlockSpec((tm, d), lambda i: (jnp.maximum(i - npb, 0), 0)),
            whole(wa), whole(wb), whole(wo), whole(g2), whole(rw), whole(rb), whole(tri),
        ],
        out_specs=[row(d), row(d), row(TOP_K), row(TOP_K), row(TOP_K),
                   pl.BlockSpec((1, LANE), lambda i: (0, 0)),
                   pl.BlockSpec((1, 2, LANE), lambda i: (i, 0, 0))],
        out_shape=[jax.ShapeDtypeStruct((t, d), F32), jax.ShapeDtypeStruct((t, d), BF16),
                   jax.ShapeDtypeStruct((t, TOP_K), jnp.int32), jax.ShapeDtypeStruct((t, TOP_K), F32),
                   jax.ShapeDtypeStruct((t, TOP_K), jnp.int32),
                   jax.ShapeDtypeStruct((1, LANE), jnp.int32),
                   jax.ShapeDtypeStruct((t // tm, 2, LANE), jnp.int32)],
        scratch_shapes=[pltpu.VMEM((1, LANE), F32)],
        compiler_params=_params("arbitrary"),
        name="merge_router",
    )(oa, ob, gates, xp, xs, wa, wb, wo, g2, rw, rb, tri)


def _route(top_idx, rank, counts, cstart, bm):
    t = top_idx.shape[0]
    ne = N_EXPERTS
    dt = DISPATCH_TILE
    n_d = t // dt
    counts = counts.reshape(-1)[:ne]
    padded = (counts + RUN - 1 + bm - 1) // bm * bm
    pad_end = jnp.cumsum(padded)
    pad_start = pad_end - padded
    n_blocks = -(-(t * TOP_K + ne * (RUN - 1)) // bm) + ne
    experts = jnp.arange(ne, dtype=jnp.int32)

    cs = cstart.reshape(n_d, LANE)[:, :ne]
    cnt = jnp.concatenate([cs[1:], counts[None]], axis=0) - cs
    c_run = (cnt + RUN - 1) // RUN * RUN
    off = jnp.cumsum(c_run, axis=1) - c_run
    onehot = (top_idx[..., None] == experts).reshape(n_d, dt, TOP_K, ne)
    per_tok = lambda tab: jnp.sum(jnp.where(onehot, tab[:, None, None, :], 0), axis=-1)
    rank_t = rank.reshape(n_d, dt, TOP_K)
    pos = (per_tok(off) + rank_t - per_tok(cs)).transpose(0, 2, 1).astype(jnp.int32)
    dst0 = pad_start[None, :] + cs
    slot = (rank_t + per_tok(jnp.broadcast_to(pad_start[None, :], cs.shape))).reshape(-1).astype(jnp.int32)

    n_run = c_run // RUN
    cum = jnp.cumsum(n_run, axis=1)
    i = jnp.arange(MAX_RUNS, dtype=jnp.int32)
    e_of = jnp.minimum(jnp.sum(i[None, :, None] >= cum[:, None, :], axis=-1), ne - 1)
    pick = lambda tab: jnp.sum(jnp.where(e_of[..., None] == experts, tab[:, None, :], 0), axis=-1)
    local = RUN * (i[None, :] - pick(cum - n_run))
    runs = jnp.concatenate([pick(off) + local, pick(dst0) + local, cum[:, -1:]], axis=1).astype(jnp.int32)

    j = jnp.arange(bm, dtype=jnp.int32)[None, :]
    fill = (pad_start + counts)[:, None] + j
    spare = n_blocks * bm + experts[:, None] * bm + j
    pad_slots = jnp.where(fill < pad_end[:, None], fill, spare).reshape(-1).astype(jnp.int32)
    starts = jnp.arange(n_blocks, dtype=jnp.int32) * bm
    block_e = jnp.minimum(jnp.sum(starts[:, None] >= pad_end[None, :], axis=-1), ne - 1)
    n_used = (pad_end[-1] // bm).astype(jnp.int32).reshape(1)
    return slot, pos, runs.reshape(n_d, 1, -1), pad_slots, block_e.astype(jnp.int32), n_used, n_blocks


def _dispatch_kernel(runs_ref, pad_ref, pos_ref, h2_ref, x_hbm, sbuf, sem):
    sb = sbuf.shape[0] // SUB
    dt = h2_ref.shape[0]
    rows = lax.broadcasted_iota(jnp.int32, (sb, dt), 0)
    hit = rows == pos_ref[0, 0:1, :]
    for k in range(1, TOP_K):
        hit = hit | (rows == pos_ref[0, k:k + 1, :])
    onehot = jnp.where(hit, 1.0, 0.0).astype(BF16)
    _store_token_tiled(sbuf, jnp.dot(onehot, h2_ref[...], preferred_element_type=F32))

    def run_copy(r):
        src = sbuf.at[pl.ds(pl.multiple_of(runs_ref[0, 0, r] * SUB, SUB), RUN * SUB)]
        dst = x_hbm.at[pl.ds(pl.multiple_of(runs_ref[0, 0, MAX_RUNS + r] * SUB, SUB), RUN * SUB)]
        return pltpu.make_async_copy(src, dst, sem)

    n_runs = runs_ref[0, 0, 2 * MAX_RUNS]

    def start(r, carry):
        run_copy(r).start()
        return carry
    lax.fori_loop(0, n_runs, start, 0)

    @pl.when(pl.program_id(0) == 0)
    def _():
        n_pad = pad_ref.shape[0]

        def pad_copy(j):
            return pltpu.make_async_copy(sbuf.at[_tile_rows(0)], x_hbm.at[_tile_rows(pad_ref[j])], sem)

        def pad_start(g, carry):
            for u in range(DMA_UNROLL):
                pad_copy(g * DMA_UNROLL + u).start(priority=u % 2)
            return carry
        lax.fori_loop(0, n_pad // DMA_UNROLL, pad_start, 0)

        def pad_wait(j, carry):
            pad_copy(j).wait()
            return carry
        lax.fori_loop(0, n_pad, pad_wait, 0)

    def wait(r, carry):
        run_copy(r).wait()
        return carry
    lax.fori_loop(0, n_runs, wait, 0)


def _dispatch(h2, pos, runs, pad_slots, n_slots):
    t, d = h2.shape
    dt = DISPATCH_TILE
    sb = dt * TOP_K + N_EXPERTS * RUN
    return pl.pallas_call(
        _dispatch_kernel,
        grid=(t // dt,),
        in_specs=[
            pl.BlockSpec((1, 1, runs.shape[2]), lambda i: (i, 0, 0), memory_space=pltpu.SMEM),
            pl.BlockSpec(memory_space=pltpu.SMEM),
            pl.BlockSpec((1, TOP_K, dt), lambda i: (i, 0, 0)),
            pl.BlockSpec((dt, d), lambda i: (i, 0)),
        ],
        out_specs=pl.BlockSpec(memory_space=pl.ANY),
        out_shape=jax.ShapeDtypeStruct((n_slots * SUB, LANE), F32),
        scratch_shapes=[pltpu.VMEM((sb * SUB, LANE), F32), pltpu.SemaphoreType.DMA(())],
        compiler_params=_params("arbitrary"),
        name="dispatch",
    )(runs, pad_slots, pos, h2)


def _wprep_kernel(w_ref, eo_ref, o_ref):
    half = o_ref.shape[2] // 2
    two = 2 * LANE
    for c in range(w_ref.shape[2] // two):
        blk = w_ref[0, :, c * two:(c + 1) * two].astype(BF16)
        r = jnp.dot(blk, eo_ref[...], preferred_element_type=F32)
        o_ref[0, :, c * LANE:(c + 1) * LANE] = r[:, :LANE].astype(BF16)
        o_ref[0, :, half + c * LANE:half + (c + 1) * LANE] = r[:, LANE:].astype(BF16)


def _wprep(w):
    ne, d, f2 = w.shape
    eo = np.zeros((2 * LANE, 2 * LANE))
    eo[2 * np.arange(LANE), np.arange(LANE)] = 1.0
    eo[2 * np.arange(LANE) + 1, LANE + np.arange(LANE)] = 1.0
    return pl.pallas_call(
        _wprep_kernel,
        grid=(ne,),
        in_specs=[pl.BlockSpec((1, d, f2), lambda e: (e, 0, 0)),
                  pl.BlockSpec((2 * LANE, 2 * LANE), lambda e: (0, 0))],
        out_specs=pl.BlockSpec((1, d, f2), lambda e: (e, 0, 0)),
        out_shape=jax.ShapeDtypeStruct(w.shape, BF16),
        compiler_params=_params("arbitrary"),
        name="expert_weight_layout",
    )(w, jnp.asarray(eo, BF16))


def _tile_rows(r):
    return pl.ds(pl.multiple_of(r * SUB, SUB), SUB)


def _store_token_tiled(ref, val):
    n = val.shape[0]
    for c in range(SUB):
        ref[pl.ds(c, n, stride=SUB), :] = val[:, c * LANE:(c + 1) * LANE]


def _load_token_tiled(ref, n):
    return jnp.concatenate([ref[pl.ds(c, n, stride=SUB), :] for c in range(SUB)], axis=1)


def _ffn_kernel(be_ref, nu_ref, x_ref, wg_ref, wu_ref, wd_ref, bg_ref, bu_ref, bd_ref, y_ref):
    bm = y_ref.shape[0] // SUB
    b = pl.program_id(0)
    n_used = nu_ref[0]

    @pl.when(b < n_used)
    def _():
        x = _load_token_tiled(x_ref, bm).astype(BF16)
        hg = jnp.dot(x, wg_ref[0], preferred_element_type=F32) + bg_ref[0]
        hu = jnp.dot(x, wu_ref[0], preferred_element_type=F32) + bu_ref[0]
        glu = jnp.minimum(hg, SWIGLU_LIMIT)
        lin = jnp.clip(hu, -SWIGLU_LIMIT, SWIGLU_LIMIT)
        act = glu * _sigmoid(SWIGLU_ALPHA * glu) * (lin + 1.0)
        y = jnp.dot(act.astype(BF16), wd_ref[0], preferred_element_type=F32) + bd_ref[0]
        _store_token_tiled(y_ref, y)

    @pl.when(b >= n_used)
    def _():
        y_ref[...] = jnp.zeros_like(y_ref)


def _expert_ffn(x_slots, block_e, n_used, wgu, wd, bg, bu, bd):
    bm = FFN_BM
    d, f = wd.shape[2], wd.shape[1]
    assert d == SUB * LANE
    nb = block_e.shape[0]
    wspec = lambda a: pl.BlockSpec((1,) + a.shape[1:], lambda b, be, nu: (be[b], 0, 0))
    wg_spec = pl.BlockSpec((1, d, f), lambda b, be, nu: (be[b], 0, 0))
    wu_spec = pl.BlockSpec((1, d, f), lambda b, be, nu: (be[b], 0, 1))
    rows = pl.BlockSpec((bm * SUB, LANE), lambda b, be, nu: (b, 0))
    return pl.pallas_call(
        _ffn_kernel,
        grid_spec=pltpu.PrefetchScalarGridSpec(
            num_scalar_prefetch=2,
            grid=(nb,),
            in_specs=[rows, wg_spec, wu_spec, wspec(wd), wspec(bg), wspec(bu), wspec(bd)],
            out_specs=rows,
        ),
        out_shape=jax.ShapeDtypeStruct((nb * bm * SUB, LANE), F32),
        compiler_params=_params("arbitrary"),
        name="expert_ffn",
    )(block_e, n_used, x_slots, wgu, wgu, wd, bg, bu, bd)


def _combine_kernel(npb, sl_ref, sln_ref, gate_ref, x1_ref, gf_ref, y_hbm, yp_ref, ys_ref, ybuf, sem):
    tm = x1_ref.shape[0]
    i = pl.program_id(0)
    n = pl.num_programs(0)
    slot = lax.rem(i, 2)

    def issue(s_ref, s):
        def body(tok, carry):
            for k in range(TOP_K):
                src = y_hbm.at[_tile_rows(s_ref[0, 0, tok * TOP_K + k])]
                pltpu.make_async_copy(src, ybuf.at[s, k, _tile_rows(tok)], sem.at[s]).start(priority=k % 2)
            return carry
        lax.fori_loop(0, tm, body, 0, unroll=2)

    @pl.when(i == 0)
    def _():
        issue(sl_ref, 0)

    @pl.when(i + 1 < n)
    def _():
        issue(sln_ref, 1 - slot)

    for k in range(TOP_K):
        pltpu.make_async_copy(y_hbm.at[pl.ds(0, tm * SUB)], ybuf.at[slot, k], sem.at[slot]).wait()
    y = jnp.zeros(x1_ref.shape, F32)
    gate = gate_ref[...]
    for k in range(TOP_K):
        y = y + gate[:, k:k + 1] * _load_token_tiled(ybuf.at[slot, k], tm)
    out = _rms(x1_ref[...] + y, gf_ref[...])

    @pl.when(i < npb)
    def _():
        yp_ref[...] = out

    @pl.when(i >= npb)
    def _():
        ys_ref[...] = out


def _combine(y_slots, slot, gate, x1, gf, npt):
    t, d = x1.shape
    tm = CMB_TM
    nt = t // tm
    npb = npt // tm
    slot3 = slot.reshape(nt, 1, tm * TOP_K)
    smem_blk = lambda f: pl.BlockSpec((1, 1, tm * TOP_K), f, memory_space=pltpu.SMEM)
    return pl.pallas_call(
        functools.partial(_combine_kernel, npb),
        grid=(nt,),
        in_specs=[
            smem_blk(lambda i: (i, 0, 0)),
            smem_blk(lambda i: (jnp.minimum(i + 1, nt - 1), 0, 0)),
            pl.BlockSpec((tm, TOP_K), lambda i: (i, 0)),
            pl.BlockSpec((tm, d), lambda i: (i, 0)),
            pl.BlockSpec((1, d), lambda i: (0, 0)),
            pl.BlockSpec(memory_space=pl.ANY),
        ],
        out_specs=[
            pl.BlockSpec((tm, d), lambda i: (jnp.minimum(i, npb - 1), 0)),
            pl.BlockSpec((tm, d), lambda i: (jnp.maximum(i - npb, 0), 0)),
        ],
        out_shape=[jax.ShapeDtypeStruct((npt, d), F32), jax.ShapeDtypeStruct((t - npt, d), F32)],
        scratch_shapes=[pltpu.VMEM((2, TOP_K, tm * SUB, LANE), F32), pltpu.SemaphoreType.DMA((2,))],
        compiler_params=_params("arbitrary"),
        name="combine_norm",
    )(slot3, slot3, gate, x1, gf, y_slots)


def _inproj_columns(d):
    da_q, db = HA_Q * HEAD_DIM, HB * HEAD_DIM
    da_kv = HA_KV * HEAD_DIM
    q_b_end = da_q + db
    g_b_end = q_b_end + 2 * d
    ka0 = g_b_end
    va0 = ka0 + da_kv
    kb0 = va0 + da_kv
    vb0 = kb0 + db
    dup = lambda c0: np.concatenate(
        [np.tile(np.arange(c0 + h * HEAD_DIM, c0 + (h + 1) * HEAD_DIM), 2) for h in range(HA_KV)])
    cols = np.concatenate([np.arange(0, g_b_end), dup(ka0), dup(va0), np.arange(kb0, vb0 + db)])
    segs = ((0, da_q), (da_q, db), (q_b_end, 2 * d), (g_b_end, 2 * da_kv),
            (g_b_end + 2 * da_kv, 2 * da_kv), (g_b_end + 4 * da_kv, db), (g_b_end + 4 * da_kv + db, db))
    return cols, segs


def kernel(x_prompt, x_sample, meta_tokens, norm1_g, w_in, attn_sink, rel_pos_bias, w_branch_a,
           w_branch_b, w_out, norm2_g, router_w, router_b, w_gate_up, b_gate_up, w_down, b_down,
           final_norm_g):
    assert norm1_g.shape[0] == 1, "single-layer trunk"
    bp, lp, d = x_prompt.shape
    bs, ls, _ = x_sample.shape
    npt, nst = bp * lp, bs * ls
    assert npt % ROW_TILE == 0 and nst % ROW_TILE == 0
    xp = x_prompt.reshape(npt, d)
    xs = x_sample.reshape(nst, d)

    cols, segs = _inproj_columns(d)
    w_p = w_in[0][:, cols].astype(BF16)
    g1 = norm1_g[0].reshape(1, d)
    qa, qb, gates, ka2, va2, kb, vb = _inproj(xp, xs, g1, w_p, segs)
    kv0 = segs[3][0]
    meta_kv = _meta_proj(meta_tokens, g1, w_p[:, kv0:])
    w_kv = 2 * HA_KV * HEAD_DIM
    meta_kv = _pad_meta(meta_kv)
    km2, vm2 = meta_kv[:, :w_kv], meta_kv[:, w_kv:2 * w_kv]
    kbm, vbm = meta_kv[:, 2 * w_kv:2 * w_kv + HB * HEAD_DIM], meta_kv[:, 2 * w_kv + HB * HEAD_DIM:]

    oa = _window_attention(qa, ka2, va2, km2, vm2, attn_sink[0].astype(F32), npt, lp, ls)
    ob = _nbr_attention(qb, kb, vb, kbm, vbm, rel_pos_bias[0], npt, lp, ls)

    rw = router_w[0].astype(F32)
    rw_top = _bf16_part(rw)
    rw_hi = rw_top.astype(BF16)
    rw_lo = (rw - rw_top).astype(BF16)
    lane_pad = lambda a, fill: jnp.pad(a, ((0, 0), (0, LANE - a.shape[1])), constant_values=fill)
    rw2 = jnp.concatenate([lane_pad(rw_hi, 0), lane_pad(rw_lo, 0)], axis=1)
    rb = lane_pad(router_b[0].reshape(1, -1).astype(F32), -jnp.inf)
    x1, h2, top_idx, gate, rank, counts, cstart = _merge(
        oa, ob, gates, xp, xs, w_branch_a[0].astype(BF16), w_branch_b[0].astype(BF16),
        w_out[0].astype(BF16), norm2_g[0].reshape(1, d), rw2, rb)

    slot, pos, runs, pad_slots, block_e, n_used, n_blocks = _route(top_idx, rank, counts, cstart, FFN_BM)
    x_slots = _dispatch(h2, pos, runs, pad_slots, (n_blocks + N_EXPERTS) * FFN_BM)
    bgu = b_gate_up[0]
    y_slots = _expert_ffn(
        x_slots, block_e, n_used, _wprep(w_gate_up[0]), w_down[0].astype(BF16),
        bgu[:, None, 0::2], bgu[:, None, 1::2], b_down[0][:, None, :])

    yp, ys = _combine(y_slots, slot, gate, x1, final_norm_g.reshape(1, d), npt)
    return yp.reshape(bp, lp, d), ys.reshape(bs, ls, d)
```

```python
import functools

import numpy as np
import jax
import jax.numpy as jnp
from jax import lax
from jax.experimental import pallas as pl
from jax.experimental.pallas import tpu as pltpu

F32 = jnp.float32
BF16 = jnp.bfloat16

N_META = 16
GRID_W = 64
HA_Q = 8
HA_KV = 2
HEAD_DIM = 64
WINDOW = 128
HB = 8
NB_ROWS = 8
NB_COLS = 16
N_EXPERTS = 32
TOP_K = 4
SWIGLU_ALPHA = 1.702
SWIGLU_LIMIT = 7.0
NORM_EPS = 1e-5
NEG_INF = -1e30
SCALE = HEAD_DIM ** -0.5

LANE = 128
SUB = 8
ROW_TILE = 512
WIN_TQ = 256
NBR_ROWS = 4
FFN_BM = 512
DMA_UNROLL = 8
DISPATCH_TILE = ROW_TILE // 2
RUN = 8
MAX_RUNS = DISPATCH_TILE * TOP_K // RUN + N_EXPERTS
CMB_TM = 256
VMEM_LIMIT = 56 * 1024 * 1024


def _rms(x, g):
    return x * lax.rsqrt(jnp.mean(x * x, axis=-1, keepdims=True) + NORM_EPS) * g


def _bf16_part(x):
    bits = lax.bitcast_convert_type(x, jnp.uint32) & jnp.uint32(0xFFFF0000)
    return lax.bitcast_convert_type(bits, F32)


def _sigmoid(z):
    return 1.0 / (1.0 + jnp.exp(-z))


def _params(*sem):
    return pltpu.CompilerParams(dimension_semantics=sem, vmem_limit_bytes=VMEM_LIMIT)


def _inproj_kernel(npb, segs, xp_ref, xs_ref, g_ref, w_ref, *o_refs):
    i = pl.program_id(0)
    x = jnp.where(i < npb, xp_ref[...], xs_ref[...])
    h = _rms(x, g_ref[...]).astype(BF16)
    for (c0, width), o_ref in zip(segs, o_refs):
        for off in range(0, width, 512):
            n = min(512, width - off)
            o_ref[:, off:off + n] = jnp.dot(
                h, w_ref[:, c0 + off:c0 + off + n], preferred_element_type=F32).astype(BF16)


def _inproj(xp, xs, g, w, segs):
    npt, d = xp.shape
    t = npt + xs.shape[0]
    tm = ROW_TILE
    npb = npt // tm
    return pl.pallas_call(
        functools.partial(_inproj_kernel, npb, segs),
        grid=(t // tm,),
        in_specs=[
            pl.BlockSpec((tm, d), lambda i: (jnp.minimum(i, npb - 1), 0)),
            pl.BlockSpec((tm, d), lambda i: (jnp.maximum(i - npb, 0), 0)),
            pl.BlockSpec((1, d), lambda i: (0, 0)),
            pl.BlockSpec(w.shape, lambda i: (0, 0)),
        ],
        out_specs=[pl.BlockSpec((tm, width), lambda i: (i, 0)) for _, width in segs],
        out_shape=[jax.ShapeDtypeStruct((t, width), BF16) for _, width in segs],
        compiler_params=_params("arbitrary"),
        name="inproj",
    )(xp, xs, g, w)


def _meta_kernel(m_ref, g_ref, w_ref, o_ref):
    h = _rms(m_ref[...], g_ref[...]).astype(BF16)
    o_ref[...] = jnp.dot(h, w_ref[...], preferred_element_type=F32).astype(BF16)


def _meta_proj(meta, g, w):
    return pl.pallas_call(
        _meta_kernel,
        out_shape=jax.ShapeDtypeStruct((meta.shape[0], w.shape[1]), BF16),
        name="meta_proj",
    )(meta, g, w)


def _window_bias(tq):
    slopes = 2.0 ** (-8.0 * np.arange(1, HA_Q + 1) / HA_Q)
    q = np.arange(tq)[:, None]
    col = np.arange(tq + 2 * WINDOW)[None, :]
    dist = np.abs(col - WINDOW - q)
    band = np.where((dist <= WINDOW)[None], -slopes[:, None, None] * dist[None], NEG_INF)
    full = np.concatenate([band, np.broadcast_to(_meta_cols(), (HA_Q, tq, LANE))], axis=-1)
    return jnp.asarray(full.reshape(HA_Q // 2, 2 * tq, -1), F32)


def _meta_cols():
    return np.where(np.arange(LANE) < N_META, 0.0, NEG_INF)


def _pad_meta(a):
    return jnp.pad(a, ((0, LANE - a.shape[0]), (0, 0)))


def _seq_pos(start, npt, lp, ls):
    is_p = start < npt
    pos = jnp.where(is_p, lax.rem(start, lp), lax.rem(jnp.maximum(start - npt, 0), ls))
    return pos, jnp.where(is_p, lp, ls)


def _window_kernel(npt, lp, ls, sink_ref, qa_ref, kp_ref, kc_ref, kn_ref, vp_ref, vc_ref, vn_ref,
                   km_ref, vm_ref, bias_ref, o_ref):
    tq = qa_ref.shape[0]
    nk = tq + 2 * WINDOW
    pos, seq_len = _seq_pos(pl.program_id(0) * tq, npt, lp, ls)
    pen_prev = jnp.where(pos == 0, NEG_INF, 0.0).astype(F32)
    pen_next = jnp.where(pos + tq == seq_len, NEG_INF, 0.0).astype(F32)
    col = lax.broadcasted_iota(jnp.int32, (1, nk + LANE), 1)
    pen = (jnp.where(col < WINDOW, pen_prev, 0.0)
           + jnp.where((col >= WINDOW + tq) & (col < nk), pen_next, 0.0))
    lo = lax.broadcasted_iota(jnp.int32, (1, LANE), 1) < HEAD_DIM
    top = lax.broadcasted_iota(jnp.int32, (2 * tq, 1), 0) < tq
    for i in range(HA_Q // 2):
        hk = (2 * i) // (HA_Q // HA_KV)
        sl = slice(i * LANE, (i + 1) * LANE)
        ksl = slice(hk * LANE, (hk + 1) * LANE)
        q = qa_ref[:, sl] * jnp.asarray(SCALE, BF16)
        zero = jnp.zeros_like(q)
        qq = jnp.concatenate([jnp.where(lo, q, zero), jnp.where(lo, zero, q)], axis=0)
        k_all = jnp.concatenate([kp_ref[:, ksl], kc_ref[:, ksl], kn_ref[:, ksl], km_ref[:, ksl]], axis=0)
        v_all = jnp.concatenate([vp_ref[:, ksl], vc_ref[:, ksl], vn_ref[:, ksl], vm_ref[:, ksl]], axis=0)
        s = lax.dot_general(qq, k_all, (((1,), (1,)), ((), ())), preferred_element_type=F32)
        s = s + bias_ref[i] + pen
        sink = jnp.where(top, sink_ref[2 * i], sink_ref[2 * i + 1])
        m = jnp.maximum(jnp.max(s, axis=-1, keepdims=True), sink)
        p = jnp.exp(s - m)
        denom = jnp.sum(p, axis=-1, keepdims=True) + jnp.exp(sink - m)
        o = jnp.dot(p.astype(BF16), v_all, preferred_element_type=F32) / denom
        o_ref[:, sl] = jnp.where(lo, o[:tq], o[tq:]).astype(BF16)


def _window_attention(qa, ka2, va2, km2, vm2, sink, npt, lp, ls):
    t = qa.shape[0]
    tq = WIN_TQ
    assert lp % tq == 0 and ls % tq == 0 and tq == 2 * WINDOW
    nhalf = t // WINDOW
    bias = _window_bias(tq)
    kw = ka2.shape[1]
    prev = pl.BlockSpec((WINDOW, kw), lambda j: (jnp.maximum(2 * j - 1, 0), 0))
    cur = pl.BlockSpec((tq, kw), lambda j: (j, 0))
    nxt = pl.BlockSpec((WINDOW, kw), lambda j: (jnp.minimum(2 * j + 2, nhalf - 1), 0))
    whole = lambda a: pl.BlockSpec(a.shape, lambda j: (0,) * a.ndim)
    return pl.pallas_call(
        functools.partial(_window_kernel, npt, lp, ls),
        grid=(t // tq,),
        in_specs=[
            pl.BlockSpec(memory_space=pltpu.SMEM),
            pl.BlockSpec((tq, qa.shape[1]), lambda j: (j, 0)),
            prev, cur, nxt, prev, cur, nxt,
            whole(km2), whole(vm2), whole(bias),
        ],
        out_specs=pl.BlockSpec((tq, qa.shape[1]), lambda j: (j, 0)),
        out_shape=jax.ShapeDtypeStruct(qa.shape, BF16),
        compiler_params=_params("arbitrary"),
        name="window_attn",
    )(sink, qa, ka2, ka2, ka2, va2, va2, va2, km2, vm2, bias)


def _nbr_tables(rpb):
    r, w = NBR_ROWS, GRID_W
    qc = np.arange(w)[:, None]
    kcol = np.arange(w)[None, :]
    col_start = np.clip(qc - NB_COLS // 2, 0, w - NB_COLS)
    col_ok = (kcol >= col_start) & (kcol < col_start + NB_COLS)
    sel = np.zeros((2 * NB_COLS - 1, w, w), np.float32)
    ci, ki = np.nonzero(col_ok)
    sel[ki - ci + NB_COLS - 1, ci, ki] = 1.0
    per_row = jnp.einsum("hrj,jck->hrck", rpb.astype(F32), jnp.asarray(sel), precision=lax.Precision.HIGHEST)
    per_row = jnp.where(jnp.asarray(col_ok), per_row, NEG_INF)
    off = NB_ROWS - 1 - NBR_ROWS
    vals = jnp.concatenate(
        [jnp.concatenate([per_row[:, b - a + off] for b in range(3 * r)], axis=-1) for a in range(r)], axis=1)
    meta = jnp.broadcast_to(jnp.asarray(_meta_cols(), F32), (HB, r * w, LANE))
    bias = jnp.concatenate([vals, meta], axis=-1)
    a = np.repeat(np.arange(r), w)[:, None]
    b = np.repeat(np.arange(3 * r), w)[None, :]
    first = NBR_ROWS
    ok = np.stack([
        (b >= first) & (b < first + NB_ROWS) & (a >= 0),
        (b >= a) & (b < a + NB_ROWS),
        (b >= 0) & (b < NB_ROWS) & (a >= 0),
    ])
    ok = np.concatenate([ok, np.ones((3, r * w, LANE), bool)], axis=-1)
    return bias, jnp.asarray(np.where(ok, 0.0, NEG_INF), F32)


def _nbr_kernel(npg, gp, gs, qb_ref, kp_ref, kc_ref, kn_ref, vp_ref, vc_ref, vn_ref, km_ref, vm_ref,
                bias_ref, mask_ref, o_ref):
    tq = qb_ref.shape[0]
    g = pl.program_id(0)
    is_p = g < npg
    per_seq = jnp.where(is_p, gp, gs)
    r = jnp.where(is_p, lax.rem(g, gp), lax.rem(jnp.maximum(g - npg, 0), gs))
    pat = jnp.where(r == 0, 0, jnp.where(r == per_seq - 1, 2, 1))
    row_mask = mask_ref[pat]
    lo = lax.broadcasted_iota(jnp.int32, (1, LANE), 1) < HEAD_DIM
    for i in range(HB // 2):
        sl = slice(i * LANE, (i + 1) * LANE)
        q = qb_ref[:, sl] * jnp.asarray(SCALE, BF16)
        zero = jnp.zeros_like(q)
        qq = jnp.concatenate([jnp.where(lo, q, zero), jnp.where(lo, zero, q)], axis=0)
        k_all = jnp.concatenate([kp_ref[:, sl], kc_ref[:, sl], kn_ref[:, sl], km_ref[:, sl]], axis=0)
        v_all = jnp.concatenate([vp_ref[:, sl], vc_ref[:, sl], vn_ref[:, sl], vm_ref[:, sl]], axis=0)
        s = lax.dot_general(qq, k_all, (((1,), (1,)), ((), ())), preferred_element_type=F32)
        halves = []
        for hf in range(2):
            sh = s[hf * tq:(hf + 1) * tq] + bias_ref[2 * i + hf] + row_mask
            m = jnp.max(sh, axis=-1, keepdims=True)
            p = jnp.exp(sh - m)
            denom = jnp.sum(p, axis=-1, keepdims=True)
            halves.append(jnp.dot(p.astype(BF16), v_all, preferred_element_type=F32) / denom)
        o_ref[:, sl] = jnp.where(lo, halves[0], halves[1]).astype(BF16)


def _nbr_attention(qb, kb, vb, kbm, vbm, rpb, npt, lp, ls):
    t, width = qb.shape
    tq = NBR_ROWS * GRID_W
    rows_p, rows_s = lp // GRID_W, ls // GRID_W
    assert rows_p % NBR_ROWS == 0 and rows_s % NBR_ROWS == 0 and min(rows_p, rows_s) >= NB_ROWS
    assert NB_ROWS == 2 * NBR_ROWS
    ng = t // tq
    bias, mask = _nbr_tables(rpb)
    prev = pl.BlockSpec((tq, width), lambda g: (jnp.maximum(g - 1, 0), 0))
    cur = pl.BlockSpec((tq, width), lambda g: (g, 0))
    nxt = pl.BlockSpec((tq, width), lambda g: (jnp.minimum(g + 1, ng - 1), 0))
    whole = lambda a: pl.BlockSpec(a.shape, lambda g: (0,) * a.ndim)
    return pl.pallas_call(
        functools.partial(_nbr_kernel, npt // tq, rows_p // NBR_ROWS, rows_s // NBR_ROWS),
        grid=(ng,),
        in_specs=[cur, prev, cur, nxt, prev, cur, nxt, whole(kbm), whole(vbm), whole(bias), whole(mask)],
        out_specs=cur,
        out_shape=jax.ShapeDtypeStruct(qb.shape, BF16),
        compiler_params=_params("arbitrary"),
        name="nbr_attn",
    )(qb, kb, kb, kb, vb, vb, vb, kbm, vbm, bias, mask)


def _merge_kernel(npb, oa_ref, ob_ref, g_ref, xp_ref, xs_ref, wa_ref, wb_ref, wo_ref, g2_ref,
                  rw_ref, rb_ref, tri_ref, x1_ref, h2_ref, idx_ref, gate_ref, rank_ref, cnt_ref, cs_ref,
                  carry):
    i = pl.program_id(0)
    d = x1_ref.shape[1]
    x = jnp.where(i < npb, xp_ref[...], xs_ref[...])
    a = jnp.dot(oa_ref[...], wa_ref[...], preferred_element_type=F32)
    b = jnp.dot(ob_ref[...], wb_ref[...], preferred_element_type=F32)
    mix = _sigmoid(g_ref[:, :d].astype(F32)) * a + _sigmoid(g_ref[:, d:].astype(F32)) * b
    x1 = x + jnp.dot(mix.astype(BF16), wo_ref[...], preferred_element_type=F32)
    x1_ref[...] = x1
    h2 = _rms(x1, g2_ref[...])
    h2_ref[...] = h2.astype(BF16)
    h_top = _bf16_part(h2)
    h_hi = h_top.astype(BF16)
    h_lo = (h2 - h_top).astype(BF16)
    r_hi = jnp.dot(h_hi, rw_ref[...], preferred_element_type=F32)
    r_lo = jnp.dot(h_lo, rw_ref[:, :LANE], preferred_element_type=F32)
    logits = r_hi[:, :LANE] + r_hi[:, LANE:] + r_lo + rb_ref[...]

    lane = lax.broadcasted_iota(jnp.int32, logits.shape, 1).astype(F32)
    work = logits
    member = jnp.zeros(logits.shape, F32)
    vals, idxs = [], []
    for _ in range(TOP_K):
        m = jnp.max(work, axis=-1, keepdims=True)
        idx = jnp.min(jnp.where(work == m, lane, float(LANE)), axis=-1, keepdims=True)
        hit = lane == idx
        work = jnp.where(hit, -jnp.inf, work)
        member = jnp.where(hit, 1.0, member)
        vals.append(m)
        idxs.append(idx)
    e = [jnp.exp(v - vals[0]) for v in vals]
    denom = functools.reduce(lambda p, q: p + q, e)
    gate_ref[...] = jnp.concatenate(e, axis=1) / denom
    idx_ref[...] = jnp.concatenate(idxs, axis=1).astype(jnp.int32)

    @pl.when(i == 0)
    def _():
        carry[...] = jnp.zeros_like(carry)

    before = jnp.dot(tri_ref[...], member.astype(BF16), preferred_element_type=F32) + carry[...]
    ranks = [jnp.sum(jnp.where(lane == idx, before, 0.0), axis=-1, keepdims=True) for idx in idxs]
    rank_ref[...] = jnp.concatenate(ranks, axis=1).astype(jnp.int32)
    half = member.shape[0] // 2
    c0 = carry[...]
    c1 = c0 + jnp.sum(member[:half], axis=0, keepdims=True)
    cs_ref[0] = jnp.concatenate([c0, c1], axis=0).astype(jnp.int32)
    carry[...] = c1 + jnp.sum(member[half:], axis=0, keepdims=True)
    cnt_ref[...] = carry[...].astype(jnp.int32)


def _merge(oa, ob, gates, xp, xs, wa, wb, wo, g2, rw, rb):
    npt, d = xp.shape
    t = npt + xs.shape[0]
    tm = ROW_TILE
    npb = npt // tm
    tri = jnp.asarray(np.tril(np.ones((tm, tm)), -1), BF16)
    row = lambda width: pl.BlockSpec((tm, width), lambda i: (i, 0))
    whole = lambda a: pl.BlockSpec(a.shape, lambda i: (0,) * a.ndim)
    return pl.pallas_call(
        functools.partial(_merge_kernel, npb),
        grid=(t // tm,),
        in_specs=[
            row(oa.shape[1]), row(ob.shape[1]), row(gates.shape[1]),
            pl.BlockSpec((tm, d), lambda i: (jnp.minimum(i, npb - 1), 0)),
            pl.BlockSpec((tm, d), lambda i: (jnp.maximum(i - npb, 0), 0)),
            whole(wa), whole(wb), whole(wo), whole(g2), whole(rw), whole(rb), whole(tri),
        ],
        out_specs=[row(d), row(d), row(TOP_K), row(TOP_K), row(TOP_K),
                   pl.BlockSpec((1, LANE), lambda i: (0, 0)),
                   pl.BlockSpec((1, 2, LANE), lambda i: (i, 0, 0))],
        out_shape=[jax.ShapeDtypeStruct((t, d), F32), jax.ShapeDtypeStruct((t, d), BF16),
                   jax.ShapeDtypeStruct((t, TOP_K), jnp.int32), jax.ShapeDtypeStruct((t, TOP_K), F32),
                   jax.ShapeDtypeStruct((t, TOP_K), jnp.int32),
                   jax.ShapeDtypeStruct((1, LANE), jnp.int32),
                   jax.ShapeDtypeStruct((t // tm, 2, LANE), jnp.int32)],
        scratch_shapes=[pltpu.VMEM((1, LANE), F32)],
        compiler_params=_params("arbitrary"),
        name="merge_router",
    )(oa, ob, gates, xp, xs, wa, wb, wo, g2, rw, rb, tri)


def _route(top_idx, rank, counts, cstart, bm):
    t = top_idx.shape[0]
    ne = N_EXPERTS
    dt = DISPATCH_TILE
    n_d = t // dt
    counts = counts.reshape(-1)[:ne]
    padded = (counts + RUN - 1 + bm - 1) // bm * bm
    pad_end = jnp.cumsum(padded)
    pad_start = pad_end - padded
    n_blocks = -(-(t * TOP_K + ne * (RUN - 1)) // bm) + ne
    experts = jnp.arange(ne, dtype=jnp.int32)

    cs = cstart.reshape(n_d, LANE)[:, :ne]
    cnt = jnp.concatenate([cs[1:], counts[None]], axis=0) - cs
    c_run = (cnt + RUN - 1) // RUN * RUN
    off = jnp.cumsum(c_run, axis=1) - c_run
    onehot = (top_idx[..., None] == experts).reshape(n_d, dt, TOP_K, ne)
    per_tok = lambda tab: jnp.sum(jnp.where(onehot, tab[:, None, None, :], 0), axis=-1)
    rank_t = rank.reshape(n_d, dt, TOP_K)
    pos = (per_tok(off) + rank_t - per_tok(cs)).transpose(0, 2, 1).astype(jnp.int32)
    dst0 = pad_start[None, :] + cs
    slot = (rank_t + per_tok(jnp.broadcast_to(pad_start[None, :], cs.shape))).reshape(-1).astype(jnp.int32)

    n_run = c_run // RUN
    cum = jnp.cumsum(n_run, axis=1)
    i = jnp.arange(MAX_RUNS, dtype=jnp.int32)
    e_of = jnp.minimum(jnp.sum(i[None, :, None] >= cum[:, None, :], axis=-1), ne - 1)
    pick = lambda tab: jnp.sum(jnp.where(e_of[..., None] == experts, tab[:, None, :], 0), axis=-1)
    local = RUN * (i[None, :] - pick(cum - n_run))
    runs = jnp.concatenate([pick(off) + local, pick(dst0) + local, cum[:, -1:]], axis=1).astype(jnp.int32)

    j = jnp.arange(bm, dtype=jnp.int32)[None, :]
    fill = (pad_start + counts)[:, None] + j
    spare = n_blocks * bm + experts[:, None] * bm + j
    pad_slots = jnp.where(fill < pad_end[:, None], fill, spare).reshape(-1).astype(jnp.int32)
    starts = jnp.arange(n_blocks, dtype=jnp.int32) * bm
    block_e = jnp.minimum(jnp.sum(starts[:, None] >= pad_end[None, :], axis=-1), ne - 1)
    n_used = (pad_end[-1] // bm).astype(jnp.int32).reshape(1)
    return slot, pos, runs.reshape(n_d, 1, -1), pad_slots, block_e.astype(jnp.int32), n_used, n_blocks


def _dispatch_kernel(runs_ref, prev_ref, pad_ref, pos_ref, h2_ref, x_hbm, sbuf, sem):
    sb = sbuf.shape[1] // SUB
    dt = h2_ref.shape[0]
    i = pl.program_id(0)
    cur = lax.rem(i, 2)
    rows = lax.broadcasted_iota(jnp.int32, (sb, dt), 0)
    hit = rows == pos_ref[0, 0:1, :]
    for k in range(1, TOP_K):
        hit = hit | (rows == pos_ref[0, k:k + 1, :])
    onehot = jnp.where(hit, 1.0, 0.0).astype(BF16)
    _store_token_tiled(sbuf.at[cur], jnp.dot(onehot, h2_ref[...], preferred_element_type=F32))

    def run_copy(ref, buf, r):
        src = sbuf.at[buf, pl.ds(pl.multiple_of(ref[0, 0, r] * SUB, SUB), RUN * SUB)]
        dst = x_hbm.at[pl.ds(pl.multiple_of(ref[0, 0, MAX_RUNS + r] * SUB, SUB), RUN * SUB)]
        return pltpu.make_async_copy(src, dst, sem)

    def drain(ref, buf):
        def wait(r, carry):
            run_copy(ref, buf, r).wait()
            return carry
        lax.fori_loop(0, ref[0, 0, 2 * MAX_RUNS], wait, 0)

    @pl.when(i > 0)
    def _():
        drain(prev_ref, 1 - cur)

    def start(r, carry):
        run_copy(runs_ref, cur, r).start()
        return carry
    lax.fori_loop(0, runs_ref[0, 0, 2 * MAX_RUNS], start, 0)

    @pl.when(i == 0)
    def _():
        n_pad = pad_ref.shape[0]

        def pad_copy(j):
            return pltpu.make_async_copy(sbuf.at[0, _tile_rows(0)], x_hbm.at[_tile_rows(pad_ref[j])], sem)

        def pad_start(g, carry):
            for u in range(DMA_UNROLL):
                pad_copy(g * DMA_UNROLL + u).start(priority=u % 2)
            return carry
        lax.fori_loop(0, n_pad // DMA_UNROLL, pad_start, 0)

        def pad_wait(j, carry):
            pad_copy(j).wait()
            return carry
        lax.fori_loop(0, n_pad, pad_wait, 0)

    @pl.when(i == pl.num_programs(0) - 1)
    def _():
        drain(runs_ref, cur)


def _dispatch(h2, pos, runs, pad_slots, n_slots):
    t, d = h2.shape
    dt = DISPATCH_TILE
    sb = dt * TOP_K + N_EXPERTS * RUN
    return pl.pallas_call(
        _dispatch_kernel,
        grid=(t // dt,),
        in_specs=[
            pl.BlockSpec((1, 1, runs.shape[2]), lambda i: (i, 0, 0), memory_space=pltpu.SMEM),
            pl.BlockSpec((1, 1, runs.shape[2]), lambda i: (jnp.maximum(i - 1, 0), 0, 0),
                         memory_space=pltpu.SMEM),
            pl.BlockSpec(memory_space=pltpu.SMEM),
            pl.BlockSpec((1, TOP_K, dt), lambda i: (i, 0, 0)),
            pl.BlockSpec((dt, d), lambda i: (i, 0)),
        ],
        out_specs=pl.BlockSpec(memory_space=pl.ANY),
        out_shape=jax.ShapeDtypeStruct((n_slots * SUB, LANE), F32),
        scratch_shapes=[pltpu.VMEM((2, sb * SUB, LANE), F32), pltpu.SemaphoreType.DMA(())],
        compiler_params=_params("arbitrary"),
        name="dispatch",
    )(runs, runs, pad_slots, pos, h2)


def _wprep_kernel(w_ref, eo_ref, o_ref):
    half = o_ref.shape[2] // 2
    two = 2 * LANE
    for c in range(w_ref.shape[2] // two):
        blk = w_ref[0, :, c * two:(c + 1) * two].astype(BF16)
        r = jnp.dot(blk, eo_ref[...], preferred_element_type=F32)
        o_ref[0, :, c * LANE:(c + 1) * LANE] = r[:, :LANE].astype(BF16)
        o_ref[0, :, half + c * LANE:half + (c + 1) * LANE] = r[:, LANE:].astype(BF16)


def _wprep(w):
    ne, d, f2 = w.shape
    eo = np.zeros((2 * LANE, 2 * LANE))
    eo[2 * np.arange(LANE), np.arange(LANE)] = 1.0
    eo[2 * np.arange(LANE) + 1, LANE + np.arange(LANE)] = 1.0
    return pl.pallas_call(
        _wprep_kernel,
        grid=(ne,),
        in_specs=[pl.BlockSpec((1, d, f2), lambda e: (e, 0, 0)),
                  pl.BlockSpec((2 * LANE, 2 * LANE), lambda e: (0, 0))],
        out_specs=pl.BlockSpec((1, d, f2), lambda e: (e, 0, 0)),
        out_shape=jax.ShapeDtypeStruct(w.shape, BF16),
        compiler_params=_params("arbitrary"),
        name="expert_weight_layout",
    )(w, jnp.asarray(eo, BF16))


def _tile_rows(r):
    return pl.ds(pl.multiple_of(r * SUB, SUB), SUB)


def _store_token_tiled(ref, val):
    n = val.shape[0]
    for c in range(SUB):
        ref[pl.ds(c, n, stride=SUB), :] = val[:, c * LANE:(c + 1) * LANE]


def _load_token_tiled(ref, n):
    return jnp.concatenate([ref[pl.ds(c, n, stride=SUB), :] for c in range(SUB)], axis=1)


def _ffn_kernel(be_ref, nu_ref, x_ref, wg_ref, wu_ref, wd_ref, bg_ref, bu_ref, bd_ref, y_ref):
    bm = y_ref.shape[0] // SUB
    b = pl.program_id(0)
    n_used = nu_ref[0]

    @pl.when(b < n_used)
    def _():
        x = _load_token_tiled(x_ref, bm).astype(BF16)
        hg = jnp.dot(x, wg_ref[0], preferred_element_type=F32) + bg_ref[0]
        hu = jnp.dot(x, wu_ref[0], preferred_element_type=F32) + bu_ref[0]
        glu = jnp.minimum(hg, SWIGLU_LIMIT)
        lin = jnp.clip(hu, -SWIGLU_LIMIT, SWIGLU_LIMIT)
        act = glu * _sigmoid(SWIGLU_ALPHA * glu) * (lin + 1.0)
        y = jnp.dot(act.astype(BF16), wd_ref[0], preferred_element_type=F32) + bd_ref[0]
        _store_token_tiled(y_ref, y)

    @pl.when(b >= n_used)
    def _():
        y_ref[...] = jnp.zeros_like(y_ref)


def _expert_ffn(x_slots, block_e, n_used, wgu, wd, bg, bu, bd):
    bm = FFN_BM
    d, f = wd.shape[2], wd.shape[1]
    assert d == SUB * LANE
    nb = block_e.shape[0]
    wspec = lambda a: pl.BlockSpec((1,) + a.shape[1:], lambda b, be, nu: (be[b], 0, 0))
    wg_spec = pl.BlockSpec((1, d, f), lambda b, be, nu: (be[b], 0, 0))
    wu_spec = pl.BlockSpec((1, d, f), lambda b, be, nu: (be[b], 0, 1))
    rows = pl.BlockSpec((bm * SUB, LANE), lambda b, be, nu: (b, 0))
    return pl.pallas_call(
        _ffn_kernel,
        grid_spec=pltpu.PrefetchScalarGridSpec(
            num_scalar_prefetch=2,
            grid=(nb,),
            in_specs=[rows, wg_spec, wu_spec, wspec(wd), wspec(bg), wspec(bu), wspec(bd)],
            out_specs=rows,
        ),
        out_shape=jax.ShapeDtypeStruct((nb * bm * SUB, LANE), F32),
        compiler_params=_params("arbitrary"),
        name="expert_ffn",
    )(block_e, n_used, x_slots, wgu, wgu, wd, bg, bu, bd)


def _combine_kernel(npb, sl_ref, sln_ref, gate_ref, x1_ref, gf_ref, y_hbm, yp_ref, ys_ref, ybuf, sem):
    tm = x1_ref.shape[0]
    i = pl.program_id(0)
    n = pl.num_programs(0)
    slot = lax.rem(i, 2)

    def issue(s_ref, s):
        def body(tok, carry):
            for k in range(TOP_K):
                src = y_hbm.at[_tile_rows(s_ref[0, 0, tok * TOP_K + k])]
                pltpu.make_async_copy(src, ybuf.at[s, k, _tile_rows(tok)], sem.at[s]).start(priority=k % 2)
            return carry
        lax.fori_loop(0, tm, body, 0, unroll=2)

    @pl.when(i == 0)
    def _():
        issue(sl_ref, 0)

    @pl.when(i + 1 < n)
    def _():
        issue(sln_ref, 1 - slot)

    for k in range(TOP_K):
        pltpu.make_async_copy(y_hbm.at[pl.ds(0, tm * SUB)], ybuf.at[slot, k], sem.at[slot]).wait()
    y = jnp.zeros(x1_ref.shape, F32)
    gate = gate_ref[...]
    for k in range(TOP_K):
        y = y + gate[:, k:k + 1] * _load_token_tiled(ybuf.at[slot, k], tm)
    out = _rms(x1_ref[...] + y, gf_ref[...])

    @pl.when(i < npb)
    def _():
        yp_ref[...] = out

    @pl.when(i >= npb)
    def _():
        ys_ref[...] = out


def _combine(y_slots, slot, gate, x1, gf, npt):
    t, d = x1.shape
    tm = CMB_TM
    nt = t // tm
    npb = npt // tm
    slot3 = slot.reshape(nt, 1, tm * TOP_K)
    smem_blk = lambda f: pl.BlockSpec((1, 1, tm * TOP_K), f, memory_space=pltpu.SMEM)
    return pl.pallas_call(
        functools.partial(_combine_kernel, npb),
        grid=(nt,),
        in_specs=[
            smem_blk(lambda i: (i, 0, 0)),
            smem_blk(lambda i: (jnp.minimum(i + 1, nt - 1), 0, 0)),
            pl.BlockSpec((tm, TOP_K), lambda i: (i, 0)),
            pl.BlockSpec((tm, d), lambda i: (i, 0)),
            pl.BlockSpec((1, d), lambda i: (0, 0)),
            pl.BlockSpec(memory_space=pl.ANY),
        ],
        out_specs=[
            pl.BlockSpec((tm, d), lambda i: (jnp.minimum(i, npb - 1), 0)),
            pl.BlockSpec((tm, d), lambda i: (jnp.maximum(i - npb, 0), 0)),
        ],
        out_shape=[jax.ShapeDtypeStruct((npt, d), F32), jax.ShapeDtypeStruct((t - npt, d), F32)],
        scratch_shapes=[pltpu.VMEM((2, TOP_K, tm * SUB, LANE), F32), pltpu.SemaphoreType.DMA((2,))],
        compiler_params=_params("arbitrary"),
        name="combine_norm",
    )(slot3, slot3, gate, x1, gf, y_slots)


def _inproj_columns(d):
    da_q, db = HA_Q * HEAD_DIM, HB * HEAD_DIM
    da_kv = HA_KV * HEAD_DIM
    q_b_end = da_q + db
    g_b_end = q_b_end + 2 * d
    ka0 = g_b_end
    va0 = ka0 + da_kv
    kb0 = va0 + da_kv
    vb0 = kb0 + db
    dup = lambda c0: np.concatenate(
        [np.tile(np.arange(c0 + h * HEAD_DIM, c0 + (h + 1) * HEAD_DIM), 2) for h in range(HA_KV)])
    cols = np.concatenate([np.arange(0, g_b_end), dup(ka0), dup(va0), np.arange(kb0, vb0 + db)])
    segs = ((0, da_q), (da_q, db), (q_b_end, 2 * d), (g_b_end, 2 * da_kv),
            (g_b_end + 2 * da_kv, 2 * da_kv), (g_b_end + 4 * da_kv, db), (g_b_end + 4 * da_kv + db, db))
    return cols, segs


def kernel(x_prompt, x_sample, meta_tokens, norm1_g, w_in, attn_sink, rel_pos_bias, w_branch_a,
           w_branch_b, w_out, norm2_g, router_w, router_b, w_gate_up, b_gate_up, w_down, b_down,
           final_norm_g):
    assert norm1_g.shape[0] == 1, "single-layer trunk"
    bp, lp, d = x_prompt.shape
    bs, ls, _ = x_sample.shape
    npt, nst = bp * lp, bs * ls
    assert npt % ROW_TILE == 0 and nst % ROW_TILE == 0
    xp = x_prompt.reshape(npt, d)
    xs = x_sample.reshape(nst, d)

    cols, segs = _inproj_columns(d)
    w_p = w_in[0][:, cols].astype(BF16)
    g1 = norm1_g[0].reshape(1, d)
    qa, qb, gates, ka2, va2, kb, vb = _inproj(xp, xs, g1, w_p, segs)
    kv0 = segs[3][0]
    meta_kv = _meta_proj(meta_tokens, g1, w_p[:, kv0:])
    w_kv = 2 * HA_KV * HEAD_DIM
    meta_kv = _pad_meta(meta_kv)
    km2, vm2 = meta_kv[:, :w_kv], meta_kv[:, w_kv:2 * w_kv]
    kbm, vbm = meta_kv[:, 2 * w_kv:2 * w_kv + HB * HEAD_DIM], meta_kv[:, 2 * w_kv + HB * HEAD_DIM:]

    oa = _window_attention(qa, ka2, va2, km2, vm2, attn_sink[0].astype(F32), npt, lp, ls)
    ob = _nbr_attention(qb, kb, vb, kbm, vbm, rel_pos_bias[0], npt, lp, ls)

    rw = router_w[0].astype(F32)
    rw_top = _bf16_part(rw)
    rw_hi = rw_top.astype(BF16)
    rw_lo = (rw - rw_top).astype(BF16)
    lane_pad = lambda a, fill: jnp.pad(a, ((0, 0), (0, LANE - a.shape[1])), constant_values=fill)
    rw2 = jnp.concatenate([lane_pad(rw_hi, 0), lane_pad(rw_lo, 0)], axis=1)
    rb = lane_pad(router_b[0].reshape(1, -1).astype(F32), -jnp.inf)
    x1, h2, top_idx, gate, rank, counts, cstart = _merge(
        oa, ob, gates, xp, xs, w_branch_a[0].astype(BF16), w_branch_b[0].astype(BF16),
        w_out[0].astype(BF16), norm2_g[0].reshape(1, d), rw2, rb)

    slot, pos, runs, pad_slots, block_e, n_used, n_blocks = _route(top_idx, rank, counts, cstart, FFN_BM)
    x_slots = _dispatch(h2, pos, runs, pad_slots, (n_blocks + N_EXPERTS) * FFN_BM)
    bgu = b_gate_up[0]
    y_slots = _expert_ffn(
        x_slots, block_e, n_used, _wprep(w_gate_up[0]), w_down[0].astype(BF16),
        bgu[:, None, 0::2], bgu[:, None, 1::2], b_down[0][:, None, :])

    yp, ys = _combine(y_slots, slot, gate, x1, final_norm_g.reshape(1, d), npt)
    return yp.reshape(bp, lp, d), ys.reshape(bs, ls, d)
```

```python
import functools

import numpy as np
import jax
import jax.numpy as jnp
from jax import lax
from jax.experimental import pallas as pl
from jax.experimental.pallas import tpu as pltpu

F32 = jnp.float32
BF16 = jnp.bfloat16

N_META = 16
GRID_W = 64
HA_Q = 8
HA_KV = 2
HEAD_DIM = 64
WINDOW = 128
HB = 8
NB_ROWS = 8
NB_COLS = 16
N_EXPERTS = 32
TOP_K = 4
SWIGLU_ALPHA = 1.702
SWIGLU_LIMIT = 7.0
NORM_EPS = 1e-5
NEG_INF = -1e30
SCALE = HEAD_DIM ** -0.5

LANE = 128
SUB = 8
ROW_TILE = 512
WIN_TQ = 256
NBR_ROWS = 4
FFN_BM = 512
DMA_UNROLL = 8
DISPATCH_TILE = ROW_TILE // 2
RUN = 8
MAX_RUNS = DISPATCH_TILE * TOP_K // RUN + N_EXPERTS
VMEM_LIMIT = 56 * 1024 * 1024


def _rms(x, g):
    return x * lax.rsqrt(jnp.mean(x * x, axis=-1, keepdims=True) + NORM_EPS) * g


def _bf16_part(x):
    bits = lax.bitcast_convert_type(x, jnp.uint32) & jnp.uint32(0xFFFF0000)
    return lax.bitcast_convert_type(bits, F32)


def _sigmoid(z):
    return 1.0 / (1.0 + jnp.exp(-z))


def _params(*sem):
    return pltpu.CompilerParams(dimension_semantics=sem, vmem_limit_bytes=VMEM_LIMIT)


def _inproj_kernel(npb, segs, xp_ref, xs_ref, g_ref, w_ref, *o_refs):
    i = pl.program_id(0)
    x = jnp.where(i < npb, xp_ref[...], xs_ref[...])
    h = _rms(x, g_ref[...]).astype(BF16)
    for (c0, width), o_ref in zip(segs, o_refs):
        for off in range(0, width, 512):
            n = min(512, width - off)
            o_ref[:, off:off + n] = jnp.dot(
                h, w_ref[:, c0 + off:c0 + off + n], preferred_element_type=F32).astype(BF16)


def _inproj(xp, xs, g, w, segs):
    npt, d = xp.shape
    t = npt + xs.shape[0]
    tm = ROW_TILE
    npb = npt // tm
    return pl.pallas_call(
        functools.partial(_inproj_kernel, npb, segs),
        grid=(t // tm,),
        in_specs=[
            pl.BlockSpec((tm, d), lambda i: (jnp.minimum(i, npb - 1), 0)),
            pl.BlockSpec((tm, d), lambda i: (jnp.maximum(i - npb, 0), 0)),
            pl.BlockSpec((1, d), lambda i: (0, 0)),
            pl.BlockSpec(w.shape, lambda i: (0, 0)),
        ],
        out_specs=[pl.BlockSpec((tm, width), lambda i: (i, 0)) for _, width in segs],
        out_shape=[jax.ShapeDtypeStruct((t, width), BF16) for _, width in segs],
        compiler_params=_params("arbitrary"),
        name="inproj",
    )(xp, xs, g, w)


def _meta_kernel(m_ref, g_ref, w_ref, o_ref):
    h = _rms(m_ref[...], g_ref[...]).astype(BF16)
    o_ref[...] = jnp.dot(h, w_ref[...], preferred_element_type=F32).astype(BF16)


def _meta_proj(meta, g, w):
    return pl.pallas_call(
        _meta_kernel,
        out_shape=jax.ShapeDtypeStruct((meta.shape[0], w.shape[1]), BF16),
        name="meta_proj",
    )(meta, g, w)


def _window_bias(tq):
    slopes = 2.0 ** (-8.0 * np.arange(1, HA_Q + 1) / HA_Q)
    q = np.arange(tq)[:, None]
    col = np.arange(tq + 2 * WINDOW)[None, :]
    dist = np.abs(col - WINDOW - q)
    band = np.where((dist <= WINDOW)[None], -slopes[:, None, None] * dist[None], NEG_INF)
    full = np.concatenate([band, np.broadcast_to(_meta_cols(), (HA_Q, tq, LANE))], axis=-1)
    return jnp.asarray(full.reshape(HA_Q // 2, 2 * tq, -1), F32)


def _meta_cols():
    return np.where(np.arange(LANE) < N_META, 0.0, NEG_INF)


def _pad_meta(a):
    return jnp.pad(a, ((0, LANE - a.shape[0]), (0, 0)))


def _seq_pos(start, npt, lp, ls):
    is_p = start < npt
    pos = jnp.where(is_p, lax.rem(start, lp), lax.rem(jnp.maximum(start - npt, 0), ls))
    return pos, jnp.where(is_p, lp, ls)


def _window_kernel(npt, lp, ls, sink_ref, qa_ref, kp_ref, kc_ref, kn_ref, vp_ref, vc_ref, vn_ref,
                   km_ref, vm_ref, bias_ref, o_ref):
    tq = qa_ref.shape[0]
    nk = tq + 2 * WINDOW
    pos, seq_len = _seq_pos(pl.program_id(0) * tq, npt, lp, ls)
    pen_prev = jnp.where(pos == 0, NEG_INF, 0.0).astype(F32)
    pen_next = jnp.where(pos + tq == seq_len, NEG_INF, 0.0).astype(F32)
    col = lax.broadcasted_iota(jnp.int32, (1, nk + LANE), 1)
    pen = (jnp.where(col < WINDOW, pen_prev, 0.0)
           + jnp.where((col >= WINDOW + tq) & (col < nk), pen_next, 0.0))
    lo = lax.broadcasted_iota(jnp.int32, (1, LANE), 1) < HEAD_DIM
    top = lax.broadcasted_iota(jnp.int32, (2 * tq, 1), 0) < tq
    for i in range(HA_Q // 2):
        hk = (2 * i) // (HA_Q // HA_KV)
        sl = slice(i * LANE, (i + 1) * LANE)
        ksl = slice(hk * LANE, (hk + 1) * LANE)
        q = qa_ref[:, sl] * jnp.asarray(SCALE, BF16)
        zero = jnp.zeros_like(q)
        qq = jnp.concatenate([jnp.where(lo, q, zero), jnp.where(lo, zero, q)], axis=0)
        k_all = jnp.concatenate([kp_ref[:, ksl], kc_ref[:, ksl], kn_ref[:, ksl], km_ref[:, ksl]], axis=0)
        v_all = jnp.concatenate([vp_ref[:, ksl], vc_ref[:, ksl], vn_ref[:, ksl], vm_ref[:, ksl]], axis=0)
        s = lax.dot_general(qq, k_all, (((1,), (1,)), ((), ())), preferred_element_type=F32)
        s = s + bias_ref[i] + pen
        sink = jnp.where(top, sink_ref[2 * i], sink_ref[2 * i + 1])
        m = jnp.maximum(jnp.max(s, axis=-1, keepdims=True), sink)
        p = jnp.exp(s - m)
        denom = jnp.sum(p, axis=-1, keepdims=True) + jnp.exp(sink - m)
        o = jnp.dot(p.astype(BF16), v_all, preferred_element_type=F32) / denom
        o_ref[:, sl] = jnp.where(lo, o[:tq], o[tq:]).astype(BF16)


def _window_attention(qa, ka2, va2, km2, vm2, sink, npt, lp, ls):
    t = qa.shape[0]
    tq = WIN_TQ
    assert lp % tq == 0 and ls % tq == 0 and tq == 2 * WINDOW
    nhalf = t // WINDOW
    bias = _window_bias(tq)
    kw = ka2.shape[1]
    prev = pl.BlockSpec((WINDOW, kw), lambda j: (jnp.maximum(2 * j - 1, 0), 0))
    cur = pl.BlockSpec((tq, kw), lambda j: (j, 0))
    nxt = pl.BlockSpec((WINDOW, kw), lambda j: (jnp.minimum(2 * j + 2, nhalf - 1), 0))
    whole = lambda a: pl.BlockSpec(a.shape, lambda j: (0,) * a.ndim)
    return pl.pallas_call(
        functools.partial(_window_kernel, npt, lp, ls),
        grid=(t // tq,),
        in_specs=[
            pl.BlockSpec(memory_space=pltpu.SMEM),
            pl.BlockSpec((tq, qa.shape[1]), lambda j: (j, 0)),
            prev, cur, nxt, prev, cur, nxt,
            whole(km2), whole(vm2), whole(bias),
        ],
        out_specs=pl.BlockSpec((tq, qa.shape[1]), lambda j: (j, 0)),
        out_shape=jax.ShapeDtypeStruct(qa.shape, BF16),
        compiler_params=_params("arbitrary"),
        name="window_attn",
    )(sink, qa, ka2, ka2, ka2, va2, va2, va2, km2, vm2, bias)


def _nbr_tables(rpb):
    r, w = NBR_ROWS, GRID_W
    qc = np.arange(w)[:, None]
    kcol = np.arange(w)[None, :]
    col_start = np.clip(qc - NB_COLS // 2, 0, w - NB_COLS)
    col_ok = (kcol >= col_start) & (kcol < col_start + NB_COLS)
    sel = np.zeros((2 * NB_COLS - 1, w, w), np.float32)
    ci, ki = np.nonzero(col_ok)
    sel[ki - ci + NB_COLS - 1, ci, ki] = 1.0
    per_row = jnp.einsum("hrj,jck->hrck", rpb.astype(F32), jnp.asarray(sel), precision=lax.Precision.HIGHEST)
    per_row = jnp.where(jnp.asarray(col_ok), per_row, NEG_INF)
    off = NB_ROWS - 1 - NBR_ROWS
    vals = jnp.concatenate(
        [jnp.concatenate([per_row[:, b - a + off] for b in range(3 * r)], axis=-1) for a in range(r)], axis=1)
    meta = jnp.broadcast_to(jnp.asarray(_meta_cols(), F32), (HB, r * w, LANE))
    bias = jnp.concatenate([vals, meta], axis=-1)
    a = np.repeat(np.arange(r), w)[:, None]
    b = np.repeat(np.arange(3 * r), w)[None, :]
    first = NBR_ROWS
    ok = np.stack([
        (b >= first) & (b < first + NB_ROWS) & (a >= 0),
        (b >= a) & (b < a + NB_ROWS),
        (b >= 0) & (b < NB_ROWS) & (a >= 0),
    ])
    ok = np.concatenate([ok, np.ones((3, r * w, LANE), bool)], axis=-1)
    return bias, jnp.asarray(np.where(ok, 0.0, NEG_INF), F32)


def _nbr_kernel(npg, gp, gs, qb_ref, kp_ref, kc_ref, kn_ref, vp_ref, vc_ref, vn_ref, km_ref, vm_ref,
                bias_ref, mask_ref, o_ref):
    tq = qb_ref.shape[0]
    g = pl.program_id(0)
    is_p = g < npg
    per_seq = jnp.where(is_p, gp, gs)
    r = jnp.where(is_p, lax.rem(g, gp), lax.rem(jnp.maximum(g - npg, 0), gs))
    pat = jnp.where(r == 0, 0, jnp.where(r == per_seq - 1, 2, 1))
    row_mask = mask_ref[pat]
    lo = lax.broadcasted_iota(jnp.int32, (1, LANE), 1) < HEAD_DIM
    for i in range(HB // 2):
        sl = slice(i * LANE, (i + 1) * LANE)
        q = qb_ref[:, sl] * jnp.asarray(SCALE, BF16)
        zero = jnp.zeros_like(q)
        qq = jnp.concatenate([jnp.where(lo, q, zero), jnp.where(lo, zero, q)], axis=0)
        k_all = jnp.concatenate([kp_ref[:, sl], kc_ref[:, sl], kn_ref[:, sl], km_ref[:, sl]], axis=0)
        v_all = jnp.concatenate([vp_ref[:, sl], vc_ref[:, sl], vn_ref[:, sl], vm_ref[:, sl]], axis=0)
        s = lax.dot_general(qq, k_all, (((1,), (1,)), ((), ())), preferred_element_type=F32)
        halves = []
        for hf in range(2):
            sh = s[hf * tq:(hf + 1) * tq] + bias_ref[2 * i + hf] + row_mask
            m = jnp.max(sh, axis=-1, keepdims=True)
            p = jnp.exp(sh - m)
            denom = jnp.sum(p, axis=-1, keepdims=True)
            halves.append(jnp.dot(p.astype(BF16), v_all, preferred_element_type=F32) / denom)
        o_ref[:, sl] = jnp.where(lo, halves[0], halves[1]).astype(BF16)


def _nbr_attention(qb, kb, vb, kbm, vbm, rpb, npt, lp, ls):
    t, width = qb.shape
    tq = NBR_ROWS * GRID_W
    rows_p, rows_s = lp // GRID_W, ls // GRID_W
    assert rows_p % NBR_ROWS == 0 and rows_s % NBR_ROWS == 0 and min(rows_p, rows_s) >= NB_ROWS
    assert NB_ROWS == 2 * NBR_ROWS
    ng = t // tq
    bias, mask = _nbr_tables(rpb)
    prev = pl.BlockSpec((tq, width), lambda g: (jnp.maximum(g - 1, 0), 0))
    cur = pl.BlockSpec((tq, width), lambda g: (g, 0))
    nxt = pl.BlockSpec((tq, width), lambda g: (jnp.minimum(g + 1, ng - 1), 0))
    whole = lambda a: pl.BlockSpec(a.shape, lambda g: (0,) * a.ndim)
    return pl.pallas_call(
        functools.partial(_nbr_kernel, npt // tq, rows_p // NBR_ROWS, rows_s // NBR_ROWS),
        grid=(ng,),
        in_specs=[cur, prev, cur, nxt, prev, cur, nxt, whole(kbm), whole(vbm), whole(bias), whole(mask)],
        out_specs=cur,
        out_shape=jax.ShapeDtypeStruct(qb.shape, BF16),
        compiler_params=_params("arbitrary"),
        name="nbr_attn",
    )(qb, kb, kb, kb, vb, vb, vb, kbm, vbm, bias, mask)


def _merge_kernel(npb, oa_ref, ob_ref, g_ref, xp_ref, xs_ref, wa_ref, wb_ref, wo_ref, g2_ref,
                  rw_ref, rb_ref, tri_ref, x1_ref, h2_ref, idx_ref, gate_ref, rank_ref, cnt_ref, cs_ref,
                  carry):
    i = pl.program_id(0)
    d = x1_ref.shape[1]
    x = jnp.where(i < npb, xp_ref[...], xs_ref[...])
    a = jnp.dot(oa_ref[...], wa_ref[...], preferred_element_type=F32)
    b = jnp.dot(ob_ref[...], wb_ref[...], preferred_element_type=F32)
    mix = _sigmoid(g_ref[:, :d].astype(F32)) * a + _sigmoid(g_ref[:, d:].astype(F32)) * b
    x1 = x + jnp.dot(mix.astype(BF16), wo_ref[...], preferred_element_type=F32)
    x1_ref[...] = x1
    h2 = _rms(x1, g2_ref[...])
    h2_ref[...] = h2.astype(BF16)
    h_top = _bf16_part(h2)
    h_hi = h_top.astype(BF16)
    h_lo = (h2 - h_top).astype(BF16)
    r_hi = jnp.dot(h_hi, rw_ref[...], preferred_element_type=F32)
    r_lo = jnp.dot(h_lo, rw_ref[:, :LANE], preferred_element_type=F32)
    logits = r_hi[:, :LANE] + r_hi[:, LANE:] + r_lo + rb_ref[...]

    lane = lax.broadcasted_iota(jnp.int32, logits.shape, 1).astype(F32)
    work = logits
    member = jnp.zeros(logits.shape, F32)
    vals, idxs = [], []
    for _ in range(TOP_K):
        m = jnp.max(work, axis=-1, keepdims=True)
        idx = jnp.min(jnp.where(work == m, lane, float(LANE)), axis=-1, keepdims=True)
        hit = lane == idx
        work = jnp.where(hit, -jnp.inf, work)
        member = jnp.where(hit, 1.0, member)
        vals.append(m)
        idxs.append(idx)
    e = [jnp.exp(v - vals[0]) for v in vals]
    denom = functools.reduce(lambda p, q: p + q, e)
    gate_ref[...] = jnp.concatenate(e, axis=1) / denom
    idx_ref[...] = jnp.concatenate(idxs, axis=1).astype(jnp.int32)

    @pl.when(i == 0)
    def _():
        carry[...] = jnp.zeros_like(carry)

    before = jnp.dot(tri_ref[...], member.astype(BF16), preferred_element_type=F32) + carry[...]
    ranks = [jnp.sum(jnp.where(lane == idx, before, 0.0), axis=-1, keepdims=True) for idx in idxs]
    rank_ref[...] = jnp.concatenate(ranks, axis=1).astype(jnp.int32)
    half = member.shape[0] // 2
    c0 = carry[...]
    c1 = c0 + jnp.sum(member[:half], axis=0, keepdims=True)
    cs_ref[0] = jnp.concatenate([c0, c1], axis=0).astype(jnp.int32)
    carry[...] = c1 + jnp.sum(member[half:], axis=0, keepdims=True)
    cnt_ref[...] = carry[...].astype(jnp.int32)


def _merge(oa, ob, gates, xp, xs, wa, wb, wo, g2, rw, rb):
    npt, d = xp.shape
    t = npt + xs.shape[0]
    tm = ROW_TILE
    npb = npt // tm
    tri = jnp.asarray(np.tril(np.ones((tm, tm)), -1), BF16)
    row = lambda width: pl.BlockSpec((tm, width), lambda i: (i, 0))
    whole = lambda a: pl.BlockSpec(a.shape, lambda i: (0,) * a.ndim)
    return pl.pallas_call(
        functools.partial(_merge_kernel, npb),
        grid=(t // tm,),
        in_specs=[
            row(oa.shape[1]), row(ob.shape[1]), row(gates.shape[1]),
            pl.BlockSpec((tm, d), lambda i: (jnp.minimum(i, npb - 1), 0)),
            pl.BlockSpec((tm, d), lambda i: (jnp.maximum(i - npb, 0), 0)),
            whole(wa), whole(wb), whole(wo), whole(g2), whole(rw), whole(rb), whole(tri),
        ],
        out_specs=[row(d), row(d), row(TOP_K), row(TOP_K), row(TOP_K),
                   pl.BlockSpec((1, LANE), lambda i: (0, 0)),
                   pl.BlockSpec((1, 2, LANE), lambda i: (i, 0, 0))],
        out_shape=[jax.ShapeDtypeStruct((t, d), F32), jax.ShapeDtypeStruct((t, d), BF16),
                   jax.ShapeDtypeStruct((t, TOP_K), jnp.int32), jax.ShapeDtypeStruct((t, TOP_K), F32),
                   jax.ShapeDtypeStruct((t, TOP_K), jnp.int32),
                   jax.ShapeDtypeStruct((1, LANE), jnp.int32),
                   jax.ShapeDtypeStruct((t // tm, 2, LANE), jnp.int32)],
        scratch_shapes=[pltpu.VMEM((1, LANE), F32)],
        compiler_params=_params("arbitrary"),
        name="merge_router",
    )(oa, ob, gates, xp, xs, wa, wb, wo, g2, rw, rb, tri)


def _route(top_idx, rank, counts, cstart, bm):
    t = top_idx.shape[0]
    ne = N_EXPERTS
    dt = DISPATCH_TILE
    n_d = t // dt
    counts = counts.reshape(-1)[:ne]
    padded = (counts + RUN - 1 + bm - 1) // bm * bm
    pad_end = jnp.cumsum(padded)
    pad_start = pad_end - padded
    n_blocks = -(-(t * TOP_K + ne * (RUN - 1)) // bm) + ne
    experts = jnp.arange(ne, dtype=jnp.int32)

    cs = cstart.reshape(n_d, LANE)[:, :ne]
    cnt = jnp.concatenate([cs[1:], counts[None]], axis=0) - cs
    c_run = (cnt + RUN - 1) // RUN * RUN
    off = jnp.cumsum(c_run, axis=1) - c_run
    onehot = (top_idx[..., None] == experts).reshape(n_d, dt, TOP_K, ne)
    per_tok = lambda tab: jnp.sum(jnp.where(onehot, tab[:, None, None, :], 0), axis=-1)
    rank_t = rank.reshape(n_d, dt, TOP_K)
    pos_tk = (per_tok(off) + rank_t - per_tok(cs)).astype(jnp.int32)
    pos = pos_tk.transpose(0, 2, 1)
    dst0 = pad_start[None, :] + cs

    n_run = c_run // RUN
    cum = jnp.cumsum(n_run, axis=1)
    i = jnp.arange(MAX_RUNS, dtype=jnp.int32)
    e_of = jnp.minimum(jnp.sum(i[None, :, None] >= cum[:, None, :], axis=-1), ne - 1)
    pick = lambda tab: jnp.sum(jnp.where(e_of[..., None] == experts, tab[:, None, :], 0), axis=-1)
    local = RUN * (i[None, :] - pick(cum - n_run))
    runs = jnp.concatenate([pick(off) + local, pick(dst0) + local, cum[:, -1:]], axis=1).astype(jnp.int32)

    j = jnp.arange(bm, dtype=jnp.int32)[None, :]
    fill = (pad_start + counts)[:, None] + j
    spare = n_blocks * bm + experts[:, None] * bm + j
    pad_slots = jnp.where(fill < pad_end[:, None], fill, spare).reshape(-1).astype(jnp.int32)
    starts = jnp.arange(n_blocks, dtype=jnp.int32) * bm
    block_e = jnp.minimum(jnp.sum(starts[:, None] >= pad_end[None, :], axis=-1), ne - 1)
    n_used = (pad_end[-1] // bm).astype(jnp.int32).reshape(1)
    return (pos_tk.reshape(t, TOP_K), pos, runs.reshape(n_d, 1, -1), pad_slots, block_e.astype(jnp.int32),
            n_used, n_blocks)


def _dispatch_kernel(runs_ref, prev_ref, pad_ref, pos_ref, h2_ref, x_hbm, sbuf, sem):
    sb = sbuf.shape[1] // SUB
    dt = h2_ref.shape[0]
    i = pl.program_id(0)
    cur = lax.rem(i, 2)
    rows = lax.broadcasted_iota(jnp.int32, (sb, dt), 0)
    hit = rows == pos_ref[0, 0:1, :]
    for k in range(1, TOP_K):
        hit = hit | (rows == pos_ref[0, k:k + 1, :])
    onehot = jnp.where(hit, 1.0, 0.0).astype(BF16)
    _store_token_tiled(sbuf.at[cur], jnp.dot(onehot, h2_ref[...], preferred_element_type=F32))

    def run_copy(ref, buf, r):
        src = sbuf.at[buf, pl.ds(pl.multiple_of(ref[0, 0, r] * SUB, SUB), RUN * SUB)]
        dst = x_hbm.at[pl.ds(pl.multiple_of(ref[0, 0, MAX_RUNS + r] * SUB, SUB), RUN * SUB)]
        return pltpu.make_async_copy(src, dst, sem)

    def drain(ref, buf):
        def wait(r, carry):
            run_copy(ref, buf, r).wait()
            return carry
        lax.fori_loop(0, ref[0, 0, 2 * MAX_RUNS], wait, 0)

    @pl.when(i > 0)
    def _():
        drain(prev_ref, 1 - cur)

    def start(r, carry):
        run_copy(runs_ref, cur, r).start()
        return carry
    lax.fori_loop(0, runs_ref[0, 0, 2 * MAX_RUNS], start, 0)

    @pl.when(i == 0)
    def _():
        n_pad = pad_ref.shape[0]

        def pad_copy(j):
            return pltpu.make_async_copy(sbuf.at[0, _tile_rows(0)], x_hbm.at[_tile_rows(pad_ref[j])], sem)

        def pad_start(g, carry):
            for u in range(DMA_UNROLL):
                pad_copy(g * DMA_UNROLL + u).start(priority=u % 2)
            return carry
        lax.fori_loop(0, n_pad // DMA_UNROLL, pad_start, 0)

        def pad_wait(j, carry):
            pad_copy(j).wait()
            return carry
        lax.fori_loop(0, n_pad, pad_wait, 0)

    @pl.when(i == pl.num_programs(0) - 1)
    def _():
        drain(runs_ref, cur)


def _dispatch(h2, pos, runs, pad_slots, n_slots):
    t, d = h2.shape
    dt = DISPATCH_TILE
    sb = dt * TOP_K + N_EXPERTS * RUN
    return pl.pallas_call(
        _dispatch_kernel,
        grid=(t // dt,),
        in_specs=[
            pl.BlockSpec((1, 1, runs.shape[2]), lambda i: (i, 0, 0), memory_space=pltpu.SMEM),
            pl.BlockSpec((1, 1, runs.shape[2]), lambda i: (jnp.maximum(i - 1, 0), 0, 0),
                         memory_space=pltpu.SMEM),
            pl.BlockSpec(memory_space=pltpu.SMEM),
            pl.BlockSpec((1, TOP_K, dt), lambda i: (i, 0, 0)),
            pl.BlockSpec((dt, d), lambda i: (i, 0)),
        ],
        out_specs=pl.BlockSpec(memory_space=pl.ANY),
        out_shape=jax.ShapeDtypeStruct((n_slots * SUB, LANE), F32),
        scratch_shapes=[pltpu.VMEM((2, sb * SUB, LANE), F32), pltpu.SemaphoreType.DMA(())],
        compiler_params=_params("arbitrary"),
        name="dispatch",
    )(runs, runs, pad_slots, pos, h2)


def _wprep_kernel(w_ref, eo_ref, o_ref):
    half = o_ref.shape[2] // 2
    two = 2 * LANE
    for c in range(w_ref.shape[2] // two):
        blk = w_ref[0, :, c * two:(c + 1) * two].astype(BF16)
        r = jnp.dot(blk, eo_ref[...], preferred_element_type=F32)
        o_ref[0, :, c * LANE:(c + 1) * LANE] = r[:, :LANE].astype(BF16)
        o_ref[0, :, half + c * LANE:half + (c + 1) * LANE] = r[:, LANE:].astype(BF16)


def _wprep(w):
    ne, d, f2 = w.shape
    eo = np.zeros((2 * LANE, 2 * LANE))
    eo[2 * np.arange(LANE), np.arange(LANE)] = 1.0
    eo[2 * np.arange(LANE) + 1, LANE + np.arange(LANE)] = 1.0
    return pl.pallas_call(
        _wprep_kernel,
        grid=(ne,),
        in_specs=[pl.BlockSpec((1, d, f2), lambda e: (e, 0, 0)),
                  pl.BlockSpec((2 * LANE, 2 * LANE), lambda e: (0, 0))],
        out_specs=pl.BlockSpec((1, d, f2), lambda e: (e, 0, 0)),
        out_shape=jax.ShapeDtypeStruct(w.shape, BF16),
        compiler_params=_params("arbitrary"),
        name="expert_weight_layout",
    )(w, jnp.asarray(eo, BF16))


def _tile_rows(r):
    return pl.ds(pl.multiple_of(r * SUB, SUB), SUB)


def _store_token_tiled(ref, val):
    n = val.shape[0]
    for c in range(SUB):
        ref[pl.ds(c, n, stride=SUB), :] = val[:, c * LANE:(c + 1) * LANE]


def _load_token_tiled(ref, n):
    return jnp.concatenate([ref[pl.ds(c, n, stride=SUB), :] for c in range(SUB)], axis=1)


def _ffn_kernel(be_ref, nu_ref, x_ref, wg_ref, wu_ref, wd_ref, bg_ref, bu_ref, bd_ref, y_ref):
    bm = y_ref.shape[0] // SUB
    b = pl.program_id(0)
    n_used = nu_ref[0]

    @pl.when(b < n_used)
    def _():
        x = _load_token_tiled(x_ref, bm).astype(BF16)
        hg = jnp.dot(x, wg_ref[0], preferred_element_type=F32) + bg_ref[0]
        hu = jnp.dot(x, wu_ref[0], preferred_element_type=F32) + bu_ref[0]
        glu = jnp.minimum(hg, SWIGLU_LIMIT)
        lin = jnp.clip(hu, -SWIGLU_LIMIT, SWIGLU_LIMIT)
        act = glu * _sigmoid(SWIGLU_ALPHA * glu) * (lin + 1.0)
        y = jnp.dot(act.astype(BF16), wd_ref[0], preferred_element_type=F32) + bd_ref[0]
        _store_token_tiled(y_ref, y)

    @pl.when(b >= n_used)
    def _():
        y_ref[...] = jnp.zeros_like(y_ref)


def _expert_ffn(x_slots, block_e, n_used, wgu, wd, bg, bu, bd):
    bm = FFN_BM
    d, f = wd.shape[2], wd.shape[1]
    assert d == SUB * LANE
    nb = block_e.shape[0]
    wspec = lambda a: pl.BlockSpec((1,) + a.shape[1:], lambda b, be, nu: (be[b], 0, 0))
    wg_spec = pl.BlockSpec((1, d, f), lambda b, be, nu: (be[b], 0, 0))
    wu_spec = pl.BlockSpec((1, d, f), lambda b, be, nu: (be[b], 0, 1))
    rows = pl.BlockSpec((bm * SUB, LANE), lambda b, be, nu: (b, 0))
    return pl.pallas_call(
        _ffn_kernel,
        grid_spec=pltpu.PrefetchScalarGridSpec(
            num_scalar_prefetch=2,
            grid=(nb,),
            in_specs=[rows, wg_spec, wu_spec, wspec(wd), wspec(bg), wspec(bu), wspec(bd)],
            out_specs=rows,
        ),
        out_shape=jax.ShapeDtypeStruct((nb * bm * SUB, LANE), F32),
        compiler_params=_params("arbitrary"),
        name="expert_ffn",
    )(block_e, n_used, x_slots, wgu, wgu, wd, bg, bu, bd)


def _combine_kernel(npb, runs_ref, next_ref, pos_ref, gate_ref, x1_ref, gf_ref, y_hbm, yp_ref, ys_ref,
                    ybuf, sem):
    sb = ybuf.shape[1] // SUB
    i = pl.program_id(0)
    n = pl.num_programs(0)
    cur = lax.rem(i, 2)

    def run_copy(ref, buf, r):
        src = y_hbm.at[pl.ds(pl.multiple_of(ref[0, 0, MAX_RUNS + r] * SUB, SUB), RUN * SUB)]
        dst = ybuf.at[buf, pl.ds(pl.multiple_of(ref[0, 0, r] * SUB, SUB), RUN * SUB)]
        return pltpu.make_async_copy(src, dst, sem.at[buf])

    def issue(ref, buf):
        def start(r, carry):
            run_copy(ref, buf, r).start()
            return carry
        lax.fori_loop(0, ref[0, 0, 2 * MAX_RUNS], start, 0)

    @pl.when(i == 0)
    def _():
        ybuf[...] = jnp.zeros_like(ybuf)
        issue(runs_ref, 0)

    @pl.when(i + 1 < n)
    def _():
        issue(next_ref, 1 - cur)

    def wait(r, carry):
        run_copy(runs_ref, cur, r).wait()
        return carry
    lax.fori_loop(0, runs_ref[0, 0, 2 * MAX_RUNS], wait, 0)

    gate = gate_ref[...]
    pos = pos_ref[...]
    col = lax.broadcasted_iota(jnp.int32, (gate.shape[0], sb), 1)
    weight = jnp.zeros(col.shape, F32)
    for k in range(TOP_K):
        weight = jnp.where(col == pos[:, k:k + 1], gate[:, k:k + 1], weight)
    ysorted = _load_token_tiled(ybuf.at[cur], sb).astype(BF16)
    y = jnp.dot(weight.astype(BF16), ysorted, preferred_element_type=F32)
    out = _rms(x1_ref[...] + y, gf_ref[...])

    @pl.when(i < npb)
    def _():
        yp_ref[...] = out

    @pl.when(i >= npb)
    def _():
        ys_ref[...] = out


def _combine(y_slots, runs, pos_tk, gate, x1, gf, npt):
    t, d = x1.shape
    tm = DISPATCH_TILE
    nt = t // tm
    npb = npt // tm
    sb = tm * TOP_K + N_EXPERTS * RUN
    smem_blk = lambda f: pl.BlockSpec((1, 1, runs.shape[2]), f, memory_space=pltpu.SMEM)
    return pl.pallas_call(
        functools.partial(_combine_kernel, npb),
        grid=(nt,),
        in_specs=[
            smem_blk(lambda i: (i, 0, 0)),
            smem_blk(lambda i: (jnp.minimum(i + 1, nt - 1), 0, 0)),
            pl.BlockSpec((tm, TOP_K), lambda i: (i, 0)),
            pl.BlockSpec((tm, TOP_K), lambda i: (i, 0)),
            pl.BlockSpec((tm, d), lambda i: (i, 0)),
            pl.BlockSpec((1, d), lambda i: (0, 0)),
            pl.BlockSpec(memory_space=pl.ANY),
        ],
        out_specs=[
            pl.BlockSpec((tm, d), lambda i: (jnp.minimum(i, npb - 1), 0)),
            pl.BlockSpec((tm, d), lambda i: (jnp.maximum(i - npb, 0), 0)),
        ],
        out_shape=[jax.ShapeDtypeStruct((npt, d), F32), jax.ShapeDtypeStruct((t - npt, d), F32)],
        scratch_shapes=[pltpu.VMEM((2, sb * SUB, LANE), F32), pltpu.SemaphoreType.DMA((2,))],
        compiler_params=_params("arbitrary"),
        name="combine_norm",
    )(runs, runs, pos_tk, gate, x1, gf, y_slots)


def _inproj_columns(d):
    da_q, db = HA_Q * HEAD_DIM, HB * HEAD_DIM
    da_kv = HA_KV * HEAD_DIM
    q_b_end = da_q + db
    g_b_end = q_b_end + 2 * d
    ka0 = g_b_end
    va0 = ka0 + da_kv
    kb0 = va0 + da_kv
    vb0 = kb0 + db
    dup = lambda c0: np.concatenate(
        [np.tile(np.arange(c0 + h * HEAD_DIM, c0 + (h + 1) * HEAD_DIM), 2) for h in range(HA_KV)])
    cols = np.concatenate([np.arange(0, g_b_end), dup(ka0), dup(va0), np.arange(kb0, vb0 + db)])
    segs = ((0, da_q), (da_q, db), (q_b_end, 2 * d), (g_b_end, 2 * da_kv),
            (g_b_end + 2 * da_kv, 2 * da_kv), (g_b_end + 4 * da_kv, db), (g_b_end + 4 * da_kv + db, db))
    return cols, segs


def kernel(x_prompt, x_sample, meta_tokens, norm1_g, w_in, attn_sink, rel_pos_bias, w_branch_a,
           w_branch_b, w_out, norm2_g, router_w, router_b, w_gate_up, b_gate_up, w_down, b_down,
           final_norm_g):
    assert norm1_g.shape[0] == 1, "single-layer trunk"
    bp, lp, d = x_prompt.shape
    bs, ls, _ = x_sample.shape
    npt, nst = bp * lp, bs * ls
    assert npt % ROW_TILE == 0 and nst % ROW_TILE == 0
    xp = x_prompt.reshape(npt, d)
    xs = x_sample.reshape(nst, d)

    cols, segs = _inproj_columns(d)
    w_p = w_in[0][:, cols].astype(BF16)
    g1 = norm1_g[0].reshape(1, d)
    qa, qb, gates, ka2, va2, kb, vb = _inproj(xp, xs, g1, w_p, segs)
    kv0 = segs[3][0]
    meta_kv = _meta_proj(meta_tokens, g1, w_p[:, kv0:])
    w_kv = 2 * HA_KV * HEAD_DIM
    meta_kv = _pad_meta(meta_kv)
    km2, vm2 = meta_kv[:, :w_kv], meta_kv[:, w_kv:2 * w_kv]
    kbm, vbm = meta_kv[:, 2 * w_kv:2 * w_kv + HB * HEAD_DIM], meta_kv[:, 2 * w_kv + HB * HEAD_DIM:]

    oa = _window_attention(qa, ka2, va2, km2, vm2, attn_sink[0].astype(F32), npt, lp, ls)
    ob = _nbr_attention(qb, kb, vb, kbm, vbm, rel_pos_bias[0], npt, lp, ls)

    rw = router_w[0].astype(F32)
    rw_top = _bf16_part(rw)
    rw_hi = rw_top.astype(BF16)
    rw_lo = (rw - rw_top).astype(BF16)
    lane_pad = lambda a, fill: jnp.pad(a, ((0, 0), (0, LANE - a.shape[1])), constant_values=fill)
    rw2 = jnp.concatenate([lane_pad(rw_hi, 0), lane_pad(rw_lo, 0)], axis=1)
    rb = lane_pad(router_b[0].reshape(1, -1).astype(F32), -jnp.inf)
    x1, h2, top_idx, gate, rank, counts, cstart = _merge(
        oa, ob, gates, xp, xs, w_branch_a[0].astype(BF16), w_branch_b[0].astype(BF16),
        w_out[0].astype(BF16), norm2_g[0].reshape(1, d), rw2, rb)

    pos_tk, pos, runs, pad_slots, block_e, n_used, n_blocks = _route(top_idx, rank, counts, cstart, FFN_BM)
    x_slots = _dispatch(h2, pos, runs, pad_slots, (n_blocks + N_EXPERTS) * FFN_BM)
    bgu = b_gate_up[0]
    y_slots = _expert_ffn(
        x_slots, block_e, n_used, _wprep(w_gate_up[0]), w_down[0].astype(BF16),
        bgu[:, None, 0::2], bgu[:, None, 1::2], b_down[0][:, None, :])

    yp, ys = _combine(y_slots, runs, pos_tk, gate, x1, final_norm_g.reshape(1, d), npt)
    return yp.reshape(bp, lp, d), ys.reshape(bs, ls, d)
```

```python
import functools

import numpy as np
import jax
import jax.numpy as jnp
from jax import lax
from jax.experimental import pallas as pl
from jax.experimental.pallas import tpu as pltpu

F32 = jnp.float32
BF16 = jnp.bfloat16

N_META = 16
GRID_W = 64
HA_Q = 8
HA_KV = 2
HEAD_DIM = 64
WINDOW = 128
HB = 8
NB_ROWS = 8
NB_COLS = 16
N_EXPERTS = 32
TOP_K = 4
SWIGLU_ALPHA = 1.702
SWIGLU_LIMIT = 7.0
NORM_EPS = 1e-5
NEG_INF = -1e30
SCALE = HEAD_DIM ** -0.5

LANE = 128
SUB = 8
ROW_TILE = 512
WIN_TQ = 256
NBR_ROWS = 4
FFN_BM = 512
DMA_UNROLL = 8
DISPATCH_TILE = ROW_TILE // 2
RUN = 8
MAX_RUNS = DISPATCH_TILE * TOP_K // RUN + N_EXPERTS
VMEM_LIMIT = 56 * 1024 * 1024


def _rms(x, g):
    return x * lax.rsqrt(jnp.mean(x * x, axis=-1, keepdims=True) + NORM_EPS) * g


def _bf16_part(x):
    bits = lax.bitcast_convert_type(x, jnp.uint32) & jnp.uint32(0xFFFF0000)
    return lax.bitcast_convert_type(bits, F32)


def _sigmoid(z):
    return 1.0 / (1.0 + jnp.exp(-z))


def _params(*sem):
    return pltpu.CompilerParams(dimension_semantics=sem, vmem_limit_bytes=VMEM_LIMIT)


def _inproj_kernel(npb, segs, xp_ref, xs_ref, g_ref, w_ref, *o_refs):
    i = pl.program_id(0)
    x = jnp.where(i < npb, xp_ref[...], xs_ref[...])
    h = _rms(x, g_ref[...]).astype(BF16)
    for (c0, width), o_ref in zip(segs, o_refs):
        for off in range(0, width, 512):
            n = min(512, width - off)
            o_ref[:, off:off + n] = jnp.dot(
                h, w_ref[:, c0 + off:c0 + off + n], preferred_element_type=F32).astype(BF16)


def _inproj(xp, xs, g, w, segs):
    npt, d = xp.shape
    t = npt + xs.shape[0]
    tm = ROW_TILE
    npb = npt // tm
    return pl.pallas_call(
        functools.partial(_inproj_kernel, npb, segs),
        grid=(t // tm,),
        in_specs=[
            pl.BlockSpec((tm, d), lambda i: (jnp.minimum(i, npb - 1), 0)),
            pl.BlockSpec((tm, d), lambda i: (jnp.maximum(i - npb, 0), 0)),
            pl.BlockSpec((1, d), lambda i: (0, 0)),
            pl.BlockSpec(w.shape, lambda i: (0, 0)),
        ],
        out_specs=[pl.BlockSpec((tm, width), lambda i: (i, 0)) for _, width in segs],
        out_shape=[jax.ShapeDtypeStruct((t, width), BF16) for _, width in segs],
        compiler_params=_params("arbitrary"),
        name="inproj",
    )(xp, xs, g, w)


def _meta_kernel(m_ref, g_ref, w_ref, o_ref):
    h = _rms(m_ref[...], g_ref[...]).astype(BF16)
    o_ref[...] = jnp.dot(h, w_ref[...], preferred_element_type=F32).astype(BF16)


def _meta_proj(meta, g, w):
    return pl.pallas_call(
        _meta_kernel,
        out_shape=jax.ShapeDtypeStruct((meta.shape[0], w.shape[1]), BF16),
        name="meta_proj",
    )(meta, g, w)


def _window_bias(tq):
    slopes = 2.0 ** (-8.0 * np.arange(1, HA_Q + 1) / HA_Q)
    q = np.arange(tq)[:, None]
    col = np.arange(tq + 2 * WINDOW)[None, :]
    dist = np.abs(col - WINDOW - q)
    band = np.where((dist <= WINDOW)[None], -slopes[:, None, None] * dist[None], NEG_INF)
    full = np.concatenate([band, np.broadcast_to(_meta_cols(), (HA_Q, tq, LANE))], axis=-1)
    return jnp.asarray(full.reshape(HA_Q // 2, 2 * tq, -1), F32)


def _meta_cols():
    return np.where(np.arange(LANE) < N_META, 0.0, NEG_INF)


def _pad_meta(a):
    return jnp.pad(a, ((0, LANE - a.shape[0]), (0, 0)))


def _seq_pos(start, npt, lp, ls):
    is_p = start < npt
    pos = jnp.where(is_p, lax.rem(start, lp), lax.rem(jnp.maximum(start - npt, 0), ls))
    return pos, jnp.where(is_p, lp, ls)


def _window_kernel(npt, lp, ls, sink_ref, qa_ref, kp_ref, kc_ref, kn_ref, vp_ref, vc_ref, vn_ref,
                   km_ref, vm_ref, bias_ref, o_ref):
    tq = qa_ref.shape[0]
    nk = tq + 2 * WINDOW
    pos, seq_len = _seq_pos(pl.program_id(0) * tq, npt, lp, ls)
    pen_prev = jnp.where(pos == 0, NEG_INF, 0.0).astype(F32)
    pen_next = jnp.where(pos + tq == seq_len, NEG_INF, 0.0).astype(F32)
    col = lax.broadcasted_iota(jnp.int32, (1, nk + LANE), 1)
    pen = (jnp.where(col < WINDOW, pen_prev, 0.0)
           + jnp.where((col >= WINDOW + tq) & (col < nk), pen_next, 0.0))
    lo = lax.broadcasted_iota(jnp.int32, (1, LANE), 1) < HEAD_DIM
    top = lax.broadcasted_iota(jnp.int32, (2 * tq, 1), 0) < tq
    for i in range(HA_Q // 2):
        hk = (2 * i) // (HA_Q // HA_KV)
        sl = slice(i * LANE, (i + 1) * LANE)
        ksl = slice(hk * LANE, (hk + 1) * LANE)
        q = qa_ref[:, sl] * jnp.asarray(SCALE, BF16)
        zero = jnp.zeros_like(q)
        qq = jnp.concatenate([jnp.where(lo, q, zero), jnp.where(lo, zero, q)], axis=0)
        k_all = jnp.concatenate([kp_ref[:, ksl], kc_ref[:, ksl], kn_ref[:, ksl], km_ref[:, ksl]], axis=0)
        v_all = jnp.concatenate([vp_ref[:, ksl], vc_ref[:, ksl], vn_ref[:, ksl], vm_ref[:, ksl]], axis=0)
        s = lax.dot_general(qq, k_all, (((1,), (1,)), ((), ())), preferred_element_type=F32)
        s = s + bias_ref[i] + pen
        sink = jnp.where(top, sink_ref[2 * i], sink_ref[2 * i + 1])
        m = jnp.maximum(jnp.max(s, axis=-1, keepdims=True), sink)
        p = jnp.exp(s - m)
        denom = jnp.sum(p, axis=-1, keepdims=True) + jnp.exp(sink - m)
        o = jnp.dot(p.astype(BF16), v_all, preferred_element_type=F32) / denom
        o_ref[:, sl] = jnp.where(lo, o[:tq], o[tq:]).astype(BF16)


def _window_attention(qa, ka2, va2, km2, vm2, sink, npt, lp, ls):
    t = qa.shape[0]
    tq = WIN_TQ
    assert lp % tq == 0 and ls % tq == 0 and tq == 2 * WINDOW
    nhalf = t // WINDOW
    bias = _window_bias(tq)
    kw = ka2.shape[1]
    prev = pl.BlockSpec((WINDOW, kw), lambda j: (jnp.maximum(2 * j - 1, 0), 0))
    cur = pl.BlockSpec((tq, kw), lambda j: (j, 0))
    nxt = pl.BlockSpec((WINDOW, kw), lambda j: (jnp.minimum(2 * j + 2, nhalf - 1), 0))
    whole = lambda a: pl.BlockSpec(a.shape, lambda j: (0,) * a.ndim)
    return pl.pallas_call(
        functools.partial(_window_kernel, npt, lp, ls),
        grid=(t // tq,),
        in_specs=[
            pl.BlockSpec(memory_space=pltpu.SMEM),
            pl.BlockSpec((tq, qa.shape[1]), lambda j: (j, 0)),
            prev, cur, nxt, prev, cur, nxt,
            whole(km2), whole(vm2), whole(bias),
        ],
        out_specs=pl.BlockSpec((tq, qa.shape[1]), lambda j: (j, 0)),
        out_shape=jax.ShapeDtypeStruct(qa.shape, BF16),
        compiler_params=_params("arbitrary"),
        name="window_attn",
    )(sink, qa, ka2, ka2, ka2, va2, va2, va2, km2, vm2, bias)


def _nbr_tables(rpb):
    r, w = NBR_ROWS, GRID_W
    qc = np.arange(w)[:, None]
    kcol = np.arange(w)[None, :]
    col_start = np.clip(qc - NB_COLS // 2, 0, w - NB_COLS)
    col_ok = (kcol >= col_start) & (kcol < col_start + NB_COLS)
    sel = np.zeros((2 * NB_COLS - 1, w, w), np.float32)
    ci, ki = np.nonzero(col_ok)
    sel[ki - ci + NB_COLS - 1, ci, ki] = 1.0
    per_row = jnp.einsum("hrj,jck->hrck", rpb.astype(F32), jnp.asarray(sel), precision=lax.Precision.HIGHEST)
    per_row = jnp.where(jnp.asarray(col_ok), per_row, NEG_INF)
    off = NB_ROWS - 1 - NBR_ROWS
    vals = jnp.concatenate(
        [jnp.concatenate([per_row[:, b - a + off] for b in range(3 * r)], axis=-1) for a in range(r)], axis=1)
    meta = jnp.broadcast_to(jnp.asarray(_meta_cols(), F32), (HB, r * w, LANE))
    bias = jnp.concatenate([vals, meta], axis=-1)
    a = np.repeat(np.arange(r), w)[:, None]
    b = np.repeat(np.arange(3 * r), w)[None, :]
    first = NBR_ROWS
    ok = np.stack([
        (b >= first) & (b < first + NB_ROWS) & (a >= 0),
        (b >= a) & (b < a + NB_ROWS),
        (b >= 0) & (b < NB_ROWS) & (a >= 0),
    ])
    ok = np.concatenate([ok, np.ones((3, r * w, LANE), bool)], axis=-1)
    return bias, jnp.asarray(np.where(ok, 0.0, NEG_INF), F32)


def _nbr_kernel(npg, gp, gs, qb_ref, kp_ref, kc_ref, kn_ref, vp_ref, vc_ref, vn_ref, km_ref, vm_ref,
                bias_ref, mask_ref, o_ref):
    tq = qb_ref.shape[0]
    g = pl.program_id(0)
    is_p = g < npg
    per_seq = jnp.where(is_p, gp, gs)
    r = jnp.where(is_p, lax.rem(g, gp), lax.rem(jnp.maximum(g - npg, 0), gs))
    pat = jnp.where(r == 0, 0, jnp.where(r == per_seq - 1, 2, 1))
    row_mask = mask_ref[pat]
    lo = lax.broadcasted_iota(jnp.int32, (1, LANE), 1) < HEAD_DIM
    for i in range(HB // 2):
        sl = slice(i * LANE, (i + 1) * LANE)
        q = qb_ref[:, sl] * jnp.asarray(SCALE, BF16)
        zero = jnp.zeros_like(q)
        qq = jnp.concatenate([jnp.where(lo, q, zero), jnp.where(lo, zero, q)], axis=0)
        k_all = jnp.concatenate([kp_ref[:, sl], kc_ref[:, sl], kn_ref[:, sl], km_ref[:, sl]], axis=0)
        v_all = jnp.concatenate([vp_ref[:, sl], vc_ref[:, sl], vn_ref[:, sl], vm_ref[:, sl]], axis=0)
        s = lax.dot_general(qq, k_all, (((1,), (1,)), ((), ())), preferred_element_type=F32)
        halves = []
        for hf in range(2):
            sh = s[hf * tq:(hf + 1) * tq] + bias_ref[2 * i + hf] + row_mask
            m = jnp.max(sh, axis=-1, keepdims=True)
            p = jnp.exp(sh - m)
            denom = jnp.sum(p, axis=-1, keepdims=True)
            halves.append(jnp.dot(p.astype(BF16), v_all, preferred_element_type=F32) / denom)
        o_ref[:, sl] = jnp.where(lo, halves[0], halves[1]).astype(BF16)


def _nbr_attention(qb, kb, vb, kbm, vbm, rpb, npt, lp, ls):
    t, width = qb.shape
    tq = NBR_ROWS * GRID_W
    rows_p, rows_s = lp // GRID_W, ls // GRID_W
    assert rows_p % NBR_ROWS == 0 and rows_s % NBR_ROWS == 0 and min(rows_p, rows_s) >= NB_ROWS
    assert NB_ROWS == 2 * NBR_ROWS
    ng = t // tq
    bias, mask = _nbr_tables(rpb)
    prev = pl.BlockSpec((tq, width), lambda g: (jnp.maximum(g - 1, 0), 0))
    cur = pl.BlockSpec((tq, width), lambda g: (g, 0))
    nxt = pl.BlockSpec((tq, width), lambda g: (jnp.minimum(g + 1, ng - 1), 0))
    whole = lambda a: pl.BlockSpec(a.shape, lambda g: (0,) * a.ndim)
    return pl.pallas_call(
        functools.partial(_nbr_kernel, npt // tq, rows_p // NBR_ROWS, rows_s // NBR_ROWS),
        grid=(ng,),
        in_specs=[cur, prev, cur, nxt, prev, cur, nxt, whole(kbm), whole(vbm), whole(bias), whole(mask)],
        out_specs=cur,
        out_shape=jax.ShapeDtypeStruct(qb.shape, BF16),
        compiler_params=_params("arbitrary"),
        name="nbr_attn",
    )(qb, kb, kb, kb, vb, vb, vb, kbm, vbm, bias, mask)


def _merge_kernel(npb, oa_ref, ob_ref, g_ref, xp_ref, xs_ref, wa_ref, wb_ref, wo_ref, g2_ref,
                  rw_ref, rb_ref, tri_ref, x1_ref, h2_ref, idx_ref, gate_ref, rank_ref, cnt_ref, cs_ref,
                  carry):
    i = pl.program_id(0)
    d = x1_ref.shape[1]
    x = jnp.where(i < npb, xp_ref[...], xs_ref[...])
    a = jnp.dot(oa_ref[...], wa_ref[...], preferred_element_type=F32)
    b = jnp.dot(ob_ref[...], wb_ref[...], preferred_element_type=F32)
    mix = _sigmoid(g_ref[:, :d].astype(F32)) * a + _sigmoid(g_ref[:, d:].astype(F32)) * b
    x1 = x + jnp.dot(mix.astype(BF16), wo_ref[...], preferred_element_type=F32)
    x1_ref[...] = x1
    h2 = _rms(x1, g2_ref[...])
    h2_ref[...] = h2.astype(BF16)
    h_top = _bf16_part(h2)
    h_hi = h_top.astype(BF16)
    h_lo = (h2 - h_top).astype(BF16)
    r_hi = jnp.dot(h_hi, rw_ref[...], preferred_element_type=F32)
    r_lo = jnp.dot(h_lo, rw_ref[:, :LANE], preferred_element_type=F32)
    logits = r_hi[:, :LANE] + r_hi[:, LANE:] + r_lo + rb_ref[...]

    lane = lax.broadcasted_iota(jnp.int32, logits.shape, 1).astype(F32)
    work = logits
    member = jnp.zeros(logits.shape, F32)
    vals, idxs = [], []
    for _ in range(TOP_K):
        m = jnp.max(work, axis=-1, keepdims=True)
        idx = jnp.min(jnp.where(work == m, lane, float(LANE)), axis=-1, keepdims=True)
        hit = lane == idx
        work = jnp.where(hit, -jnp.inf, work)
        member = jnp.where(hit, 1.0, member)
        vals.append(m)
        idxs.append(idx)
    e = [jnp.exp(v - vals[0]) for v in vals]
    denom = functools.reduce(lambda p, q: p + q, e)
    gate_ref[...] = jnp.concatenate(e, axis=1) / denom
    idx_ref[...] = jnp.concatenate(idxs, axis=1).astype(jnp.int32)

    @pl.when(i == 0)
    def _():
        carry[...] = jnp.zeros_like(carry)

    before = jnp.dot(tri_ref[...], member.astype(BF16), preferred_element_type=F32) + carry[...]
    ranks = [jnp.sum(jnp.where(lane == idx, before, 0.0), axis=-1, keepdims=True) for idx in idxs]
    rank_ref[...] = jnp.concatenate(ranks, axis=1).astype(jnp.int32)
    half = member.shape[0] // 2
    c0 = carry[...]
    c1 = c0 + jnp.sum(member[:half], axis=0, keepdims=True)
    cs_ref[0] = jnp.concatenate([c0, c1], axis=0).astype(jnp.int32)
    carry[...] = c1 + jnp.sum(member[half:], axis=0, keepdims=True)
    cnt_ref[...] = carry[...].astype(jnp.int32)


def _merge(oa, ob, gates, xp, xs, wa, wb, wo, g2, rw, rb):
    npt, d = xp.shape
    t = npt + xs.shape[0]
    tm = ROW_TILE
    npb = npt // tm
    tri = jnp.asarray(np.tril(np.ones((tm, tm)), -1), BF16)
    row = lambda width: pl.BlockSpec((tm, width), lambda i: (i, 0))
    whole = lambda a: pl.BlockSpec(a.shape, lambda i: (0,) * a.ndim)
    return pl.pallas_call(
        functools.partial(_merge_kernel, npb),
        grid=(t // tm,),
        in_specs=[
            row(oa.shape[1]), row(ob.shape[1]), row(gates.shape[1]),
            pl.BlockSpec((tm, d), lambda i: (jnp.minimum(i, npb - 1), 0)),
            pl.BlockSpec((tm, d), lambda i: (jnp.maximum(i - npb, 0), 0)),
            whole(wa), whole(wb), whole(wo), whole(g2), whole(rw), whole(rb), whole(tri),
        ],
        out_specs=[row(d), row(d), row(TOP_K), row(TOP_K), row(TOP_K),
                   pl.BlockSpec((1, LANE), lambda i: (0, 0)),
                   pl.BlockSpec((1, 2, LANE), lambda i: (i, 0, 0))],
        out_shape=[jax.ShapeDtypeStruct((t, d), F32), jax.ShapeDtypeStruct((t, d), BF16),
                   jax.ShapeDtypeStruct((t, TOP_K), jnp.int32), jax.ShapeDtypeStruct((t, TOP_K), F32),
                   jax.ShapeDtypeStruct((t, TOP_K), jnp.int32),
                   jax.ShapeDtypeStruct((1, LANE), jnp.int32),
                   jax.ShapeDtypeStruct((t // tm, 2, LANE), jnp.int32)],
        scratch_shapes=[pltpu.VMEM((1, LANE), F32)],
        compiler_params=_params("arbitrary"),
        name="merge_router",
    )(oa, ob, gates, xp, xs, wa, wb, wo, g2, rw, rb, tri)


def _route(top_idx, rank, counts, cstart, bm):
    t = top_idx.shape[0]
    ne = N_EXPERTS
    dt = DISPATCH_TILE
    n_d = t // dt
    counts = counts.reshape(-1)[:ne]
    padded = (counts + RUN - 1 + bm - 1) // bm * bm
    pad_end = jnp.cumsum(padded)
    pad_start = pad_end - padded
    n_blocks = -(-(t * TOP_K + ne * (RUN - 1)) // bm) + ne
    experts = jnp.arange(ne, dtype=jnp.int32)

    cs = cstart.reshape(n_d, LANE)[:, :ne]
    cnt = jnp.concatenate([cs[1:], counts[None]], axis=0) - cs
    c_run = (cnt + RUN - 1) // RUN * RUN
    off = jnp.cumsum(c_run, axis=1) - c_run
    onehot = (top_idx[..., None] == experts).reshape(n_d, dt, TOP_K, ne)
    per_tok = lambda tab: jnp.sum(jnp.where(onehot, tab[:, None, None, :], 0), axis=-1)
    rank_t = rank.reshape(n_d, dt, TOP_K)
    pos_tk = (per_tok(off) + rank_t - per_tok(cs)).astype(jnp.int32)
    pos = pos_tk.transpose(0, 2, 1)
    dst0 = pad_start[None, :] + cs

    n_run = c_run // RUN
    cum = jnp.cumsum(n_run, axis=1)
    total = cum[:, -1:]
    i = jnp.minimum(jnp.arange(MAX_RUNS, dtype=jnp.int32)[None, :], total - 1)
    e_of = jnp.minimum(jnp.sum(i[:, :, None] >= cum[:, None, :], axis=-1), ne - 1)
    pick = lambda tab: jnp.sum(jnp.where(e_of[..., None] == experts, tab[:, None, :], 0), axis=-1)
    local = RUN * (i - pick(cum - n_run))
    runs = jnp.concatenate([pick(off) + local, pick(dst0) + local, (total + 1) // 2], axis=1).astype(jnp.int32)

    j = jnp.arange(bm, dtype=jnp.int32)[None, :]
    fill = (pad_start + counts)[:, None] + j
    spare = n_blocks * bm + experts[:, None] * bm + j
    pad_slots = jnp.where(fill < pad_end[:, None], fill, spare).reshape(-1).astype(jnp.int32)
    starts = jnp.arange(n_blocks, dtype=jnp.int32) * bm
    block_e = jnp.minimum(jnp.sum(starts[:, None] >= pad_end[None, :], axis=-1), ne - 1)
    n_used = (pad_end[-1] // bm).astype(jnp.int32).reshape(1)
    return (pos_tk.reshape(t, TOP_K), pos, runs.reshape(n_d, 1, -1), pad_slots, block_e.astype(jnp.int32),
            n_used, n_blocks)


def _dispatch_kernel(runs_ref, prev_ref, pad_ref, pos_ref, h2_ref, x_hbm, sbuf, sem):
    sb = sbuf.shape[1] // SUB
    dt = h2_ref.shape[0]
    i = pl.program_id(0)
    cur = lax.rem(i, 2)
    rows = lax.broadcasted_iota(jnp.int32, (sb, dt), 0)
    hit = rows == pos_ref[0, 0:1, :]
    for k in range(1, TOP_K):
        hit = hit | (rows == pos_ref[0, k:k + 1, :])
    onehot = jnp.where(hit, 1.0, 0.0).astype(BF16)
    _store_token_tiled(sbuf.at[cur], jnp.dot(onehot, h2_ref[...], preferred_element_type=F32))

    def run_copy(ref, buf, r):
        src = sbuf.at[buf, pl.ds(pl.multiple_of(ref[0, 0, r] * SUB, SUB), RUN * SUB)]
        dst = x_hbm.at[pl.ds(pl.multiple_of(ref[0, 0, MAX_RUNS + r] * SUB, SUB), RUN * SUB)]
        return pltpu.make_async_copy(src, dst, sem)

    def drain(ref, buf):
        def wait(p, carry):
            for q in range(2):
                run_copy(ref, buf, 2 * p + q).wait()
            return carry
        lax.fori_loop(0, ref[0, 0, 2 * MAX_RUNS], wait, 0)

    @pl.when(i > 0)
    def _():
        drain(prev_ref, 1 - cur)

    def start(p, carry):
        for q in range(2):
            run_copy(runs_ref, cur, 2 * p + q).start(priority=q)
        return carry
    lax.fori_loop(0, runs_ref[0, 0, 2 * MAX_RUNS], start, 0)

    @pl.when(i == 0)
    def _():
        n_pad = pad_ref.shape[0]

        def pad_copy(j):
            return pltpu.make_async_copy(sbuf.at[0, _tile_rows(0)], x_hbm.at[_tile_rows(pad_ref[j])], sem)

        def pad_start(g, carry):
            for u in range(DMA_UNROLL):
                pad_copy(g * DMA_UNROLL + u).start(priority=u % 2)
            return carry
        lax.fori_loop(0, n_pad // DMA_UNROLL, pad_start, 0)

        def pad_wait(j, carry):
            pad_copy(j).wait()
            return carry
        lax.fori_loop(0, n_pad, pad_wait, 0)

    @pl.when(i == pl.num_programs(0) - 1)
    def _():
        drain(runs_ref, cur)


def _dispatch(h2, pos, runs, pad_slots, n_slots):
    t, d = h2.shape
    dt = DISPATCH_TILE
    sb = dt * TOP_K + N_EXPERTS * RUN
    return pl.pallas_call(
        _dispatch_kernel,
        grid=(t // dt,),
        in_specs=[
            pl.BlockSpec((1, 1, runs.shape[2]), lambda i: (i, 0, 0), memory_space=pltpu.SMEM),
            pl.BlockSpec((1, 1, runs.shape[2]), lambda i: (jnp.maximum(i - 1, 0), 0, 0),
                         memory_space=pltpu.SMEM),
            pl.BlockSpec(memory_space=pltpu.SMEM),
            pl.BlockSpec((1, TOP_K, dt), lambda i: (i, 0, 0)),
            pl.BlockSpec((dt, d), lambda i: (i, 0)),
        ],
        out_specs=pl.BlockSpec(memory_space=pl.ANY),
        out_shape=jax.ShapeDtypeStruct((n_slots * SUB, LANE), F32),
        scratch_shapes=[pltpu.VMEM((2, sb * SUB, LANE), F32), pltpu.SemaphoreType.DMA(())],
        compiler_params=_params("arbitrary"),
        name="dispatch",
    )(runs, runs, pad_slots, pos, h2)


def _wprep_kernel(w_ref, eo_ref, o_ref):
    half = o_ref.shape[2] // 2
    two = 2 * LANE
    for c in range(w_ref.shape[2] // two):
        blk = w_ref[0, :, c * two:(c + 1) * two].astype(BF16)
        r = jnp.dot(blk, eo_ref[...], preferred_element_type=F32)
        o_ref[0, :, c * LANE:(c + 1) * LANE] = r[:, :LANE].astype(BF16)
        o_ref[0, :, half + c * LANE:half + (c + 1) * LANE] = r[:, LANE:].astype(BF16)


def _wprep(w):
    ne, d, f2 = w.shape
    eo = np.zeros((2 * LANE, 2 * LANE))
    eo[2 * np.arange(LANE), np.arange(LANE)] = 1.0
    eo[2 * np.arange(LANE) + 1, LANE + np.arange(LANE)] = 1.0
    return pl.pallas_call(
        _wprep_kernel,
        grid=(ne,),
        in_specs=[pl.BlockSpec((1, d, f2), lambda e: (e, 0, 0)),
                  pl.BlockSpec((2 * LANE, 2 * LANE), lambda e: (0, 0))],
        out_specs=pl.BlockSpec((1, d, f2), lambda e: (e, 0, 0)),
        out_shape=jax.ShapeDtypeStruct(w.shape, BF16),
        compiler_params=_params("arbitrary"),
        name="expert_weight_layout",
    )(w, jnp.asarray(eo, BF16))


def _tile_rows(r):
    return pl.ds(pl.multiple_of(r * SUB, SUB), SUB)


def _store_token_tiled(ref, val):
    n = val.shape[0]
    for c in range(SUB):
        ref[pl.ds(c, n, stride=SUB), :] = val[:, c * LANE:(c + 1) * LANE]


def _load_token_tiled(ref, n):
    return jnp.concatenate([ref[pl.ds(c, n, stride=SUB), :] for c in range(SUB)], axis=1)


def _ffn_kernel(be_ref, nu_ref, x_ref, wg_ref, wu_ref, wd_ref, bg_ref, bu_ref, bd_ref, y_ref):
    bm = y_ref.shape[0] // SUB
    b = pl.program_id(0)
    n_used = nu_ref[0]

    @pl.when(b < n_used)
    def _():
        x = _load_token_tiled(x_ref, bm).astype(BF16)
        hg = jnp.dot(x, wg_ref[0], preferred_element_type=F32) + bg_ref[0]
        hu = jnp.dot(x, wu_ref[0], preferred_element_type=F32) + bu_ref[0]
        glu = jnp.minimum(hg, SWIGLU_LIMIT)
        lin = jnp.clip(hu, -SWIGLU_LIMIT, SWIGLU_LIMIT)
        act = glu * _sigmoid(SWIGLU_ALPHA * glu) * (lin + 1.0)
        y = jnp.dot(act.astype(BF16), wd_ref[0], preferred_element_type=F32) + bd_ref[0]
        _store_token_tiled(y_ref, y)

    @pl.when(b >= n_used)
    def _():
        y_ref[...] = jnp.zeros_like(y_ref)


def _expert_ffn(x_slots, block_e, n_used, wgu, wd, bg, bu, bd):
    bm = FFN_BM
    d, f = wd.shape[2], wd.shape[1]
    assert d == SUB * LANE
    nb = block_e.shape[0]
    wspec = lambda a: pl.BlockSpec((1,) + a.shape[1:], lambda b, be, nu: (be[b], 0, 0))
    wg_spec = pl.BlockSpec((1, d, f), lambda b, be, nu: (be[b], 0, 0))
    wu_spec = pl.BlockSpec((1, d, f), lambda b, be, nu: (be[b], 0, 1))
    rows = pl.BlockSpec((bm * SUB, LANE), lambda b, be, nu: (b, 0))
    return pl.pallas_call(
        _ffn_kernel,
        grid_spec=pltpu.PrefetchScalarGridSpec(
            num_scalar_prefetch=2,
            grid=(nb,),
            in_specs=[rows, wg_spec, wu_spec, wspec(wd), wspec(bg), wspec(bu), wspec(bd)],
            out_specs=rows,
        ),
        out_shape=jax.ShapeDtypeStruct((nb * bm * SUB, LANE), F32),
        compiler_params=_params("arbitrary"),
        name="expert_ffn",
    )(block_e, n_used, x_slots, wgu, wgu, wd, bg, bu, bd)


def _combine_kernel(npb, runs_ref, next_ref, pos_ref, gate_ref, x1_ref, gf_ref, y_hbm, yp_ref, ys_ref,
                    ybuf, sem):
    sb = ybuf.shape[1] // SUB
    i = pl.program_id(0)
    n = pl.num_programs(0)
    cur = lax.rem(i, 2)

    def run_copy(ref, buf, r):
        src = y_hbm.at[pl.ds(pl.multiple_of(ref[0, 0, MAX_RUNS + r] * SUB, SUB), RUN * SUB)]
        dst = ybuf.at[buf, pl.ds(pl.multiple_of(ref[0, 0, r] * SUB, SUB), RUN * SUB)]
        return pltpu.make_async_copy(src, dst, sem.at[buf])

    def issue(ref, buf):
        def start(p, carry):
            for q in range(2):
                run_copy(ref, buf, 2 * p + q).start(priority=q)
            return carry
        lax.fori_loop(0, ref[0, 0, 2 * MAX_RUNS], start, 0)

    @pl.when(i == 0)
    def _():
        ybuf[...] = jnp.zeros_like(ybuf)
        issue(runs_ref, 0)

    @pl.when(i + 1 < n)
    def _():
        issue(next_ref, 1 - cur)

    def wait(p, carry):
        for q in range(2):
            run_copy(runs_ref, cur, 2 * p + q).wait()
        return carry
    lax.fori_loop(0, runs_ref[0, 0, 2 * MAX_RUNS], wait, 0)

    gate = gate_ref[...]
    pos = pos_ref[...]
    col = lax.broadcasted_iota(jnp.int32, (gate.shape[0], sb), 1)
    weight = jnp.zeros(col.shape, F32)
    for k in range(TOP_K):
        weight = jnp.where(col == pos[:, k:k + 1], gate[:, k:k + 1], weight)
    ysorted = _load_token_tiled(ybuf.at[cur], sb).astype(BF16)
    y = jnp.dot(weight.astype(BF16), ysorted, preferred_element_type=F32)
    out = _rms(x1_ref[...] + y, gf_ref[...])

    @pl.when(i < npb)
    def _():
        yp_ref[...] = out

    @pl.when(i >= npb)
    def _():
        ys_ref[...] = out


def _combine(y_slots, runs, pos_tk, gate, x1, gf, npt):
    t, d = x1.shape
    tm = DISPATCH_TILE
    nt = t // tm
    npb = npt // tm
    sb = tm * TOP_K + N_EXPERTS * RUN
    smem_blk = lambda f: pl.BlockSpec((1, 1, runs.shape[2]), f, memory_space=pltpu.SMEM)
    return pl.pallas_call(
        functools.partial(_combine_kernel, npb),
        grid=(nt,),
        in_specs=[
            smem_blk(lambda i: (i, 0, 0)),
            smem_blk(lambda i: (jnp.minimum(i + 1, nt - 1), 0, 0)),
            pl.BlockSpec((tm, TOP_K), lambda i: (i, 0)),
            pl.BlockSpec((tm, TOP_K), lambda i: (i, 0)),
            pl.BlockSpec((tm, d), lambda i: (i, 0)),
            pl.BlockSpec((1, d), lambda i: (0, 0)),
            pl.BlockSpec(memory_space=pl.ANY),
        ],
        out_specs=[
            pl.BlockSpec((tm, d), lambda i: (jnp.minimum(i, npb - 1), 0)),
            pl.BlockSpec((tm, d), lambda i: (jnp.maximum(i - npb, 0), 0)),
        ],
        out_shape=[jax.ShapeDtypeStruct((npt, d), F32), jax.ShapeDtypeStruct((t - npt, d), F32)],
        scratch_shapes=[pltpu.VMEM((2, sb * SUB, LANE), F32), pltpu.SemaphoreType.DMA((2,))],
        compiler_params=_params("arbitrary"),
        name="combine_norm",
    )(runs, runs, pos_tk, gate, x1, gf, y_slots)


def _inproj_columns(d):
    da_q, db = HA_Q * HEAD_DIM, HB * HEAD_DIM
    da_kv = HA_KV * HEAD_DIM
    q_b_end = da_q + db
    g_b_end = q_b_end + 2 * d
    ka0 = g_b_end
    va0 = ka0 + da_kv
    kb0 = va0 + da_kv
    vb0 = kb0 + db
    dup = lambda c0: np.concatenate(
        [np.tile(np.arange(c0 + h * HEAD_DIM, c0 + (h + 1) * HEAD_DIM), 2) for h in range(HA_KV)])
    cols = np.concatenate([np.arange(0, g_b_end), dup(ka0), dup(va0), np.arange(kb0, vb0 + db)])
    segs = ((0, da_q), (da_q, db), (q_b_end, 2 * d), (g_b_end, 2 * da_kv),
            (g_b_end + 2 * da_kv, 2 * da_kv), (g_b_end + 4 * da_kv, db), (g_b_end + 4 * da_kv + db, db))
    return cols, segs


def kernel(x_prompt, x_sample, meta_tokens, norm1_g, w_in, attn_sink, rel_pos_bias, w_branch_a,
           w_branch_b, w_out, norm2_g, router_w, router_b, w_gate_up, b_gate_up, w_down, b_down,
           final_norm_g):
    assert norm1_g.shape[0] == 1, "single-layer trunk"
    bp, lp, d = x_prompt.shape
    bs, ls, _ = x_sample.shape
    npt, nst = bp * lp, bs * ls
    assert npt % ROW_TILE == 0 and nst % ROW_TILE == 0
    xp = x_prompt.reshape(npt, d)
    xs = x_sample.reshape(nst, d)

    cols, segs = _inproj_columns(d)
    w_p = w_in[0][:, cols].astype(BF16)
    g1 = norm1_g[0].reshape(1, d)
    qa, qb, gates, ka2, va2, kb, vb = _inproj(xp, xs, g1, w_p, segs)
    kv0 = segs[3][0]
    meta_kv = _meta_proj(meta_tokens, g1, w_p[:, kv0:])
    w_kv = 2 * HA_KV * HEAD_DIM
    meta_kv = _pad_meta(meta_kv)
    km2, vm2 = meta_kv[:, :w_kv], meta_kv[:, w_kv:2 * w_kv]
    kbm, vbm = meta_kv[:, 2 * w_kv:2 * w_kv + HB * HEAD_DIM], meta_kv[:, 2 * w_kv + HB * HEAD_DIM:]

    oa = _window_attention(qa, ka2, va2, km2, vm2, attn_sink[0].astype(F32), npt, lp, ls)
    ob = _nbr_attention(qb, kb, vb, kbm, vbm, rel_pos_bias[0], npt, lp, ls)

    rw = router_w[0].astype(F32)
    rw_top = _bf16_part(rw)
    rw_hi = rw_top.astype(BF16)
    rw_lo = (rw - rw_top).astype(BF16)
    lane_pad = lambda a, fill: jnp.pad(a, ((0, 0), (0, LANE - a.shape[1])), constant_values=fill)
    rw2 = jnp.concatenate([lane_pad(rw_hi, 0), lane_pad(rw_lo, 0)], axis=1)
    rb = lane_pad(router_b[0].reshape(1, -1).astype(F32), -jnp.inf)
    x1, h2, top_idx, gate, rank, counts, cstart = _merge(
        oa, ob, gates, xp, xs, w_branch_a[0].astype(BF16), w_branch_b[0].astype(BF16),
        w_out[0].astype(BF16), norm2_g[0].reshape(1, d), rw2, rb)

    pos_tk, pos, runs, pad_slots, block_e, n_used, n_blocks = _route(top_idx, rank, counts, cstart, FFN_BM)
    x_slots = _dispatch(h2, pos, runs, pad_slots, (n_blocks + N_EXPERTS) * FFN_BM)
    bgu = b_gate_up[0]
    y_slots = _expert_ffn(
        x_slots, block_e, n_used, _wprep(w_gate_up[0]), w_down[0].astype(BF16),
        bgu[:, None, 0::2], bgu[:, None, 1::2], b_down[0][:, None, :])

    yp, ys = _combine(y_slots, runs, pos_tk, gate, x1, final_norm_g.reshape(1, d), npt)
    return yp.reshape(bp, lp, d), ys.reshape(bs, ls, d)
```

```python
import functools

import numpy as np
import jax
import jax.numpy as jnp
from jax import lax
from jax.experimental import pallas as pl
from jax.experimental.pallas import tpu as pltpu

F32 = jnp.float32
BF16 = jnp.bfloat16

N_META = 16
GRID_W = 64
HA_Q = 8
HA_KV = 2
HEAD_DIM = 64
WINDOW = 128
HB = 8
NB_ROWS = 8
NB_COLS = 16
N_EXPERTS = 32
TOP_K = 4
SWIGLU_ALPHA = 1.702
SWIGLU_LIMIT = 7.0
NORM_EPS = 1e-5
NEG_INF = -1e30
SCALE = HEAD_DIM ** -0.5

LANE = 128
SUB = 8
ROW_TILE = 512
WIN_TQ = 256
NBR_ROWS = 4
FFN_BM = 512
DMA_UNROLL = 8
DISPATCH_TILE = ROW_TILE // 2
RUN = 16
MAX_RUNS = DISPATCH_TILE * TOP_K // RUN + N_EXPERTS
VMEM_LIMIT = 56 * 1024 * 1024


def _rms(x, g):
    return x * lax.rsqrt(jnp.mean(x * x, axis=-1, keepdims=True) + NORM_EPS) * g


def _bf16_part(x):
    bits = lax.bitcast_convert_type(x, jnp.uint32) & jnp.uint32(0xFFFF0000)
    return lax.bitcast_convert_type(bits, F32)


def _sigmoid(z):
    return 1.0 / (1.0 + jnp.exp(-z))


def _params(*sem):
    return pltpu.CompilerParams(dimension_semantics=sem, vmem_limit_bytes=VMEM_LIMIT)


def _inproj_kernel(npb, segs, xp_ref, xs_ref, g_ref, w_ref, *o_refs):
    i = pl.program_id(0)
    x = jnp.where(i < npb, xp_ref[...], xs_ref[...])
    h = _rms(x, g_ref[...]).astype(BF16)
    for (c0, width), o_ref in zip(segs, o_refs):
        for off in range(0, width, 512):
            n = min(512, width - off)
            o_ref[:, off:off + n] = jnp.dot(
                h, w_ref[:, c0 + off:c0 + off + n], preferred_element_type=F32).astype(BF16)


def _inproj(xp, xs, g, w, segs):
    npt, d = xp.shape
    t = npt + xs.shape[0]
    tm = ROW_TILE
    npb = npt // tm
    return pl.pallas_call(
        functools.partial(_inproj_kernel, npb, segs),
        grid=(t // tm,),
        in_specs=[
            pl.BlockSpec((tm, d), lambda i: (jnp.minimum(i, npb - 1), 0)),
            pl.BlockSpec((tm, d), lambda i: (jnp.maximum(i - npb, 0), 0)),
            pl.BlockSpec((1, d), lambda i: (0, 0)),
            pl.BlockSpec(w.shape, lambda i: (0, 0)),
        ],
        out_specs=[pl.BlockSpec((tm, width), lambda i: (i, 0)) for _, width in segs],
        out_shape=[jax.ShapeDtypeStruct((t, width), BF16) for _, width in segs],
        compiler_params=_params("arbitrary"),
        name="inproj",
    )(xp, xs, g, w)


def _meta_kernel(m_ref, g_ref, w_ref, o_ref):
    h = _rms(m_ref[...], g_ref[...]).astype(BF16)
    o_ref[...] = jnp.dot(h, w_ref[...], preferred_element_type=F32).astype(BF16)


def _meta_proj(meta, g, w):
    return pl.pallas_call(
        _meta_kernel,
        out_shape=jax.ShapeDtypeStruct((meta.shape[0], w.shape[1]), BF16),
        name="meta_proj",
    )(meta, g, w)


def _window_bias(tq):
    slopes = 2.0 ** (-8.0 * np.arange(1, HA_Q + 1) / HA_Q)
    q = np.arange(tq)[:, None]
    col = np.arange(tq + 2 * WINDOW)[None, :]
    dist = np.abs(col - WINDOW - q)
    band = np.where((dist <= WINDOW)[None], -slopes[:, None, None] * dist[None], NEG_INF)
    full = np.concatenate([band, np.broadcast_to(_meta_cols(), (HA_Q, tq, LANE))], axis=-1)
    return jnp.asarray(full.reshape(HA_Q // 2, 2 * tq, -1), F32)


def _meta_cols():
    return np.where(np.arange(LANE) < N_META, 0.0, NEG_INF)


def _pad_meta(a):
    return jnp.pad(a, ((0, LANE - a.shape[0]), (0, 0)))


def _seq_pos(start, npt, lp, ls):
    is_p = start < npt
    pos = jnp.where(is_p, lax.rem(start, lp), lax.rem(jnp.maximum(start - npt, 0), ls))
    return pos, jnp.where(is_p, lp, ls)


def _window_kernel(npt, lp, ls, sink_ref, qa_ref, kp_ref, kc_ref, kn_ref, vp_ref, vc_ref, vn_ref,
                   km_ref, vm_ref, bias_ref, o_ref):
    tq = qa_ref.shape[0]
    nk = tq + 2 * WINDOW
    pos, seq_len = _seq_pos(pl.program_id(0) * tq, npt, lp, ls)
    pen_prev = jnp.where(pos == 0, NEG_INF, 0.0).astype(F32)
    pen_next = jnp.where(pos + tq == seq_len, NEG_INF, 0.0).astype(F32)
    col = lax.broadcasted_iota(jnp.int32, (1, nk + LANE), 1)
    pen = (jnp.where(col < WINDOW, pen_prev, 0.0)
           + jnp.where((col >= WINDOW + tq) & (col < nk), pen_next, 0.0))
    lo = lax.broadcasted_iota(jnp.int32, (1, LANE), 1) < HEAD_DIM
    top = lax.broadcasted_iota(jnp.int32, (2 * tq, 1), 0) < tq
    for i in range(HA_Q // 2):
        hk = (2 * i) // (HA_Q // HA_KV)
        sl = slice(i * LANE, (i + 1) * LANE)
        ksl = slice(hk * LANE, (hk + 1) * LANE)
        q = qa_ref[:, sl] * jnp.asarray(SCALE, BF16)
        zero = jnp.zeros_like(q)
        qq = jnp.concatenate([jnp.where(lo, q, zero), jnp.where(lo, zero, q)], axis=0)
        k_all = jnp.concatenate([kp_ref[:, ksl], kc_ref[:, ksl], kn_ref[:, ksl], km_ref[:, ksl]], axis=0)
        v_all = jnp.concatenate([vp_ref[:, ksl], vc_ref[:, ksl], vn_ref[:, ksl], vm_ref[:, ksl]], axis=0)
        s = lax.dot_general(qq, k_all, (((1,), (1,)), ((), ())), preferred_element_type=F32)
        s = s + bias_ref[i] + pen
        sink = jnp.where(top, sink_ref[2 * i], sink_ref[2 * i + 1])
        m = jnp.maximum(jnp.max(s, axis=-1, keepdims=True), sink)
        p = jnp.exp(s - m)
        denom = jnp.sum(p, axis=-1, keepdims=True) + jnp.exp(sink - m)
        o = jnp.dot(p.astype(BF16), v_all, preferred_element_type=F32) / denom
        o_ref[:, sl] = jnp.where(lo, o[:tq], o[tq:]).astype(BF16)


def _window_attention(qa, ka2, va2, km2, vm2, sink, npt, lp, ls):
    t = qa.shape[0]
    tq = WIN_TQ
    assert lp % tq == 0 and ls % tq == 0 and tq == 2 * WINDOW
    nhalf = t // WINDOW
    bias = _window_bias(tq)
    kw = ka2.shape[1]
    prev = pl.BlockSpec((WINDOW, kw), lambda j: (jnp.maximum(2 * j - 1, 0), 0))
    cur = pl.BlockSpec((tq, kw), lambda j: (j, 0))
    nxt = pl.BlockSpec((WINDOW, kw), lambda j: (jnp.minimum(2 * j + 2, nhalf - 1), 0))
    whole = lambda a: pl.BlockSpec(a.shape, lambda j: (0,) * a.ndim)
    return pl.pallas_call(
        functools.partial(_window_kernel, npt, lp, ls),
        grid=(t // tq,),
        in_specs=[
            pl.BlockSpec(memory_space=pltpu.SMEM),
            pl.BlockSpec((tq, qa.shape[1]), lambda j: (j, 0)),
            prev, cur, nxt, prev, cur, nxt,
            whole(km2), whole(vm2), whole(bias),
        ],
        out_specs=pl.BlockSpec((tq, qa.shape[1]), lambda j: (j, 0)),
        out_shape=jax.ShapeDtypeStruct(qa.shape, BF16),
        compiler_params=_params("arbitrary"),
        name="window_attn",
    )(sink, qa, ka2, ka2, ka2, va2, va2, va2, km2, vm2, bias)


def _nbr_tables(rpb):
    r, w = NBR_ROWS, GRID_W
    qc = np.arange(w)[:, None]
    kcol = np.arange(w)[None, :]
    col_start = np.clip(qc - NB_COLS // 2, 0, w - NB_COLS)
    col_ok = (kcol >= col_start) & (kcol < col_start + NB_COLS)
    sel = np.zeros((2 * NB_COLS - 1, w, w), np.float32)
    ci, ki = np.nonzero(col_ok)
    sel[ki - ci + NB_COLS - 1, ci, ki] = 1.0
    per_row = jnp.einsum("hrj,jck->hrck", rpb.astype(F32), jnp.asarray(sel), precision=lax.Precision.HIGHEST)
    per_row = jnp.where(jnp.asarray(col_ok), per_row, NEG_INF)
    off = NB_ROWS - 1 - NBR_ROWS
    vals = jnp.concatenate(
        [jnp.concatenate([per_row[:, b - a + off] for b in range(3 * r)], axis=-1) for a in range(r)], axis=1)
    meta = jnp.broadcast_to(jnp.asarray(_meta_cols(), F32), (HB, r * w, LANE))
    bias = jnp.concatenate([vals, meta], axis=-1)
    a = np.repeat(np.arange(r), w)[:, None]
    b = np.repeat(np.arange(3 * r), w)[None, :]
    first = NBR_ROWS
    ok = np.stack([
        (b >= first) & (b < first + NB_ROWS) & (a >= 0),
        (b >= a) & (b < a + NB_ROWS),
        (b >= 0) & (b < NB_ROWS) & (a >= 0),
    ])
    ok = np.concatenate([ok, np.ones((3, r * w, LANE), bool)], axis=-1)
    return bias, jnp.asarray(np.where(ok, 0.0, NEG_INF), F32)


def _nbr_kernel(npg, gp, gs, qb_ref, kp_ref, kc_ref, kn_ref, vp_ref, vc_ref, vn_ref, km_ref, vm_ref,
                bias_ref, mask_ref, o_ref):
    tq = qb_ref.shape[0]
    g = pl.program_id(0)
    is_p = g < npg
    per_seq = jnp.where(is_p, gp, gs)
    r = jnp.where(is_p, lax.rem(g, gp), lax.rem(jnp.maximum(g - npg, 0), gs))
    pat = jnp.where(r == 0, 0, jnp.where(r == per_seq - 1, 2, 1))
    row_mask = mask_ref[pat]
    lo = lax.broadcasted_iota(jnp.int32, (1, LANE), 1) < HEAD_DIM
    for i in range(HB // 2):
        sl = slice(i * LANE, (i + 1) * LANE)
        q = qb_ref[:, sl] * jnp.asarray(SCALE, BF16)
        zero = jnp.zeros_like(q)
        qq = jnp.concatenate([jnp.where(lo, q, zero), jnp.where(lo, zero, q)], axis=0)
        k_all = jnp.concatenate([kp_ref[:, sl], kc_ref[:, sl], kn_ref[:, sl], km_ref[:, sl]], axis=0)
        v_all = jnp.concatenate([vp_ref[:, sl], vc_ref[:, sl], vn_ref[:, sl], vm_ref[:, sl]], axis=0)
        s = lax.dot_general(qq, k_all, (((1,), (1,)), ((), ())), preferred_element_type=F32)
        halves = []
        for hf in range(2):
            sh = s[hf * tq:(hf + 1) * tq] + bias_ref[2 * i + hf] + row_mask
            m = jnp.max(sh, axis=-1, keepdims=True)
            p = jnp.exp(sh - m)
            denom = jnp.sum(p, axis=-1, keepdims=True)
            halves.append(jnp.dot(p.astype(BF16), v_all, preferred_element_type=F32) / denom)
        o_ref[:, sl] = jnp.where(lo, halves[0], halves[1]).astype(BF16)


def _nbr_attention(qb, kb, vb, kbm, vbm, rpb, npt, lp, ls):
    t, width = qb.shape
    tq = NBR_ROWS * GRID_W
    rows_p, rows_s = lp // GRID_W, ls // GRID_W
    assert rows_p % NBR_ROWS == 0 and rows_s % NBR_ROWS == 0 and min(rows_p, rows_s) >= NB_ROWS
    assert NB_ROWS == 2 * NBR_ROWS
    ng = t // tq
    bias, mask = _nbr_tables(rpb)
    prev = pl.BlockSpec((tq, width), lambda g: (jnp.maximum(g - 1, 0), 0))
    cur = pl.BlockSpec((tq, width), lambda g: (g, 0))
    nxt = pl.BlockSpec((tq, width), lambda g: (jnp.minimum(g + 1, ng - 1), 0))
    whole = lambda a: pl.BlockSpec(a.shape, lambda g: (0,) * a.ndim)
    return pl.pallas_call(
        functools.partial(_nbr_kernel, npt // tq, rows_p // NBR_ROWS, rows_s // NBR_ROWS),
        grid=(ng,),
        in_specs=[cur, prev, cur, nxt, prev, cur, nxt, whole(kbm), whole(vbm), whole(bias), whole(mask)],
        out_specs=cur,
        out_shape=jax.ShapeDtypeStruct(qb.shape, BF16),
        compiler_params=_params("arbitrary"),
        name="nbr_attn",
    )(qb, kb, kb, kb, vb, vb, vb, kbm, vbm, bias, mask)


def _merge_kernel(npb, oa_ref, ob_ref, g_ref, xp_ref, xs_ref, wa_ref, wb_ref, wo_ref, g2_ref,
                  rw_ref, rb_ref, tri_ref, x1_ref, h2_ref, idx_ref, gate_ref, rank_ref, cnt_ref, cs_ref,
                  carry):
    i = pl.program_id(0)
    d = x1_ref.shape[1]
    x = jnp.where(i < npb, xp_ref[...], xs_ref[...])
    a = jnp.dot(oa_ref[...], wa_ref[...], preferred_element_type=F32)
    b = jnp.dot(ob_ref[...], wb_ref[...], preferred_element_type=F32)
    mix = _sigmoid(g_ref[:, :d].astype(F32)) * a + _sigmoid(g_ref[:, d:].astype(F32)) * b
    x1 = x + jnp.dot(mix.astype(BF16), wo_ref[...], preferred_element_type=F32)
    x1_ref[...] = x1
    h2 = _rms(x1, g2_ref[...])
    h2_ref[...] = h2.astype(BF16)
    h_top = _bf16_part(h2)
    h_hi = h_top.astype(BF16)
    h_lo = (h2 - h_top).astype(BF16)
    r_hi = jnp.dot(h_hi, rw_ref[...], preferred_element_type=F32)
    r_lo = jnp.dot(h_lo, rw_ref[:, :LANE], preferred_element_type=F32)
    logits = r_hi[:, :LANE] + r_hi[:, LANE:] + r_lo + rb_ref[...]

    lane = lax.broadcasted_iota(jnp.int32, logits.shape, 1).astype(F32)
    work = logits
    member = jnp.zeros(logits.shape, F32)
    vals, idxs = [], []
    for _ in range(TOP_K):
        m = jnp.max(work, axis=-1, keepdims=True)
        idx = jnp.min(jnp.where(work == m, lane, float(LANE)), axis=-1, keepdims=True)
        hit = lane == idx
        work = jnp.where(hit, -jnp.inf, work)
        member = jnp.where(hit, 1.0, member)
        vals.append(m)
        idxs.append(idx)
    e = [jnp.exp(v - vals[0]) for v in vals]
    denom = functools.reduce(lambda p, q: p + q, e)
    gate_ref[...] = jnp.concatenate(e, axis=1) / denom
    idx_ref[...] = jnp.concatenate(idxs, axis=1).astype(jnp.int32)

    @pl.when(i == 0)
    def _():
        carry[...] = jnp.zeros_like(carry)

    before = jnp.dot(tri_ref[...], member.astype(BF16), preferred_element_type=F32) + carry[...]
    ranks = [jnp.sum(jnp.where(lane == idx, before, 0.0), axis=-1, keepdims=True) for idx in idxs]
    rank_ref[...] = jnp.concatenate(ranks, axis=1).astype(jnp.int32)
    half = member.shape[0] // 2
    c0 = carry[...]
    c1 = c0 + jnp.sum(member[:half], axis=0, keepdims=True)
    cs_ref[0] = jnp.concatenate([c0, c1], axis=0).astype(jnp.int32)
    carry[...] = c1 + jnp.sum(member[half:], axis=0, keepdims=True)
    cnt_ref[...] = carry[...].astype(jnp.int32)


def _merge(oa, ob, gates, xp, xs, wa, wb, wo, g2, rw, rb):
    npt, d = xp.shape
    t = npt + xs.shape[0]
    tm = ROW_TILE
    npb = npt // tm
    tri = jnp.asarray(np.tril(np.ones((tm, tm)), -1), BF16)
    row = lambda width: pl.BlockSpec((tm, width), lambda i: (i, 0))
    whole = lambda a: pl.BlockSpec(a.shape, lambda i: (0,) * a.ndim)
    return pl.pallas_call(
        functools.partial(_merge_kernel, npb),
        grid=(t // tm,),
        in_specs=[
            row(oa.shape[1]), row(ob.shape[1]), row(gates.shape[1]),
            pl.BlockSpec((tm, d), lambda i: (jnp.minimum(i, npb - 1), 0)),
            pl.BlockSpec((tm, d), lambda i: (jnp.maximum(i - npb, 0), 0)),
            whole(wa), whole(wb), whole(wo), whole(g2), whole(rw), whole(rb), whole(tri),
        ],
        out_specs=[row(d), row(d), row(TOP_K), row(TOP_K), row(TOP_K),
                   pl.BlockSpec((1, LANE), lambda i: (0, 0)),
                   pl.BlockSpec((1, 2, LANE), lambda i: (i, 0, 0))],
        out_shape=[jax.ShapeDtypeStruct((t, d), F32), jax.ShapeDtypeStruct((t, d), BF16),
                   jax.ShapeDtypeStruct((t, TOP_K), jnp.int32), jax.ShapeDtypeStruct((t, TOP_K), F32),
                   jax.ShapeDtypeStruct((t, TOP_K), jnp.int32),
                   jax.ShapeDtypeStruct((1, LANE), jnp.int32),
                   jax.ShapeDtypeStruct((t // tm, 2, LANE), jnp.int32)],
        scratch_shapes=[pltpu.VMEM((1, LANE), F32)],
        compiler_params=_params("arbitrary"),
        name="merge_router",
    )(oa, ob, gates, xp, xs, wa, wb, wo, g2, rw, rb, tri)


def _route(top_idx, rank, counts, cstart, bm):
    t = top_idx.shape[0]
    ne = N_EXPERTS
    dt = DISPATCH_TILE
    n_d = t // dt
    counts = counts.reshape(-1)[:ne]
    padded = (counts + RUN - 1 + bm - 1) // bm * bm
    pad_end = jnp.cumsum(padded)
    pad_start = pad_end - padded
    n_blocks = -(-(t * TOP_K + ne * (RUN - 1)) // bm) + ne
    experts = jnp.arange(ne, dtype=jnp.int32)

    cs = cstart.reshape(n_d, LANE)[:, :ne]
    cnt = jnp.concatenate([cs[1:], counts[None]], axis=0) - cs
    c_run = (cnt + RUN - 1) // RUN * RUN
    off = jnp.cumsum(c_run, axis=1) - c_run
    onehot = (top_idx[..., None] == experts).reshape(n_d, dt, TOP_K, ne)
    per_tok = lambda tab: jnp.sum(jnp.where(onehot, tab[:, None, None, :], 0), axis=-1)
    rank_t = rank.reshape(n_d, dt, TOP_K)
    pos_tk = (per_tok(off) + rank_t - per_tok(cs)).astype(jnp.int32)
    pos = pos_tk.transpose(0, 2, 1)
    dst0 = pad_start[None, :] + cs

    n_run = c_run // RUN
    cum = jnp.cumsum(n_run, axis=1)
    total = cum[:, -1:]
    i = jnp.minimum(jnp.arange(MAX_RUNS, dtype=jnp.int32)[None, :], total - 1)
    e_of = jnp.minimum(jnp.sum(i[:, :, None] >= cum[:, None, :], axis=-1), ne - 1)
    pick = lambda tab: jnp.sum(jnp.where(e_of[..., None] == experts, tab[:, None, :], 0), axis=-1)
    local = RUN * (i - pick(cum - n_run))
    runs = jnp.concatenate([pick(off) + local, pick(dst0) + local, (total + 1) // 2], axis=1).astype(jnp.int32)

    j = jnp.arange(bm, dtype=jnp.int32)[None, :]
    fill = (pad_start + counts)[:, None] + j
    spare = n_blocks * bm + experts[:, None] * bm + j
    pad_slots = jnp.where(fill < pad_end[:, None], fill, spare).reshape(-1).astype(jnp.int32)
    starts = jnp.arange(n_blocks, dtype=jnp.int32) * bm
    block_e = jnp.minimum(jnp.sum(starts[:, None] >= pad_end[None, :], axis=-1), ne - 1)
    n_used = (pad_end[-1] // bm).astype(jnp.int32).reshape(1)
    return (pos_tk.reshape(t, TOP_K), pos, runs.reshape(n_d, 1, -1), pad_slots, block_e.astype(jnp.int32),
            n_used, n_blocks)


def _dispatch_kernel(runs_ref, prev_ref, pad_ref, pos_ref, h2_ref, x_hbm, sbuf, sem):
    sb = sbuf.shape[1] // SUB
    dt = h2_ref.shape[0]
    i = pl.program_id(0)
    cur = lax.rem(i, 2)
    rows = lax.broadcasted_iota(jnp.int32, (sb, dt), 0)
    hit = rows == pos_ref[0, 0:1, :]
    for k in range(1, TOP_K):
        hit = hit | (rows == pos_ref[0, k:k + 1, :])
    onehot = jnp.where(hit, 1.0, 0.0).astype(BF16)
    _store_token_tiled(sbuf.at[cur], jnp.dot(onehot, h2_ref[...], preferred_element_type=F32))

    def run_copy(ref, buf, r):
        src = sbuf.at[buf, pl.ds(pl.multiple_of(ref[0, 0, r] * SUB, SUB), RUN * SUB)]
        dst = x_hbm.at[pl.ds(pl.multiple_of(ref[0, 0, MAX_RUNS + r] * SUB, SUB), RUN * SUB)]
        return pltpu.make_async_copy(src, dst, sem)

    def drain(ref, buf):
        def wait(p, carry):
            for q in range(2):
                run_copy(ref, buf, 2 * p + q).wait()
            return carry
        lax.fori_loop(0, ref[0, 0, 2 * MAX_RUNS], wait, 0)

    @pl.when(i > 0)
    def _():
        drain(prev_ref, 1 - cur)

    def start(p, carry):
        for q in range(2):
            run_copy(runs_ref, cur, 2 * p + q).start(priority=q)
        return carry
    lax.fori_loop(0, runs_ref[0, 0, 2 * MAX_RUNS], start, 0)

    @pl.when(i == 0)
    def _():
        n_pad = pad_ref.shape[0]

        def pad_copy(j):
            return pltpu.make_async_copy(sbuf.at[0, _tile_rows(0)], x_hbm.at[_tile_rows(pad_ref[j])], sem)

        def pad_start(g, carry):
            for u in range(DMA_UNROLL):
                pad_copy(g * DMA_UNROLL + u).start(priority=u % 2)
            return carry
        lax.fori_loop(0, n_pad // DMA_UNROLL, pad_start, 0)

        def pad_wait(j, carry):
            pad_copy(j).wait()
            return carry
        lax.fori_loop(0, n_pad, pad_wait, 0)

    @pl.when(i == pl.num_programs(0) - 1)
    def _():
        drain(runs_ref, cur)


def _dispatch(h2, pos, runs, pad_slots, n_slots):
    t, d = h2.shape
    dt = DISPATCH_TILE
    sb = dt * TOP_K + N_EXPERTS * RUN
    return pl.pallas_call(
        _dispatch_kernel,
        grid=(t // dt,),
        in_specs=[
            pl.BlockSpec((1, 1, runs.shape[2]), lambda i: (i, 0, 0), memory_space=pltpu.SMEM),
            pl.BlockSpec((1, 1, runs.shape[2]), lambda i: (jnp.maximum(i - 1, 0), 0, 0),
                         memory_space=pltpu.SMEM),
            pl.BlockSpec(memory_space=pltpu.SMEM),
            pl.BlockSpec((1, TOP_K, dt), lambda i: (i, 0, 0)),
            pl.BlockSpec((dt, d), lambda i: (i, 0)),
        ],
        out_specs=pl.BlockSpec(memory_space=pl.ANY),
        out_shape=jax.ShapeDtypeStruct((n_slots * SUB, LANE), F32),
        scratch_shapes=[pltpu.VMEM((2, sb * SUB, LANE), F32), pltpu.SemaphoreType.DMA(())],
        compiler_params=_params("arbitrary"),
        name="dispatch",
    )(runs, runs, pad_slots, pos, h2)


def _wprep_kernel(w_ref, eo_ref, o_ref):
    half = o_ref.shape[2] // 2
    two = 2 * LANE
    for c in range(w_ref.shape[2] // two):
        blk = w_ref[0, :, c * two:(c + 1) * two].astype(BF16)
        r = jnp.dot(blk, eo_ref[...], preferred_element_type=F32)
        o_ref[0, :, c * LANE:(c + 1) * LANE] = r[:, :LANE].astype(BF16)
        o_ref[0, :, half + c * LANE:half + (c + 1) * LANE] = r[:, LANE:].astype(BF16)


def _wprep(w):
    ne, d, f2 = w.shape
    eo = np.zeros((2 * LANE, 2 * LANE))
    eo[2 * np.arange(LANE), np.arange(LANE)] = 1.0
    eo[2 * np.arange(LANE) + 1, LANE + np.arange(LANE)] = 1.0
    return pl.pallas_call(
        _wprep_kernel,
        grid=(ne,),
        in_specs=[pl.BlockSpec((1, d, f2), lambda e: (e, 0, 0)),
                  pl.BlockSpec((2 * LANE, 2 * LANE), lambda e: (0, 0))],
        out_specs=pl.BlockSpec((1, d, f2), lambda e: (e, 0, 0)),
        out_shape=jax.ShapeDtypeStruct(w.shape, BF16),
        compiler_params=_params("arbitrary"),
        name="expert_weight_layout",
    )(w, jnp.asarray(eo, BF16))


def _tile_rows(r):
    return pl.ds(pl.multiple_of(r * SUB, SUB), SUB)


def _store_token_tiled(ref, val):
    n = val.shape[0]
    for c in range(SUB):
        ref[pl.ds(c, n, stride=SUB), :] = val[:, c * LANE:(c + 1) * LANE]


def _load_token_tiled(ref, n):
    return jnp.concatenate([ref[pl.ds(c, n, stride=SUB), :] for c in range(SUB)], axis=1)


def _ffn_kernel(be_ref, nu_ref, x_ref, wg_ref, wu_ref, wd_ref, bg_ref, bu_ref, bd_ref, y_ref):
    bm = y_ref.shape[0] // SUB
    b = pl.program_id(0)
    n_used = nu_ref[0]

    @pl.when(b < n_used)
    def _():
        x = _load_token_tiled(x_ref, bm).astype(BF16)
        hg = jnp.dot(x, wg_ref[0], preferred_element_type=F32) + bg_ref[0]
        hu = jnp.dot(x, wu_ref[0], preferred_element_type=F32) + bu_ref[0]
        glu = jnp.minimum(hg, SWIGLU_LIMIT)
        lin = jnp.clip(hu, -SWIGLU_LIMIT, SWIGLU_LIMIT)
        act = glu * _sigmoid(SWIGLU_ALPHA * glu) * (lin + 1.0)
        y = jnp.dot(act.astype(BF16), wd_ref[0], preferred_element_type=F32) + bd_ref[0]
        _store_token_tiled(y_ref, y)

    @pl.when(b >= n_used)
    def _():
        y_ref[...] = jnp.zeros_like(y_ref)


def _expert_ffn(x_slots, block_e, n_used, wgu, wd, bg, bu, bd):
    bm = FFN_BM
    d, f = wd.shape[2], wd.shape[1]
    assert d == SUB * LANE
    nb = block_e.shape[0]
    wspec = lambda a: pl.BlockSpec((1,) + a.shape[1:], lambda b, be, nu: (be[b], 0, 0))
    wg_spec = pl.BlockSpec((1, d, f), lambda b, be, nu: (be[b], 0, 0))
    wu_spec = pl.BlockSpec((1, d, f), lambda b, be, nu: (be[b], 0, 1))
    rows = pl.BlockSpec((bm * SUB, LANE), lambda b, be, nu: (b, 0))
    return pl.pallas_call(
        _ffn_kernel,
        grid_spec=pltpu.PrefetchScalarGridSpec(
            num_scalar_prefetch=2,
            grid=(nb,),
            in_specs=[rows, wg_spec, wu_spec, wspec(wd), wspec(bg), wspec(bu), wspec(bd)],
            out_specs=rows,
        ),
        out_shape=jax.ShapeDtypeStruct((nb * bm * SUB, LANE), F32),
        compiler_params=_params("arbitrary"),
        name="expert_ffn",
    )(block_e, n_used, x_slots, wgu, wgu, wd, bg, bu, bd)


def _combine_kernel(npb, runs_ref, next_ref, pos_ref, gate_ref, x1_ref, gf_ref, y_hbm, yp_ref, ys_ref,
                    ybuf, sem):
    sb = ybuf.shape[1] // SUB
    i = pl.program_id(0)
    n = pl.num_programs(0)
    cur = lax.rem(i, 2)

    def run_copy(ref, buf, r):
        src = y_hbm.at[pl.ds(pl.multiple_of(ref[0, 0, MAX_RUNS + r] * SUB, SUB), RUN * SUB)]
        dst = ybuf.at[buf, pl.ds(pl.multiple_of(ref[0, 0, r] * SUB, SUB), RUN * SUB)]
        return pltpu.make_async_copy(src, dst, sem.at[buf])

    def issue(ref, buf):
        def start(p, carry):
            for q in range(2):
                run_copy(ref, buf, 2 * p + q).start(priority=q)
            return carry
        lax.fori_loop(0, ref[0, 0, 2 * MAX_RUNS], start, 0)

    @pl.when(i == 0)
    def _():
        ybuf[...] = jnp.zeros_like(ybuf)
        issue(runs_ref, 0)

    @pl.when(i + 1 < n)
    def _():
        issue(next_ref, 1 - cur)

    def wait(p, carry):
        for q in range(2):
            run_copy(runs_ref, cur, 2 * p + q).wait()
        return carry
    lax.fori_loop(0, runs_ref[0, 0, 2 * MAX_RUNS], wait, 0)

    gate = gate_ref[...]
    pos = pos_ref[...]
    col = lax.broadcasted_iota(jnp.int32, (gate.shape[0], sb), 1)
    weight = jnp.zeros(col.shape, F32)
    for k in range(TOP_K):
        weight = jnp.where(col == pos[:, k:k + 1], gate[:, k:k + 1], weight)
    ysorted = _load_token_tiled(ybuf.at[cur], sb).astype(BF16)
    y = jnp.dot(weight.astype(BF16), ysorted, preferred_element_type=F32)
    out = _rms(x1_ref[...] + y, gf_ref[...])

    @pl.when(i < npb)
    def _():
        yp_ref[...] = out

    @pl.when(i >= npb)
    def _():
        ys_ref[...] = out


def _combine(y_slots, runs, pos_tk, gate, x1, gf, npt):
    t, d = x1.shape
    tm = DISPATCH_TILE
    nt = t // tm
    npb = npt // tm
    sb = tm * TOP_K + N_EXPERTS * RUN
    smem_blk = lambda f: pl.BlockSpec((1, 1, runs.shape[2]), f, memory_space=pltpu.SMEM)
    return pl.pallas_call(
        functools.partial(_combine_kernel, npb),
        grid=(nt,),
        in_specs=[
            smem_blk(lambda i: (i, 0, 0)),
            smem_blk(lambda i: (jnp.minimum(i + 1, nt - 1), 0, 0)),
            pl.BlockSpec((tm, TOP_K), lambda i: (i, 0)),
            pl.BlockSpec((tm, TOP_K), lambda i: (i, 0)),
            pl.BlockSpec((tm, d), lambda i: (i, 0)),
            pl.BlockSpec((1, d), lambda i: (0, 0)),
            pl.BlockSpec(memory_space=pl.ANY),
        ],
        out_specs=[
            pl.BlockSpec((tm, d), lambda i: (jnp.minimum(i, npb - 1), 0)),
            pl.BlockSpec((tm, d), lambda i: (jnp.maximum(i - npb, 0), 0)),
        ],
        out_shape=[jax.ShapeDtypeStruct((npt, d), F32), jax.ShapeDtypeStruct((t - npt, d), F32)],
        scratch_shapes=[pltpu.VMEM((2, sb * SUB, LANE), F32), pltpu.SemaphoreType.DMA((2,))],
        compiler_params=_params("arbitrary"),
        name="combine_norm",
    )(runs, runs, pos_tk, gate, x1, gf, y_slots)


def _inproj_columns(d):
    da_q, db = HA_Q * HEAD_DIM, HB * HEAD_DIM
    da_kv = HA_KV * HEAD_DIM
    q_b_end = da_q + db
    g_b_end = q_b_end + 2 * d
    ka0 = g_b_end
    va0 = ka0 + da_kv
    kb0 = va0 + da_kv
    vb0 = kb0 + db
    dup = lambda c0: np.concatenate(
        [np.tile(np.arange(c0 + h * HEAD_DIM, c0 + (h + 1) * HEAD_DIM), 2) for h in range(HA_KV)])
    cols = np.concatenate([np.arange(0, g_b_end), dup(ka0), dup(va0), np.arange(kb0, vb0 + db)])
    segs = ((0, da_q), (da_q, db), (q_b_end, 2 * d), (g_b_end, 2 * da_kv),
            (g_b_end + 2 * da_kv, 2 * da_kv), (g_b_end + 4 * da_kv, db), (g_b_end + 4 * da_kv + db, db))
    return cols, segs


def kernel(x_prompt, x_sample, meta_tokens, norm1_g, w_in, attn_sink, rel_pos_bias, w_branch_a,
           w_branch_b, w_out, norm2_g, router_w, router_b, w_gate_up, b_gate_up, w_down, b_down,
           final_norm_g):
    assert norm1_g.shape[0] == 1, "single-layer trunk"
    bp, lp, d = x_prompt.shape
    bs, ls, _ = x_sample.shape
    npt, nst = bp * lp, bs * ls
    assert npt % ROW_TILE == 0 and nst % ROW_TILE == 0
    xp = x_prompt.reshape(npt, d)
    xs = x_sample.reshape(nst, d)

    cols, segs = _inproj_columns(d)
    w_p = w_in[0][:, cols].astype(BF16)
    g1 = norm1_g[0].reshape(1, d)
    qa, qb, gates, ka2, va2, kb, vb = _inproj(xp, xs, g1, w_p, segs)
    kv0 = segs[3][0]
    meta_kv = _meta_proj(meta_tokens, g1, w_p[:, kv0:])
    w_kv = 2 * HA_KV * HEAD_DIM
    meta_kv = _pad_meta(meta_kv)
    km2, vm2 = meta_kv[:, :w_kv], meta_kv[:, w_kv:2 * w_kv]
    kbm, vbm = meta_kv[:, 2 * w_kv:2 * w_kv + HB * HEAD_DIM], meta_kv[:, 2 * w_kv + HB * HEAD_DIM:]

    oa = _window_attention(qa, ka2, va2, km2, vm2, attn_sink[0].astype(F32), npt, lp, ls)
    ob = _nbr_attention(qb, kb, vb, kbm, vbm, rel_pos_bias[0], npt, lp, ls)

    rw = router_w[0].astype(F32)
    rw_top = _bf16_part(rw)
    rw_hi = rw_top.astype(BF16)
    rw_lo = (rw - rw_top).astype(BF16)
    lane_pad = lambda a, fill: jnp.pad(a, ((0, 0), (0, LANE - a.shape[1])), constant_values=fill)
    rw2 = jnp.concatenate([lane_pad(rw_hi, 0), lane_pad(rw_lo, 0)], axis=1)
    rb = lane_pad(router_b[0].reshape(1, -1).astype(F32), -jnp.inf)
    x1, h2, top_idx, gate, rank, counts, cstart = _merge(
        oa, ob, gates, xp, xs, w_branch_a[0].astype(BF16), w_branch_b[0].astype(BF16),
        w_out[0].astype(BF16), norm2_g[0].reshape(1, d), rw2, rb)

    pos_tk, pos, runs, pad_slots, block_e, n_used, n_blocks = _route(top_idx, rank, counts, cstart, FFN_BM)
    x_slots = _dispatch(h2, pos, runs, pad_slots, (n_blocks + N_EXPERTS) * FFN_BM)
    bgu = b_gate_up[0]
    y_slots = _expert_ffn(
        x_slots, block_e, n_used, _wprep(w_gate_up[0]), w_down[0].astype(BF16),
        bgu[:, None, 0::2], bgu[:, None, 1::2], b_down[0][:, None, :])

    yp, ys = _combine(y_slots, runs, pos_tk, gate, x1, final_norm_g.reshape(1, d), npt)
    return yp.reshape(bp, lp, d), ys.reshape(bs, ls, d)
```

```python
import functools

import numpy as np
import jax
import jax.numpy as jnp
from jax import lax
from jax.experimental import pallas as pl
from jax.experimental.pallas import tpu as pltpu

F32 = jnp.float32
BF16 = jnp.bfloat16

N_META = 16
GRID_W = 64
HA_Q = 8
HA_KV = 2
HEAD_DIM = 64
WINDOW = 128
HB = 8
NB_ROWS = 8
NB_COLS = 16
N_EXPERTS = 32
TOP_K = 4
SWIGLU_ALPHA = 1.702
SWIGLU_LIMIT = 7.0
NORM_EPS = 1e-5
NEG_INF = -1e30
SCALE = HEAD_DIM ** -0.5

LANE = 128
SUB = 8
ROW_TILE = 512
WIN_TQ = 256
NBR_ROWS = 4
FFN_BM = 512
DMA_UNROLL = 8
DISPATCH_TILE = ROW_TILE // 2
RUN = 16
MAX_RUNS = DISPATCH_TILE * TOP_K // RUN + N_EXPERTS
VMEM_LIMIT = 56 * 1024 * 1024


def _rms(x, g):
    return x * lax.rsqrt(jnp.mean(x * x, axis=-1, keepdims=True) + NORM_EPS) * g


def _bf16_part(x):
    bits = lax.bitcast_convert_type(x, jnp.uint32) & jnp.uint32(0xFFFF0000)
    return lax.bitcast_convert_type(bits, F32)


def _with_ones(v):
    return jnp.concatenate([v, jnp.ones(v.shape, v.dtype)], axis=1)


def _sigmoid(z):
    return 1.0 / (1.0 + jnp.exp(-z))


def _params(*sem):
    return pltpu.CompilerParams(dimension_semantics=sem, vmem_limit_bytes=VMEM_LIMIT)


def _inproj_kernel(npb, segs, xp_ref, xs_ref, g_ref, w_ref, *o_refs):
    i = pl.program_id(0)
    x = jnp.where(i < npb, xp_ref[...], xs_ref[...])
    h = _rms(x, g_ref[...]).astype(BF16)
    for (c0, width), o_ref in zip(segs, o_refs):
        for off in range(0, width, 512):
            n = min(512, width - off)
            o_ref[:, off:off + n] = jnp.dot(
                h, w_ref[:, c0 + off:c0 + off + n], preferred_element_type=F32).astype(BF16)


def _inproj(xp, xs, g, w, segs):
    npt, d = xp.shape
    t = npt + xs.shape[0]
    tm = ROW_TILE
    npb = npt // tm
    return pl.pallas_call(
        functools.partial(_inproj_kernel, npb, segs),
        grid=(t // tm,),
        in_specs=[
            pl.BlockSpec((tm, d), lambda i: (jnp.minimum(i, npb - 1), 0)),
            pl.BlockSpec((tm, d), lambda i: (jnp.maximum(i - npb, 0), 0)),
            pl.BlockSpec((1, d), lambda i: (0, 0)),
            pl.BlockSpec(w.shape, lambda i: (0, 0)),
        ],
        out_specs=[pl.BlockSpec((tm, width), lambda i: (i, 0)) for _, width in segs],
        out_shape=[jax.ShapeDtypeStruct((t, width), BF16) for _, width in segs],
        compiler_params=_params("arbitrary"),
        name="inproj",
    )(xp, xs, g, w)


def _meta_kernel(m_ref, g_ref, w_ref, o_ref):
    h = _rms(m_ref[...], g_ref[...]).astype(BF16)
    o_ref[...] = jnp.dot(h, w_ref[...], preferred_element_type=F32).astype(BF16)


def _meta_proj(meta, g, w):
    return pl.pallas_call(
        _meta_kernel,
        out_shape=jax.ShapeDtypeStruct((meta.shape[0], w.shape[1]), BF16),
        name="meta_proj",
    )(meta, g, w)


def _window_bias(tq):
    slopes = 2.0 ** (-8.0 * np.arange(1, HA_Q + 1) / HA_Q)
    q = np.arange(tq)[:, None]
    col = np.arange(tq + 2 * WINDOW)[None, :]
    dist = np.abs(col - WINDOW - q)
    band = np.where((dist <= WINDOW)[None], -slopes[:, None, None] * dist[None], NEG_INF)
    full = np.concatenate([band, np.broadcast_to(_meta_cols(), (HA_Q, tq, LANE))], axis=-1)
    return jnp.asarray(full.reshape(HA_Q // 2, 2 * tq, -1), F32)


def _meta_cols():
    return np.where(np.arange(LANE) < N_META, 0.0, NEG_INF)


def _pad_meta(a):
    return jnp.pad(a, ((0, LANE - a.shape[0]), (0, 0)))


def _seq_pos(start, npt, lp, ls):
    is_p = start < npt
    pos = jnp.where(is_p, lax.rem(start, lp), lax.rem(jnp.maximum(start - npt, 0), ls))
    return pos, jnp.where(is_p, lp, ls)


def _window_kernel(npt, lp, ls, sink_ref, qa_ref, kp_ref, kc_ref, kn_ref, vp_ref, vc_ref, vn_ref,
                   km_ref, vm_ref, bias_ref, o_ref):
    tq = qa_ref.shape[0]
    nk = tq + 2 * WINDOW
    pos, seq_len = _seq_pos(pl.program_id(0) * tq, npt, lp, ls)
    pen_prev = jnp.where(pos == 0, NEG_INF, 0.0).astype(F32)
    pen_next = jnp.where(pos + tq == seq_len, NEG_INF, 0.0).astype(F32)
    col = lax.broadcasted_iota(jnp.int32, (1, nk + LANE), 1)
    pen = (jnp.where(col < WINDOW, pen_prev, 0.0)
           + jnp.where((col >= WINDOW + tq) & (col < nk), pen_next, 0.0))
    lo = lax.broadcasted_iota(jnp.int32, (1, LANE), 1) < HEAD_DIM
    top = lax.broadcasted_iota(jnp.int32, (2 * tq, 1), 0) < tq
    for i in range(HA_Q // 2):
        hk = (2 * i) // (HA_Q // HA_KV)
        sl = slice(i * LANE, (i + 1) * LANE)
        ksl = slice(hk * LANE, (hk + 1) * LANE)
        q = qa_ref[:, sl] * jnp.asarray(SCALE, BF16)
        zero = jnp.zeros_like(q)
        qq = jnp.concatenate([jnp.where(lo, q, zero), jnp.where(lo, zero, q)], axis=0)
        k_all = jnp.concatenate([kp_ref[:, ksl], kc_ref[:, ksl], kn_ref[:, ksl], km_ref[:, ksl]], axis=0)
        v_all = jnp.concatenate([vp_ref[:, ksl], vc_ref[:, ksl], vn_ref[:, ksl], vm_ref[:, ksl]], axis=0)
        s = lax.dot_general(qq, k_all, (((1,), (1,)), ((), ())), preferred_element_type=F32)
        s = s + bias_ref[i] + pen
        sink = jnp.where(top, sink_ref[2 * i], sink_ref[2 * i + 1])
        m = jnp.maximum(jnp.max(s, axis=-1, keepdims=True), sink)
        acc = jnp.dot(jnp.exp((s - m).astype(BF16)), _with_ones(v_all), preferred_element_type=F32)
        o = acc[:, :LANE] / (acc[:, LANE:LANE + 1] + jnp.exp(sink - m))
        o_ref[:, sl] = jnp.where(lo, o[:tq], o[tq:]).astype(BF16)


def _window_attention(qa, ka2, va2, km2, vm2, sink, npt, lp, ls):
    t = qa.shape[0]
    tq = WIN_TQ
    assert lp % tq == 0 and ls % tq == 0 and tq == 2 * WINDOW
    nhalf = t // WINDOW
    bias = _window_bias(tq)
    kw = ka2.shape[1]
    prev = pl.BlockSpec((WINDOW, kw), lambda j: (jnp.maximum(2 * j - 1, 0), 0))
    cur = pl.BlockSpec((tq, kw), lambda j: (j, 0))
    nxt = pl.BlockSpec((WINDOW, kw), lambda j: (jnp.minimum(2 * j + 2, nhalf - 1), 0))
    whole = lambda a: pl.BlockSpec(a.shape, lambda j: (0,) * a.ndim)
    return pl.pallas_call(
        functools.partial(_window_kernel, npt, lp, ls),
        grid=(t // tq,),
        in_specs=[
            pl.BlockSpec(memory_space=pltpu.SMEM),
            pl.BlockSpec((tq, qa.shape[1]), lambda j: (j, 0)),
            prev, cur, nxt, prev, cur, nxt,
            whole(km2), whole(vm2), whole(bias),
        ],
        out_specs=pl.BlockSpec((tq, qa.shape[1]), lambda j: (j, 0)),
        out_shape=jax.ShapeDtypeStruct(qa.shape, BF16),
        compiler_params=_params("arbitrary"),
        name="window_attn",
    )(sink, qa, ka2, ka2, ka2, va2, va2, va2, km2, vm2, bias)


def _nbr_tables(rpb):
    r, w = NBR_ROWS, GRID_W
    qc = np.arange(w)[:, None]
    kcol = np.arange(w)[None, :]
    col_start = np.clip(qc - NB_COLS // 2, 0, w - NB_COLS)
    col_ok = (kcol >= col_start) & (kcol < col_start + NB_COLS)
    sel = np.zeros((2 * NB_COLS - 1, w, w), np.float32)
    ci, ki = np.nonzero(col_ok)
    sel[ki - ci + NB_COLS - 1, ci, ki] = 1.0
    per_row = jnp.einsum("hrj,jck->hrck", rpb.astype(F32), jnp.asarray(sel), precision=lax.Precision.HIGHEST)
    per_row = jnp.where(jnp.asarray(col_ok), per_row, NEG_INF)
    off = NB_ROWS - 1 - NBR_ROWS
    vals = jnp.concatenate(
        [jnp.concatenate([per_row[:, b - a + off] for b in range(3 * r)], axis=-1) for a in range(r)], axis=1)
    meta = jnp.broadcast_to(jnp.asarray(_meta_cols(), F32), (HB, r * w, LANE))
    bias = jnp.concatenate([vals, meta], axis=-1)
    a = np.repeat(np.arange(r), w)[:, None]
    b = np.repeat(np.arange(3 * r), w)[None, :]
    first = NBR_ROWS
    ok = np.stack([
        (b >= first) & (b < first + NB_ROWS) & (a >= 0),
        (b >= a) & (b < a + NB_ROWS),
        (b >= 0) & (b < NB_ROWS) & (a >= 0),
    ])
    ok = np.concatenate([ok, np.ones((3, r * w, LANE), bool)], axis=-1)
    return bias, jnp.asarray(np.where(ok, 0.0, NEG_INF), F32)


def _nbr_kernel(npg, gp, gs, qb_ref, kp_ref, kc_ref, kn_ref, vp_ref, vc_ref, vn_ref, km_ref, vm_ref,
                bias_ref, mask_ref, o_ref):
    tq = qb_ref.shape[0]
    g = pl.program_id(0)
    is_p = g < npg
    per_seq = jnp.where(is_p, gp, gs)
    r = jnp.where(is_p, lax.rem(g, gp), lax.rem(jnp.maximum(g - npg, 0), gs))
    pat = jnp.where(r == 0, 0, jnp.where(r == per_seq - 1, 2, 1))
    row_mask = mask_ref[pat]
    lo = lax.broadcasted_iota(jnp.int32, (1, LANE), 1) < HEAD_DIM
    for i in range(HB // 2):
        sl = slice(i * LANE, (i + 1) * LANE)
        q = qb_ref[:, sl] * jnp.asarray(SCALE, BF16)
        zero = jnp.zeros_like(q)
        qq = jnp.concatenate([jnp.where(lo, q, zero), jnp.where(lo, zero, q)], axis=0)
        k_all = jnp.concatenate([kp_ref[:, sl], kc_ref[:, sl], kn_ref[:, sl], km_ref[:, sl]], axis=0)
        v_all = jnp.concatenate([vp_ref[:, sl], vc_ref[:, sl], vn_ref[:, sl], vm_ref[:, sl]], axis=0)
        s = lax.dot_general(qq, k_all, (((1,), (1,)), ((), ())), preferred_element_type=F32)
        v_ext = _with_ones(v_all)
        halves = []
        for hf in range(2):
            sh = s[hf * tq:(hf + 1) * tq] + bias_ref[2 * i + hf] + row_mask
            m = jnp.max(sh, axis=-1, keepdims=True)
            acc = jnp.dot(jnp.exp((sh - m).astype(BF16)), v_ext, preferred_element_type=F32)
            halves.append(acc[:, :LANE] / acc[:, LANE:LANE + 1])
        o_ref[:, sl] = jnp.where(lo, halves[0], halves[1]).astype(BF16)


def _nbr_attention(qb, kb, vb, kbm, vbm, rpb, npt, lp, ls):
    t, width = qb.shape
    tq = NBR_ROWS * GRID_W
    rows_p, rows_s = lp // GRID_W, ls // GRID_W
    assert rows_p % NBR_ROWS == 0 and rows_s % NBR_ROWS == 0 and min(rows_p, rows_s) >= NB_ROWS
    assert NB_ROWS == 2 * NBR_ROWS
    ng = t // tq
    bias, mask = _nbr_tables(rpb)
    prev = pl.BlockSpec((tq, width), lambda g: (jnp.maximum(g - 1, 0), 0))
    cur = pl.BlockSpec((tq, width), lambda g: (g, 0))
    nxt = pl.BlockSpec((tq, width), lambda g: (jnp.minimum(g + 1, ng - 1), 0))
    whole = lambda a: pl.BlockSpec(a.shape, lambda g: (0,) * a.ndim)
    return pl.pallas_call(
        functools.partial(_nbr_kernel, npt // tq, rows_p // NBR_ROWS, rows_s // NBR_ROWS),
        grid=(ng,),
        in_specs=[cur, prev, cur, nxt, prev, cur, nxt, whole(kbm), whole(vbm), whole(bias), whole(mask)],
        out_specs=cur,
        out_shape=jax.ShapeDtypeStruct(qb.shape, BF16),
        compiler_params=_params("arbitrary"),
        name="nbr_attn",
    )(qb, kb, kb, kb, vb, vb, vb, kbm, vbm, bias, mask)


def _merge_kernel(npb, oa_ref, ob_ref, g_ref, xp_ref, xs_ref, wa_ref, wb_ref, wo_ref, g2_ref,
                  rw_ref, rb_ref, tri_ref, x1_ref, h2_ref, idx_ref, gate_ref, rank_ref, cnt_ref, cs_ref,
                  carry):
    i = pl.program_id(0)
    d = x1_ref.shape[1]
    x = jnp.where(i < npb, xp_ref[...], xs_ref[...])
    a = jnp.dot(oa_ref[...], wa_ref[...], preferred_element_type=F32)
    b = jnp.dot(ob_ref[...], wb_ref[...], preferred_element_type=F32)
    mix = _sigmoid(g_ref[:, :d].astype(F32)) * a + _sigmoid(g_ref[:, d:].astype(F32)) * b
    x1 = x + jnp.dot(mix.astype(BF16), wo_ref[...], preferred_element_type=F32)
    x1_ref[...] = x1
    h2 = _rms(x1, g2_ref[...])
    h2_ref[...] = h2.astype(BF16)
    h_top = _bf16_part(h2)
    h_hi = h_top.astype(BF16)
    h_lo = (h2 - h_top).astype(BF16)
    r_hi = jnp.dot(h_hi, rw_ref[...], preferred_element_type=F32)
    r_lo = jnp.dot(h_lo, rw_ref[:, :LANE], preferred_element_type=F32)
    logits = r_hi[:, :LANE] + r_hi[:, LANE:] + r_lo + rb_ref[...]

    lane = lax.broadcasted_iota(jnp.int32, logits.shape, 1).astype(F32)
    work = logits
    member = jnp.zeros(logits.shape, F32)
    vals, idxs = [], []
    for _ in range(TOP_K):
        m = jnp.max(work, axis=-1, keepdims=True)
        idx = jnp.min(jnp.where(work == m, lane, float(LANE)), axis=-1, keepdims=True)
        hit = lane == idx
        work = jnp.where(hit, -jnp.inf, work)
        member = jnp.where(hit, 1.0, member)
        vals.append(m)
        idxs.append(idx)
    e = [jnp.exp(v - vals[0]) for v in vals]
    denom = functools.reduce(lambda p, q: p + q, e)
    gate_ref[...] = jnp.concatenate(e, axis=1) / denom
    idx_ref[...] = jnp.concatenate(idxs, axis=1).astype(jnp.int32)

    @pl.when(i == 0)
    def _():
        carry[...] = jnp.zeros_like(carry)

    before = jnp.dot(tri_ref[...], member.astype(BF16), preferred_element_type=F32) + carry[...]
    ranks = [jnp.sum(jnp.where(lane == idx, before, 0.0), axis=-1, keepdims=True) for idx in idxs]
    rank_ref[...] = jnp.concatenate(ranks, axis=1).astype(jnp.int32)
    half = member.shape[0] // 2
    c0 = carry[...]
    c1 = c0 + jnp.sum(member[:half], axis=0, keepdims=True)
    cs_ref[0] = jnp.concatenate([c0, c1], axis=0).astype(jnp.int32)
    carry[...] = c1 + jnp.sum(member[half:], axis=0, keepdims=True)
    cnt_ref[...] = carry[...].astype(jnp.int32)


def _merge(oa, ob, gates, xp, xs, wa, wb, wo, g2, rw, rb):
    npt, d = xp.shape
    t = npt + xs.shape[0]
    tm = ROW_TILE
    npb = npt // tm
    tri = jnp.asarray(np.tril(np.ones((tm, tm)), -1), BF16)
    row = lambda width: pl.BlockSpec((tm, width), lambda i: (i, 0))
    whole = lambda a: pl.BlockSpec(a.shape, lambda i: (0,) * a.ndim)
    return pl.pallas_call(
        functools.partial(_merge_kernel, npb),
        grid=(t // tm,),
        in_specs=[
            row(oa.shape[1]), row(ob.shape[1]), row(gates.shape[1]),
            pl.BlockSpec((tm, d), lambda i: (jnp.minimum(i, npb - 1), 0)),
            pl.BlockSpec((tm, d), lambda i: (jnp.maximum(i - npb, 0), 0)),
            whole(wa), whole(wb), whole(wo), whole(g2), whole(rw), whole(rb), whole(tri),
        ],
        out_specs=[row(d), row(d), row(TOP_K), row(TOP_K), row(TOP_K),
                   pl.BlockSpec((1, LANE), lambda i: (0, 0)),
                   pl.BlockSpec((1, 2, LANE), lambda i: (i, 0, 0))],
        out_shape=[jax.ShapeDtypeStruct((t, d), F32), jax.ShapeDtypeStruct((t, d), BF16),
                   jax.ShapeDtypeStruct((t, TOP_K), jnp.int32), jax.ShapeDtypeStruct((t, TOP_K), F32),
                   jax.ShapeDtypeStruct((t, TOP_K), jnp.int32),
                   jax.ShapeDtypeStruct((1, LANE), jnp.int32),
                   jax.ShapeDtypeStruct((t // tm, 2, LANE), jnp.int32)],
        scratch_shapes=[pltpu.VMEM((1, LANE), F32)],
        compiler_params=_params("arbitrary"),
        name="merge_router",
    )(oa, ob, gates, xp, xs, wa, wb, wo, g2, rw, rb, tri)


def _route(top_idx, rank, counts, cstart, bm):
    t = top_idx.shape[0]
    ne = N_EXPERTS
    dt = DISPATCH_TILE
    n_d = t // dt
    counts = counts.reshape(-1)[:ne]
    padded = (counts + RUN - 1 + bm - 1) // bm * bm
    pad_end = jnp.cumsum(padded)
    pad_start = pad_end - padded
    n_blocks = -(-(t * TOP_K + ne * (RUN - 1)) // bm) + ne
    experts = jnp.arange(ne, dtype=jnp.int32)

    cs = cstart.reshape(n_d, LANE)[:, :ne]
    cnt = jnp.concatenate([cs[1:], counts[None]], axis=0) - cs
    c_run = (cnt + RUN - 1) // RUN * RUN
    off = jnp.cumsum(c_run, axis=1) - c_run
    onehot = (top_idx[..., None] == experts).reshape(n_d, dt, TOP_K, ne)
    per_tok = lambda tab: jnp.sum(jnp.where(onehot, tab[:, None, None, :], 0), axis=-1)
    rank_t = rank.reshape(n_d, dt, TOP_K)
    pos_tk = (per_tok(off) + rank_t - per_tok(cs)).astype(jnp.int32)
    pos = pos_tk.transpose(0, 2, 1)
    dst0 = pad_start[None, :] + cs

    n_run = c_run // RUN
    cum = jnp.cumsum(n_run, axis=1)
    total = cum[:, -1:]
    i = jnp.minimum(jnp.arange(MAX_RUNS, dtype=jnp.int32)[None, :], total - 1)
    e_of = jnp.minimum(jnp.sum(i[:, :, None] >= cum[:, None, :], axis=-1), ne - 1)
    pick = lambda tab: jnp.sum(jnp.where(e_of[..., None] == experts, tab[:, None, :], 0), axis=-1)
    local = RUN * (i - pick(cum - n_run))
    runs = jnp.concatenate([pick(off) + local, pick(dst0) + local, (total + 1) // 2], axis=1).astype(jnp.int32)

    j = jnp.arange(bm, dtype=jnp.int32)[None, :]
    fill = (pad_start + counts)[:, None] + j
    spare = n_blocks * bm + experts[:, None] * bm + j
    pad_slots = jnp.where(fill < pad_end[:, None], fill, spare).reshape(-1).astype(jnp.int32)
    starts = jnp.arange(n_blocks, dtype=jnp.int32) * bm
    block_e = jnp.minimum(jnp.sum(starts[:, None] >= pad_end[None, :], axis=-1), ne - 1)
    n_used = (pad_end[-1] // bm).astype(jnp.int32).reshape(1)
    return (pos_tk.reshape(t, TOP_K), pos, runs.reshape(n_d, 1, -1), pad_slots, block_e.astype(jnp.int32),
            n_used, n_blocks)


def _dispatch_kernel(runs_ref, prev_ref, pad_ref, pos_ref, h2_ref, x_hbm, sbuf, sem):
    sb = sbuf.shape[1] // SUB
    dt = h2_ref.shape[0]
    i = pl.program_id(0)
    cur = lax.rem(i, 2)
    rows = lax.broadcasted_iota(jnp.int32, (sb, dt), 0)
    hit = rows == pos_ref[0, 0:1, :]
    for k in range(1, TOP_K):
        hit = hit | (rows == pos_ref[0, k:k + 1, :])
    onehot = jnp.where(hit, 1.0, 0.0).astype(BF16)
    _store_token_tiled(sbuf.at[cur], jnp.dot(onehot, h2_ref[...], preferred_element_type=F32))

    def run_copy(ref, buf, r):
        src = sbuf.at[buf, pl.ds(pl.multiple_of(ref[0, 0, r] * SUB, SUB), RUN * SUB)]
        dst = x_hbm.at[pl.ds(pl.multiple_of(ref[0, 0, MAX_RUNS + r] * SUB, SUB), RUN * SUB)]
        return pltpu.make_async_copy(src, dst, sem)

    def drain(ref, buf):
        def wait(p, carry):
            for q in range(2):
                run_copy(ref, buf, 2 * p + q).wait()
            return carry
        lax.fori_loop(0, ref[0, 0, 2 * MAX_RUNS], wait, 0)

    @pl.when(i > 0)
    def _():
        drain(prev_ref, 1 - cur)

    def start(p, carry):
        for q in range(2):
            run_copy(runs_ref, cur, 2 * p + q).start(priority=q)
        return carry
    lax.fori_loop(0, runs_ref[0, 0, 2 * MAX_RUNS], start, 0)

    @pl.when(i == 0)
    def _():
        n_pad = pad_ref.shape[0]

        def pad_copy(j):
            return pltpu.make_async_copy(sbuf.at[0, _tile_rows(0)], x_hbm.at[_tile_rows(pad_ref[j])], sem)

        def pad_start(g, carry):
            for u in range(DMA_UNROLL):
                pad_copy(g * DMA_UNROLL + u).start(priority=u % 2)
            return carry
        lax.fori_loop(0, n_pad // DMA_UNROLL, pad_start, 0)

        def pad_wait(j, carry):
            pad_copy(j).wait()
            return carry
        lax.fori_loop(0, n_pad, pad_wait, 0)

    @pl.when(i == pl.num_programs(0) - 1)
    def _():
        drain(runs_ref, cur)


def _dispatch(h2, pos, runs, pad_slots, n_slots):
    t, d = h2.shape
    dt = DISPATCH_TILE
    sb = dt * TOP_K + N_EXPERTS * RUN
    return pl.pallas_call(
        _dispatch_kernel,
        grid=(t // dt,),
        in_specs=[
            pl.BlockSpec((1, 1, runs.shape[2]), lambda i: (i, 0, 0), memory_space=pltpu.SMEM),
            pl.BlockSpec((1, 1, runs.shape[2]), lambda i: (jnp.maximum(i - 1, 0), 0, 0),
                         memory_space=pltpu.SMEM),
            pl.BlockSpec(memory_space=pltpu.SMEM),
            pl.BlockSpec((1, TOP_K, dt), lambda i: (i, 0, 0)),
            pl.BlockSpec((dt, d), lambda i: (i, 0)),
        ],
        out_specs=pl.BlockSpec(memory_space=pl.ANY),
        out_shape=jax.ShapeDtypeStruct((n_slots * SUB, LANE), F32),
        scratch_shapes=[pltpu.VMEM((2, sb * SUB, LANE), F32), pltpu.SemaphoreType.DMA(())],
        compiler_params=_params("arbitrary"),
        name="dispatch",
    )(runs, runs, pad_slots, pos, h2)


def _wprep_kernel(w_ref, eo_ref, o_ref):
    half = o_ref.shape[2] // 2
    two = 2 * LANE
    for c in range(w_ref.shape[2] // two):
        blk = w_ref[0, :, c * two:(c + 1) * two].astype(BF16)
        r = jnp.dot(blk, eo_ref[...], preferred_element_type=F32)
        o_ref[0, :, c * LANE:(c + 1) * LANE] = r[:, :LANE].astype(BF16)
        o_ref[0, :, half + c * LANE:half + (c + 1) * LANE] = r[:, LANE:].astype(BF16)


def _wprep(w):
    ne, d, f2 = w.shape
    eo = np.zeros((2 * LANE, 2 * LANE))
    eo[2 * np.arange(LANE), np.arange(LANE)] = 1.0
    eo[2 * np.arange(LANE) + 1, LANE + np.arange(LANE)] = 1.0
    return pl.pallas_call(
        _wprep_kernel,
        grid=(ne,),
        in_specs=[pl.BlockSpec((1, d, f2), lambda e: (e, 0, 0)),
                  pl.BlockSpec((2 * LANE, 2 * LANE), lambda e: (0, 0))],
        out_specs=pl.BlockSpec((1, d, f2), lambda e: (e, 0, 0)),
        out_shape=jax.ShapeDtypeStruct(w.shape, BF16),
        compiler_params=_params("arbitrary"),
        name="expert_weight_layout",
    )(w, jnp.asarray(eo, BF16))


def _tile_rows(r):
    return pl.ds(pl.multiple_of(r * SUB, SUB), SUB)


def _store_token_tiled(ref, val):
    n = val.shape[0]
    for c in range(SUB):
        ref[pl.ds(c, n, stride=SUB), :] = val[:, c * LANE:(c + 1) * LANE]


def _load_token_tiled(ref, n):
    return jnp.concatenate([ref[pl.ds(c, n, stride=SUB), :] for c in range(SUB)], axis=1)


def _ffn_kernel(be_ref, nu_ref, x_ref, wg_ref, wu_ref, wd_ref, bg_ref, bu_ref, bd_ref, y_ref):
    bm = y_ref.shape[0] // SUB
    b = pl.program_id(0)
    n_used = nu_ref[0]

    @pl.when(b < n_used)
    def _():
        x = _load_token_tiled(x_ref, bm).astype(BF16)
        hg = jnp.dot(x, wg_ref[0], preferred_element_type=F32) + bg_ref[0]
        hu = jnp.dot(x, wu_ref[0], preferred_element_type=F32) + bu_ref[0]
        glu = jnp.minimum(hg, SWIGLU_LIMIT)
        lin = jnp.clip(hu, -SWIGLU_LIMIT, SWIGLU_LIMIT)
        act = glu * _sigmoid(SWIGLU_ALPHA * glu) * (lin + 1.0)
        y = jnp.dot(act.astype(BF16), wd_ref[0], preferred_element_type=F32) + bd_ref[0]
        _store_token_tiled(y_ref, y)

    @pl.when(b >= n_used)
    def _():
        y_ref[...] = jnp.zeros_like(y_ref)


def _expert_ffn(x_slots, block_e, n_used, wgu, wd, bg, bu, bd):
    bm = FFN_BM
    d, f = wd.shape[2], wd.shape[1]
    assert d == SUB * LANE
    nb = block_e.shape[0]
    wspec = lambda a: pl.BlockSpec((1,) + a.shape[1:], lambda b, be, nu: (be[b], 0, 0))
    wg_spec = pl.BlockSpec((1, d, f), lambda b, be, nu: (be[b], 0, 0))
    wu_spec = pl.BlockSpec((1, d, f), lambda b, be, nu: (be[b], 0, 1))
    rows = pl.BlockSpec((bm * SUB, LANE), lambda b, be, nu: (b, 0))
    return pl.pallas_call(
        _ffn_kernel,
        grid_spec=pltpu.PrefetchScalarGridSpec(
            num_scalar_prefetch=2,
            grid=(nb,),
            in_specs=[rows, wg_spec, wu_spec, wspec(wd), wspec(bg), wspec(bu), wspec(bd)],
            out_specs=rows,
        ),
        out_shape=jax.ShapeDtypeStruct((nb * bm * SUB, LANE), F32),
        compiler_params=_params("arbitrary"),
        name="expert_ffn",
    )(block_e, n_used, x_slots, wgu, wgu, wd, bg, bu, bd)


def _combine_kernel(npb, runs_ref, next_ref, pos_ref, gate_ref, x1_ref, gf_ref, y_hbm, yp_ref, ys_ref,
                    ybuf, sem):
    sb = ybuf.shape[1] // SUB
    i = pl.program_id(0)
    n = pl.num_programs(0)
    cur = lax.rem(i, 2)

    def run_copy(ref, buf, r):
        src = y_hbm.at[pl.ds(pl.multiple_of(ref[0, 0, MAX_RUNS + r] * SUB, SUB), RUN * SUB)]
        dst = ybuf.at[buf, pl.ds(pl.multiple_of(ref[0, 0, r] * SUB, SUB), RUN * SUB)]
        return pltpu.make_async_copy(src, dst, sem.at[buf])

    def issue(ref, buf):
        def start(p, carry):
            for q in range(2):
                run_copy(ref, buf, 2 * p + q).start(priority=q)
            return carry
        lax.fori_loop(0, ref[0, 0, 2 * MAX_RUNS], start, 0)

    @pl.when(i == 0)
    def _():
        ybuf[...] = jnp.zeros_like(ybuf)
        issue(runs_ref, 0)

    @pl.when(i + 1 < n)
    def _():
        issue(next_ref, 1 - cur)

    def wait(p, carry):
        for q in range(2):
            run_copy(runs_ref, cur, 2 * p + q).wait()
        return carry
    lax.fori_loop(0, runs_ref[0, 0, 2 * MAX_RUNS], wait, 0)

    gate = gate_ref[...]
    pos = pos_ref[...]
    col = lax.broadcasted_iota(jnp.int32, (gate.shape[0], sb), 1)
    weight = jnp.zeros(col.shape, F32)
    for k in range(TOP_K):
        weight = jnp.where(col == pos[:, k:k + 1], gate[:, k:k + 1], weight)
    ysorted = _load_token_tiled(ybuf.at[cur], sb).astype(BF16)
    y = jnp.dot(weight.astype(BF16), ysorted, preferred_element_type=F32)
    out = _rms(x1_ref[...] + y, gf_ref[...])

    @pl.when(i < npb)
    def _():
        yp_ref[...] = out

    @pl.when(i >= npb)
    def _():
        ys_ref[...] = out


def _combine(y_slots, runs, pos_tk, gate, x1, gf, npt):
    t, d = x1.shape
    tm = DISPATCH_TILE
    nt = t // tm
    npb = npt // tm
    sb = tm * TOP_K + N_EXPERTS * RUN
    smem_blk = lambda f: pl.BlockSpec((1, 1, runs.shape[2]), f, memory_space=pltpu.SMEM)
    return pl.pallas_call(
        functools.partial(_combine_kernel, npb),
        grid=(nt,),
        in_specs=[
            smem_blk(lambda i: (i, 0, 0)),
            smem_blk(lambda i: (jnp.minimum(i + 1, nt - 1), 0, 0)),
            pl.BlockSpec((tm, TOP_K), lambda i: (i, 0)),
            pl.BlockSpec((tm, TOP_K), lambda i: (i, 0)),
            pl.BlockSpec((tm, d), lambda i: (i, 0)),
            pl.BlockSpec((1, d), lambda i: (0, 0)),
            pl.BlockSpec(memory_space=pl.ANY),
        ],
        out_specs=[
            pl.BlockSpec((tm, d), lambda i: (jnp.minimum(i, npb - 1), 0)),
            pl.BlockSpec((tm, d), lambda i: (jnp.maximum(i - npb, 0), 0)),
        ],
        out_shape=[jax.ShapeDtypeStruct((npt, d), F32), jax.ShapeDtypeStruct((t - npt, d), F32)],
        scratch_shapes=[pltpu.VMEM((2, sb * SUB, LANE), F32), pltpu.SemaphoreType.DMA((2,))],
        compiler_params=_params("arbitrary"),
        name="combine_norm",
    )(runs, runs, pos_tk, gate, x1, gf, y_slots)


def _inproj_columns(d):
    da_q, db = HA_Q * HEAD_DIM, HB * HEAD_DIM
    da_kv = HA_KV * HEAD_DIM
    q_b_end = da_q + db
    g_b_end = q_b_end + 2 * d
    ka0 = g_b_end
    va0 = ka0 + da_kv
    kb0 = va0 + da_kv
    vb0 = kb0 + db
    dup = lambda c0: np.concatenate(
        [np.tile(np.arange(c0 + h * HEAD_DIM, c0 + (h + 1) * HEAD_DIM), 2) for h in range(HA_KV)])
    cols = np.concatenate([np.arange(0, g_b_end), dup(ka0), dup(va0), np.arange(kb0, vb0 + db)])
    segs = ((0, da_q), (da_q, db), (q_b_end, 2 * d), (g_b_end, 2 * da_kv),
            (g_b_end + 2 * da_kv, 2 * da_kv), (g_b_end + 4 * da_kv, db), (g_b_end + 4 * da_kv + db, db))
    return cols, segs


def kernel(x_prompt, x_sample, meta_tokens, norm1_g, w_in, attn_sink, rel_pos_bias, w_branch_a,
           w_branch_b, w_out, norm2_g, router_w, router_b, w_gate_up, b_gate_up, w_down, b_down,
           final_norm_g):
    assert norm1_g.shape[0] == 1, "single-layer trunk"
    bp, lp, d = x_prompt.shape
    bs, ls, _ = x_sample.shape
    npt, nst = bp * lp, bs * ls
    assert npt % ROW_TILE == 0 and nst % ROW_TILE == 0
    xp = x_prompt.reshape(npt, d)
    xs = x_sample.reshape(nst, d)

    cols, segs = _inproj_columns(d)
    w_p = w_in[0][:, cols].astype(BF16)
    g1 = norm1_g[0].reshape(1, d)
    qa, qb, gates, ka2, va2, kb, vb = _inproj(xp, xs, g1, w_p, segs)
    kv0 = segs[3][0]
    meta_kv = _meta_proj(meta_tokens, g1, w_p[:, kv0:])
    w_kv = 2 * HA_KV * HEAD_DIM
    meta_kv = _pad_meta(meta_kv)
    km2, vm2 = meta_kv[:, :w_kv], meta_kv[:, w_kv:2 * w_kv]
    kbm, vbm = meta_kv[:, 2 * w_kv:2 * w_kv + HB * HEAD_DIM], meta_kv[:, 2 * w_kv + HB * HEAD_DIM:]

    oa = _window_attention(qa, ka2, va2, km2, vm2, attn_sink[0].astype(F32), npt, lp, ls)
    ob = _nbr_attention(qb, kb, vb, kbm, vbm, rel_pos_bias[0], npt, lp, ls)

    rw = router_w[0].astype(F32)
    rw_top = _bf16_part(rw)
    rw_hi = rw_top.astype(BF16)
    rw_lo = (rw - rw_top).astype(BF16)
    lane_pad = lambda a, fill: jnp.pad(a, ((0, 0), (0, LANE - a.shape[1])), constant_values=fill)
    rw2 = jnp.concatenate([lane_pad(rw_hi, 0), lane_pad(rw_lo, 0)], axis=1)
    rb = lane_pad(router_b[0].reshape(1, -1).astype(F32), -jnp.inf)
    x1, h2, top_idx, gate, rank, counts, cstart = _merge(
        oa, ob, gates, xp, xs, w_branch_a[0].astype(BF16), w_branch_b[0].astype(BF16),
        w_out[0].astype(BF16), norm2_g[0].reshape(1, d), rw2, rb)

    pos_tk, pos, runs, pad_slots, block_e, n_used, n_blocks = _route(top_idx, rank, counts, cstart, FFN_BM)
    x_slots = _dispatch(h2, pos, runs, pad_slots, (n_blocks + N_EXPERTS) * FFN_BM)
    bgu = b_gate_up[0]
    y_slots = _expert_ffn(
        x_slots, block_e, n_used, _wprep(w_gate_up[0]), w_down[0].astype(BF16),
        bgu[:, None, 0::2], bgu[:, None, 1::2], b_down[0][:, None, :])

    yp, ys = _combine(y_slots, runs, pos_tk, gate, x1, final_norm_g.reshape(1, d), npt)
    return yp.reshape(bp, lp, d), ys.reshape(bs, ls, d)
```
